```python
import jax, jax.numpy as jnp
from jax import lax
import numpy as np


D_MODEL = 2048
BATCH = 4
SEQ = 2048
DEPTH = 4

GRID_W = 64
CTX_LEN = 256
F32 = jnp.float32

N_BRANCH = 4
RMS_EPS = 1e-6
CONV_DIM = 512
CONV_WIDTH = 31
LN_EPS = 1e-5
SSD_HEADS = 12
SSD_HEAD_DIM = 64
SSD_DIM = SSD_HEADS * SSD_HEAD_DIM
SSD_GROUPS = 4
SSD_STATE = 128
SSD_CONV = 5
SSD_CHUNK = 128
SSD_XBC = SSD_DIM + 2 * SSD_GROUPS * SSD_STATE
SSD_IN = SSD_DIM + SSD_XBC + 2 * SSD_HEADS
FOURIER_GROUPS = 4
FOURIER_GROUP_DIM = 128
FOURIER_DIM = FOURIER_GROUPS * FOURIER_GROUP_DIM
RWKV_HEADS = 8
RWKV_HEAD_DIM = 64
RWKV_DIM = RWKV_HEADS * RWKV_HEAD_DIM
DECAY_LORA = 64
ICL_LORA = 64
GATE_LORA = 128
GN_EPS = 64e-5
RWKV_IN = 3 * RWKV_DIM + 2 * DECAY_LORA + ICL_LORA + GATE_LORA
RWKV_OFFSETS = [RWKV_DIM, 2 * RWKV_DIM, 3 * RWKV_DIM, 3 * RWKV_DIM + DECAY_LORA, 3 * RWKV_DIM + 2 * DECAY_LORA, 3 * RWKV_DIM + 2 * DECAY_LORA + ICL_LORA]
REC_IN = SSD_IN + RWKV_IN
IN_OFFSETS = [SSD_IN, REC_IN, REC_IN + 2 * CONV_DIM, REC_IN + 2 * CONV_DIM + FOURIER_DIM]
IN_DIM = REC_IN + 2 * CONV_DIM + FOURIER_DIM + N_BRANCH * D_MODEL
D_FF = 4 * D_MODEL

kernel_name = 'hybrid_gated_conv_ssd_fourier_rwkv_dit'


def rms_norm(v, g):
    vf = v.astype(F32)
    y = vf * lax.rsqrt(jnp.mean(vf * vf, axis=-1, keepdims=True) + RMS_EPS)
    return (y * g.astype(F32)).astype(v.dtype)


def layer_norm(v, g, b, eps):
    vf = v.astype(F32)
    mu = jnp.mean(vf, axis=-1, keepdims=True)
    var = jnp.mean(jnp.square(vf - mu), axis=-1, keepdims=True)
    return ((vf - mu) * lax.rsqrt(var + eps) * g.astype(F32) + b.astype(F32)).astype(v.dtype)


def _flip(t):
    return jnp.flip(t, axis=1)


def dwconv1d(v, w, bias):
    k, ch = w.shape
    y = lax.conv_general_dilated(v, w.astype(v.dtype)[:, None, :], (1,), [((k - 1) // 2, k // 2)],
                                 dimension_numbers=('NWC', 'WIO', 'NWC'), feature_group_count=ch)
    return y + bias.astype(v.dtype)


def token_shift(v):
    z = jnp.zeros_like(v[:, :1])
    prev = jnp.concatenate([z, v[:, :-1]], axis=1)
    nxt = jnp.concatenate([v[:, 1:], z], axis=1)
    return 0.5 * (prev + nxt) - v


def to_cols(t, rows):
    b, l, ch = t.shape
    return t.reshape(b, rows, GRID_W, ch).transpose(0, 2, 1, 3).reshape(b, l, ch)


def to_rows(t, rows):
    b, l, ch = t.shape
    return t.reshape(b, GRID_W, rows, ch).transpose(0, 2, 1, 3).reshape(b, l, ch)


def segsum(a):
    t = a.shape[-1]
    cs = jnp.cumsum(a, axis=-1)
    diff = cs[..., :, None] - cs[..., None, :]
    return jnp.where(jnp.tril(jnp.ones((t, t), dtype=bool)), diff, -jnp.inf)


def ssd_scan(xs, dt, A, Bm, Cm, h0, need_y):
    b, l, nh, p = xs.shape
    g, n = Bm.shape[2], Bm.shape[3]
    r = nh // g
    c, q = l // SSD_CHUNK, SSD_CHUNK
    xd = (xs * dt[..., None]).reshape(b, c, q, g, r, p)
    a = (dt * A).reshape(b, c, q, g, r)
    a_cs = jnp.cumsum(a, axis=2)
    Bc = Bm.reshape(b, c, q, g, n)
    decay_s = jnp.exp(a_cs[:, :, -1:] - a_cs)
    states = jnp.einsum('bcsgn,bcsgrp->bcgrpn', Bc, xd * decay_s[..., None])
    states = jnp.concatenate([h0.reshape(b, 1, g, r, p, n), states], axis=1)
    chunk_tot = jnp.pad(a_cs[:, :, -1], ((0, 0), (1, 0), (0, 0), (0, 0))).transpose(0, 2, 3, 1)
    decay_c = jnp.exp(segsum(chunk_tot))
    states = jnp.einsum('bgrzc,bcgrpn->bzgrpn', decay_c, states)
    final = states[:, -1].reshape(b, nh, p, n)
    if not need_y:
        return None, final
    Cc = Cm.reshape(b, c, q, g, n)
    L = jnp.exp(segsum(a.transpose(0, 3, 4, 1, 2)))
    CB = jnp.einsum('bclgn,bcsgn->bgcls', Cc, Bc)
    y_diag = jnp.einsum('bgrcls,bcsgrp->bclgrp', CB[:, :, None] * L, xd)
    y_off = jnp.einsum('bclgn,bcgrpn->bclgrp', Cc, states[:, :-1]) * jnp.exp(a_cs)[..., None]
    return (y_diag + y_off).reshape(b, l, nh, p), final


def ssd_branch(u, uc, ctx_out, conv_w, conv_b, A_log, dt_bias, D_skip, norm_g, w_out):
    A = -jnp.exp(A_log.astype(F32))
    dtb = dt_bias.astype(F32)
    Dsk = D_skip.astype(F32)[:, None]

    def prep(v):
        b, l, _ = v.shape
        z, xbc, dt = jnp.split(v, [SSD_DIM, SSD_DIM + SSD_XBC], axis=-1)
        xbc = jax.nn.silu(dwconv1d(xbc, conv_w, conv_b)).astype(F32)
        xs, Bm, Cm = jnp.split(xbc, [SSD_DIM, SSD_DIM + SSD_GROUPS * SSD_STATE], axis=-1)
        xs = xs.reshape(b, l, SSD_HEADS, SSD_HEAD_DIM)
        Bm = Bm.reshape(b, l, SSD_GROUPS, SSD_STATE)
        Cm = Cm.reshape(b, l, SSD_GROUPS, SSD_STATE)
        dt = jax.nn.softplus(dt.astype(F32).reshape(b, l, 2, SSD_HEADS) + dtb)
        return z, xs, Bm, Cm, dt

    def run(v, h0f, h0b, need_y):
        z, xs, Bm, Cm, dt = prep(v)
        yf, sf = ssd_scan(xs, dt[:, :, 0], A[0], Bm, Cm, h0f, need_y)
        yb, sb = ssd_scan(_flip(xs), _flip(dt[:, :, 1]), A[1], _flip(Bm), _flip(Cm), h0b, need_y)
        if not need_y:
            return None, sf, sb
        y = yf + _flip(yb) + Dsk * xs
        y = y.reshape(z.shape).astype(z.dtype) * jax.nn.silu(z)
        return rms_norm(y, norm_g) @ w_out, sf, sb

    h0 = jnp.zeros((u.shape[0], SSD_HEADS, SSD_HEAD_DIM, SSD_STATE), F32)
    yc, sf, sb = run(uc, h0, h0, ctx_out)
    y, _, _ = run(u, sf, sb, True)
    return y, yc


def rwkv7_scan(r, w, k, v, kk, ka, S0, readout):
    def step(S, inp):
        r_t, w_t, k_t, v_t, kk_t, ka_t = inp
        sa = jnp.einsum('bhvk,bhk->bhv', S, -kk_t)
        S = S * w_t[:, :, None, :] + sa[..., None] * ka_t[:, :, None, :] + v_t[..., None] * k_t[:, :, None, :]
        y = jnp.einsum('bhvk,bhk->bhv', S, r_t) if readout else None
        return S, y
    seq = tuple(jnp.moveaxis(t, 1, 0) for t in (r, w, k, v, kk, ka))
    S, ys = lax.scan(step, S0, seq)
    return (jnp.moveaxis(ys, 0, 1) if readout else None), S


def rwkv_branch(u, uc, rows, ctx_out, mu, w0, w2, a0, a2, g2, k_k, k_a, r_k, ln_g, ln_b, w_out):
    def prep(v):
        b, l, _ = v.shape
        v = v + token_shift(v) * mu
        r, k, vv, wf, wb, al, gl = jnp.split(v, RWKV_OFFSETS, axis=-1)
        heads = lambda t: t.astype(F32).reshape(b, l, RWKV_HEADS, RWKV_HEAD_DIM)

        def decay(lo, d):
            wl = -jax.nn.softplus(-(w0[d] + jnp.tanh(lo) @ w2[d])) - 0.5
            return heads(jnp.exp(-jnp.exp(wl.astype(F32))))
        a = jax.nn.sigmoid(a0 + al @ a2)
        kk = heads(k * k_k)
        kk = kk * lax.rsqrt(jnp.sum(kk * kk, axis=-1, keepdims=True) + 1e-12)
        kmod = heads(k * (1 + (a - 1) * k_a))
        g = jax.nn.sigmoid(gl) @ g2
        return heads(r), decay(wf, 0), decay(wb, 1), kmod, heads(vv), kk, kk * heads(a), g

    def run(v, S0f, S0b, need_y):
        b, l, _ = v.shape
        r, wf, wb, k, vv, kk, ka, g = prep(v)
        yf, Sf = rwkv7_scan(r, wf, k, vv, kk, ka, S0f, need_y)
        yb, Sb = rwkv7_scan(*(_flip(t) for t in (r, wb, k, vv, kk, ka)), S0b, need_y)
        if not need_y:
            return None, Sf, Sb
        y = yf + _flip(yb)
        m = jnp.mean(y, axis=-1, keepdims=True)
        var = jnp.mean(jnp.square(y - m), axis=-1, keepdims=True)
        y = ((y - m) * lax.rsqrt(var + GN_EPS)).reshape(b, l, RWKV_DIM) * ln_g + ln_b
        bonus = (jnp.sum(r * k * r_k, axis=-1, keepdims=True) * vv).reshape(b, l, RWKV_DIM)
        return ((y + bonus) * g).astype(v.dtype), Sf, Sb

    S0 = jnp.zeros((u.shape[0], RWKV_HEADS, RWKV_HEAD_DIM, RWKV_HEAD_DIM), F32)
    oc, Sf, Sb = run(uc, S0, S0, ctx_out)
    o, _, _ = run(to_cols(u, rows), Sf, Sb, True)
    y = to_rows(o, rows) @ w_out
    yc = oc @ w_out if ctx_out else None
    return y, yc


def conformer_branch(u, seg, conv_w, conv_b, ln_g, ln_b, w_out):
    b, l, _ = u.shape
    val, gate = jnp.split(u, 2, axis=-1)
    v = (val * jax.nn.sigmoid(gate)).reshape(-1, seg, CONV_DIM)
    v = dwconv1d(v, conv_w, conv_b).reshape(b, l, CONV_DIM)
    v = layer_norm(v, ln_g, ln_b, LN_EPS)
    return jax.nn.silu(v) @ w_out


def fourier_branch(u, w_out):
    b, l, _ = u.shape
    uf = u.astype(F32).reshape(b, l, FOURIER_GROUPS, FOURIER_GROUP_DIM)
    y = jnp.fft.fftn(uf, axes=(1, 3), norm='ortho').real
    return y.reshape(b, l, FOURIER_DIM).astype(u.dtype) @ w_out


def gated_merge(gate_pre, branches, w_o):
    b, l, _ = gate_pre.shape
    gates = jax.nn.sigmoid(gate_pre).reshape(b, l, N_BRANCH, D_MODEL)
    m = gates[:, :, 0] * branches[0]
    for i in range(1, N_BRANCH):
        m = m + gates[:, :, i] * branches[i]
    return m @ w_o


def hybrid_mixer(h, hc, rows, ctx_out, w_in, conv_p, ssd_p, fourier_out, rwkv_p, w_o):
    u = h @ w_in
    uc = hc @ (w_in if ctx_out else w_in[:, :REC_IN])
    u_ssd, u_rwkv, u_conv, u_fft, u_gate = jnp.split(u, IN_OFFSETS, axis=-1)
    uc_ssd, uc_rwkv = uc[..., :SSD_IN], uc[..., SSD_IN:REC_IN]
    p_ssd, pc_ssd = ssd_branch(u_ssd, uc_ssd, ctx_out, *ssd_p)
    p_rwkv, pc_rwkv = rwkv_branch(u_rwkv, uc_rwkv, rows, ctx_out, *rwkv_p)
    p_conv = conformer_branch(u_conv, GRID_W, *conv_p)
    p_fft = fourier_branch(u_fft, fourier_out)
    y = gated_merge(u_gate, (p_conv, p_ssd, p_fft, p_rwkv), w_o)
    if not ctx_out:
        return y, None
    uc_conv, uc_fft, uc_gate = uc[..., IN_OFFSETS[1]:IN_OFFSETS[2]], uc[..., IN_OFFSETS[2]:IN_OFFSETS[3]], uc[..., IN_OFFSETS[3]:]
    pc_conv = conformer_branch(uc_conv, uc_conv.shape[1], *conv_p)
    pc_fft = fourier_branch(uc_fft, fourier_out)
    yc = gated_merge(uc_gate, (pc_conv, pc_ssd, pc_fft, pc_rwkv), w_o)
    return y, yc


def sq_relu_mlp(h, w_up, w_down):
    return jnp.square(jax.nn.relu(h @ w_up)) @ w_down


def setup_inputs(seed: int = 0) -> dict:
    key = jax.random.key(seed)
    ks = iter(jax.random.split(key, 48))
    nrm = lambda shape, scale: jax.random.normal(next(ks), shape, F32) * scale
    L, D = DEPTH, D_MODEL
    x = nrm((BATCH, SEQ, D), 1.0)
    c = nrm((BATCH, D), 1.0)
    ctx = nrm((BATCH, CTX_LEN, D), 1.0)
    c_ctx = nrm((D,), 1.0)
    mod_w = nrm((L, D, 6 * D), 0.5 * D ** -0.5)
    mod_b = nrm((L, 6 * D), 0.02)
    norm_g = 1.0 + nrm((L, 4, D), 0.02)
    w_in = nrm((L, D, IN_DIM), D ** -0.5)
    conv_w = nrm((L, CONV_WIDTH, CONV_DIM), CONV_WIDTH ** -0.5)
    conv_b = nrm((L, CONV_DIM), 0.02)
    conv_ln_g = 1.0 + nrm((L, CONV_DIM), 0.02)
    conv_ln_b = nrm((L, CONV_DIM), 0.02)
    conv_out = nrm((L, CONV_DIM, D), CONV_DIM ** -0.5)
    ssd_conv_w = nrm((L, SSD_CONV, SSD_XBC), SSD_CONV ** -0.5)
    ssd_conv_b = nrm((L, SSD_XBC), 0.02)
    ssd_A_log = jnp.log(jax.random.uniform(next(ks), (L, 2, SSD_HEADS), F32, 1.0, 16.0))
    dt0 = jnp.exp(jax.random.uniform(next(ks), (L, 2, SSD_HEADS), F32, float(np.log(1e-3)), float(np.log(1e-1))))
    ssd_dt_bias = dt0 + jnp.log(-jnp.expm1(-dt0))
    ssd_D = 1.0 + nrm((L, SSD_HEADS), 0.1)
    ssd_norm_g = 1.0 + nrm((L, SSD_DIM), 0.02)
    ssd_out = nrm((L, SSD_DIM, D), SSD_DIM ** -0.5)
    fourier_out = nrm((L, FOURIER_DIM, D), FOURIER_DIM ** -0.5)
    rwkv_mu = jax.random.uniform(next(ks), (L, RWKV_IN), F32, 0.0, 1.0)
    rwkv_w0 = nrm((L, 2, RWKV_DIM), 0.5)
    rwkv_w2 = nrm((L, 2, DECAY_LORA, RWKV_DIM), 0.1 * DECAY_LORA ** -0.5)
    rwkv_a0 = nrm((L, RWKV_DIM), 0.1)
    rwkv_a2 = nrm((L, ICL_LORA, RWKV_DIM), 0.1 * ICL_LORA ** -0.5)
    rwkv_g2 = nrm((L, GATE_LORA, RWKV_DIM), GATE_LORA ** -0.5)
    rwkv_k_k = 0.85 + nrm((L, RWKV_DIM), 0.02)
    rwkv_k_a = 1.0 + nrm((L, RWKV_DIM), 0.02)
    rwkv_r_k = nrm((L, RWKV_HEADS, RWKV_HEAD_DIM), 0.1)
    rwkv_ln_g = 1.0 + nrm((L, RWKV_DIM), 0.02)
    rwkv_ln_b = nrm((L, RWKV_DIM), 0.02)
    rwkv_out = nrm((L, RWKV_DIM, D), RWKV_DIM ** -0.5)
    w_o = nrm((L, D, D), D ** -0.5)
    mlp_up = nrm((L, D, D_FF), D ** -0.5)
    mlp_down = nrm((L, D_FF, D), D_FF ** -0.5)
    return {'x': x, 'c': c, 'ctx': ctx, 'c_ctx': c_ctx, 'mod_w': mod_w, 'mod_b': mod_b,
            'norm_g': norm_g, 'w_in': w_in,
            'conv_w': conv_w, 'conv_b': conv_b, 'conv_ln_g': conv_ln_g, 'conv_ln_b': conv_ln_b, 'conv_out': conv_out,
            'ssd_conv_w': ssd_conv_w, 'ssd_conv_b': ssd_conv_b, 'ssd_A_log': ssd_A_log, 'ssd_dt_bias': ssd_dt_bias,
            'ssd_D': ssd_D, 'ssd_norm_g': ssd_norm_g, 'ssd_out': ssd_out,
            'fourier_out': fourier_out,
            'rwkv_mu': rwkv_mu, 'rwkv_w0': rwkv_w0, 'rwkv_w2': rwkv_w2, 'rwkv_a0': rwkv_a0, 'rwkv_a2': rwkv_a2,
            'rwkv_g2': rwkv_g2, 'rwkv_k_k': rwkv_k_k, 'rwkv_k_a': rwkv_k_a, 'rwkv_r_k': rwkv_r_k,
            'rwkv_ln_g': rwkv_ln_g, 'rwkv_ln_b': rwkv_ln_b, 'rwkv_out': rwkv_out,
            'w_o': w_o, 'mlp_up': mlp_up, 'mlp_down': mlp_down}


def reference(x, c, ctx, c_ctx, mod_w, mod_b, norm_g, w_in,
              conv_w, conv_b, conv_ln_g, conv_ln_b, conv_out,
              ssd_conv_w, ssd_conv_b, ssd_A_log, ssd_dt_bias, ssd_D, ssd_norm_g, ssd_out,
              fourier_out,
              rwkv_mu, rwkv_w0, rwkv_w2, rwkv_a0, rwkv_a2, rwkv_g2, rwkv_k_k, rwkv_k_a, rwkv_r_k,
              rwkv_ln_g, rwkv_ln_b, rwkv_out,
              w_o, mlp_up, mlp_down):
    rows = x.shape[1] // GRID_W
    xc = ctx
    silu_c = jax.nn.silu(c)
    silu_cc = jax.nn.silu(c_ctx)
    for i in range(DEPTH):
        last = i == DEPTH - 1
        mod = silu_c @ mod_w[i] + mod_b[i]
        modc = silu_cc @ mod_w[i] + mod_b[i]
        sh1, sc1, g1, sh2, sc2, g2 = jnp.split(mod[:, None, :], 6, axis=-1)
        csh1, csc1, cg1, csh2, csc2, cg2 = jnp.split(modc, 6, axis=-1)
        conv_p = (conv_w[i], conv_b[i], conv_ln_g[i], conv_ln_b[i], conv_out[i])
        ssd_p = (ssd_conv_w[i], ssd_conv_b[i], ssd_A_log[i], ssd_dt_bias[i], ssd_D[i], ssd_norm_g[i], ssd_out[i])
        rwkv_p = (rwkv_mu[i], rwkv_w0[i], rwkv_w2[i], rwkv_a0[i], rwkv_a2[i], rwkv_g2[i], rwkv_k_k[i],
                  rwkv_k_a[i], rwkv_r_k[i], rwkv_ln_g[i], rwkv_ln_b[i], rwkv_out[i])
        h = rms_norm(x, norm_g[i, 0]) * (1 + sc1) + sh1
        hc = rms_norm(xc, norm_g[i, 0]) * (1 + csc1) + csh1
        y, yc = hybrid_mixer(h, hc, rows, not last, w_in[i], conv_p, ssd_p, fourier_out[i], rwkv_p, w_o[i])
        x = x + g1 * rms_norm(y, norm_g[i, 1])
        h = rms_norm(x, norm_g[i, 2]) * (1 + sc2) + sh2
        x = x + g2 * rms_norm(sq_relu_mlp(h, mlp_up[i], mlp_down[i]), norm_g[i, 3])
        if not last:
            xc = xc + cg1 * rms_norm(yc, norm_g[i, 1])
            hc = rms_norm(xc, norm_g[i, 2]) * (1 + csc2) + csh2
            xc = xc + cg2 * rms_norm(sq_relu_mlp(hc, mlp_up[i], mlp_down[i]), norm_g[i, 3])
    return x
```

```python
import functools
import math

import jax
import jax.numpy as jnp
from jax import lax
from jax.experimental import pallas as pl
from jax.experimental.pallas import tpu as pltpu

F32 = jnp.float32
BF16 = jnp.bfloat16

GRID_W = 64
RMS_EPS = 1e-6
LN_EPS = 1e-5
GN_EPS = 64e-5
CONV_DIM = 512
CONV_WIDTH = 31
SSD_HEADS = 12
SSD_HEAD_DIM = 64
SSD_DIM = SSD_HEADS * SSD_HEAD_DIM
SSD_GROUPS = 4
SSD_HEADS_PER_GROUP = SSD_HEADS // SSD_GROUPS
SSD_STATE = 128
SSD_CONV = 5
SSD_BC = SSD_GROUPS * SSD_STATE
SSD_XBC = SSD_DIM + 2 * SSD_BC
SSD_IN = SSD_DIM + SSD_XBC + 2 * SSD_HEADS
SSD_CHUNK = 128
FOURIER_GROUPS = 4
FOURIER_GROUP_DIM = 128
FOURIER_DIM = FOURIER_GROUPS * FOURIER_GROUP_DIM
RWKV_HEADS = 8
RWKV_HEAD_DIM = 64
RWKV_DIM = RWKV_HEADS * RWKV_HEAD_DIM
RWKV_PAIRS = RWKV_HEADS // 2
DECAY_LORA = 64
ICL_LORA = 64
GATE_LORA = 128
RWKV_IN = 3 * RWKV_DIM + 2 * DECAY_LORA + ICL_LORA + GATE_LORA
RWKV_CHUNK = 64
REC_IN = SSD_IN + RWKV_IN
N_BRANCH = 4

LANE = 128
SUBLANE = 8
VMEM_LIMIT = 56 * 1024 * 1024

U_RWKV, U_RWKV_W = 0, 2048
U_CONV, U_CONV_W = 2048, 2 * CONV_DIM
U_FFT, U_FFT_W = 3072, FOURIER_DIM
U_XBC, U_XBC_W = 3584, SSD_XBC
U_Z, U_Z_W = 5376, SSD_DIM
U_DT, U_DT_W = 6144, LANE
U_N = 6272
RW_WF, RW_WB, RW_AL, RW_GL = 1536, 1664, 1792, 1920


def _params(*sem):
    return pltpu.CompilerParams(dimension_semantics=sem, vmem_limit_bytes=VMEM_LIMIT)


def _tile(n, cap, quantum=SUBLANE):
    if n <= cap:
        return n
    t = (cap // quantum) * quantum
    while t >= quantum:
        if n % t == 0:
            return t
        t -= quantum
    raise ValueError(f"no tile for {n} under {cap}")


def _const_spec(shape):
    nd = len(shape)
    return pl.BlockSpec(shape, lambda *_: (0,) * nd)


def _dot(a, b):
    return jnp.dot(a.astype(BF16), b.astype(BF16), preferred_element_type=F32)


def _dot_nt(a, b):
    return lax.dot_general(a.astype(BF16), b.astype(BF16), (((1,), (1,)), ((), ())),
                           preferred_element_type=F32)


def _split3(x):
    h1 = x.astype(BF16)
    r1 = x - h1.astype(F32)
    h2 = r1.astype(BF16)
    h3 = (r1 - h2.astype(F32)).astype(BF16)
    return h1, h2, h3


def _dot_exact_lhs(m, x):
    h1, h2, h3 = _split3(x)
    d = lambda h: jnp.dot(m, h, preferred_element_type=F32)
    return d(h1) + d(h2) + d(h3)


def _blocksum(x, bd):
    hi = x.astype(BF16)
    lo = (x - hi.astype(F32)).astype(BF16)
    return (jnp.dot(hi, bd, preferred_element_type=F32) + jnp.dot(lo, bd, preferred_element_type=F32))


def _sigmoid(x):
    return 1.0 / (1.0 + jnp.exp(-x))


def _softplus(x):
    return jnp.maximum(x, 0.0) + jnp.log1p(jnp.exp(-jnp.abs(x)))


def _rms(y, g):
    return y * lax.rsqrt(jnp.mean(y * y, axis=-1, keepdims=True) + RMS_EPS) * g


def _mod_kernel(a_ref, w_ref, b_ref, o_ref):
    a = a_ref[...]
    a = a * _sigmoid(a)
    o_ref[...] = _dot(a, w_ref[...]) + b_ref[...]


def _mod_all(cond, mod_w, mod_b):
    depth, d, n = mod_w.shape
    tn = _tile(n, 1024, LANE)
    return pl.pallas_call(
        _mod_kernel,
        grid=(depth, n // tn),
        in_specs=[_const_spec(cond.shape),
                  pl.BlockSpec((None, d, tn), lambda l, j: (l, 0, j)),
                  pl.BlockSpec((None, 1, tn), lambda l, j: (l, 0, j))],
        out_specs=pl.BlockSpec((None, cond.shape[0], tn), lambda l, j: (l, 0, j)),
        out_shape=jax.ShapeDtypeStruct((depth, cond.shape[0], n), F32),
        compiler_params=_params("parallel", "parallel"),
    )(cond, mod_w, mod_b.reshape(depth, 1, n))


def _norm_mod_kernel(x_ref, g_ref, sc_ref, sh_ref, o_ref):
    o_ref[...] = (_rms(x_ref[...], g_ref[...]) * (1.0 + sc_ref[...]) + sh_ref[...]).astype(o_ref.dtype)


def _norm_mod(x, g, sc, sh, m_rows, mod_idx, tm):
    d = x.shape[1]
    return pl.pallas_call(
        _norm_mod_kernel,
        grid=(m_rows // tm,),
        in_specs=[pl.BlockSpec((tm, d), lambda i: (i, 0)),
                  _const_spec((1, d)),
                  pl.BlockSpec((None, 1, d), lambda i: (mod_idx(i), 0, 0)),
                  pl.BlockSpec((None, 1, d), lambda i: (mod_idx(i), 0, 0))],
        out_specs=pl.BlockSpec((tm, d), lambda i: (i, 0)),
        out_shape=jax.ShapeDtypeStruct((m_rows, d), BF16),
        compiler_params=_params("parallel"),
    )(x, g, sc, sh)


def _mm_kernel(a_ref, w_ref, o_ref, *, epilogue):
    acc = _dot(a_ref[...], w_ref[...])
    if epilogue == "relu2":
        acc = jnp.square(jnp.maximum(acc, 0.0))
    elif epilogue == "sigmoid":
        acc = _sigmoid(acc)
    o_ref[...] = acc.astype(o_ref.dtype)


def _matmul(a, w, m_rows, tm, out_dtype, epilogue=None, a_col=0, tn_cap=1024):
    k, n = w.shape
    tn = _tile(n, tn_cap, LANE)
    return pl.pallas_call(
        functools.partial(_mm_kernel, epilogue=epilogue),
        grid=(m_rows // tm, n // tn),
        in_specs=[pl.BlockSpec((tm, k), lambda i, j: (i, a_col)),
                  pl.BlockSpec((k, tn), lambda i, j: (0, j))],
        out_specs=pl.BlockSpec((tm, tn), lambda i, j: (i, j)),
        out_shape=jax.ShapeDtypeStruct((m_rows, n), out_dtype),
        compiler_params=_params("parallel", "arbitrary"),
    )(a, w)


def _mm_res_kernel(a_ref, w_ref, x_ref, gate_ref, g_ref, o_ref, acc_ref):
    k = pl.program_id(1)

    @pl.when(k == 0)
    def _():
        acc_ref[...] = jnp.zeros_like(acc_ref)

    acc_ref[...] += _dot(a_ref[...], w_ref[...])

    @pl.when(k == pl.num_programs(1) - 1)
    def _():
        o_ref[...] = x_ref[...] + gate_ref[...] * _rms(acc_ref[...], g_ref[...])


def _matmul_residual(a, w, x, gate, g, m_rows, mod_idx, tm, tk_cap=1024):
    k, d = w.shape
    tk = _tile(k, tk_cap, LANE)
    return pl.pallas_call(
        _mm_res_kernel,
        grid=(m_rows // tm, k // tk),
        in_specs=[pl.BlockSpec((tm, tk), lambda i, kk: (i, kk)),
                  pl.BlockSpec((tk, d), lambda i, kk: (kk, 0)),
                  pl.BlockSpec((tm, d), lambda i, kk: (i, 0)),
                  pl.BlockSpec((None, 1, d), lambda i, kk: (mod_idx(i), 0, 0)),
                  _const_spec((1, d))],
        out_specs=pl.BlockSpec((tm, d), lambda i, kk: (i, 0)),
        out_shape=jax.ShapeDtypeStruct((m_rows, d), F32),
        scratch_shapes=[pltpu.VMEM((tm, d), F32)],
        compiler_params=_params("parallel", "arbitrary"),
    )(a, w, x, gate, g)


def _merge_kernel(gates_ref, a0, a1, a2, a3, w0, w1, w2, w3, o_ref):
    d = o_ref.shape[-1]
    m = None
    for i, (a, w) in enumerate(((a0, w0), (a1, w1), (a2, w2), (a3, w3))):
        t = gates_ref[:, i * d:(i + 1) * d].astype(F32) * _dot(a[...], w[...])
        m = t if m is None else m + t
    o_ref[...] = m.astype(o_ref.dtype)


def _merge(gates, acts, weights, m_rows, tm):
    d = weights[0].shape[1]
    return pl.pallas_call(
        _merge_kernel,
        grid=(m_rows // tm,),
        in_specs=[pl.BlockSpec((tm, N_BRANCH * d), lambda i: (i, 0))]
        + [pl.BlockSpec((tm, a.shape[1]), lambda i: (i, 0)) for a in acts]
        + [_const_spec(w.shape) for w in weights],
        out_specs=pl.BlockSpec((tm, d), lambda i: (i, 0)),
        out_shape=jax.ShapeDtypeStruct((m_rows, d), BF16),
        compiler_params=_params("parallel"),
    )(gates, *acts, *weights)


def _conformer_kernel(u_ref, w_ref, b_ref, lg_ref, lb_ref, o_ref, pad_ref, *, seg, nseg):
    pad = 16
    zeros = jnp.zeros((pad, CONV_DIM), F32)
    pad_ref[0:pad, :] = zeros
    pad_ref[pad + seg:2 * pad + seg, :] = zeros

    def body(s, carry):
        r0 = pl.multiple_of(s * seg, seg)
        u = u_ref[pl.ds(r0, seg), :]
        pad_ref[pad:pad + seg, :] = u[:, :CONV_DIM] * _sigmoid(u[:, CONV_DIM:])
        acc = jnp.zeros((seg, CONV_DIM), F32) + b_ref[...]
        for j in range(CONV_WIDTH):
            acc = acc + w_ref[j:j + 1, :] * pad_ref[pl.ds(pad - (CONV_WIDTH - 1) // 2 + j, seg), :]
        mu = jnp.mean(acc, axis=-1, keepdims=True)
        dv = acc - mu
        var = jnp.mean(dv * dv, axis=-1, keepdims=True)
        v = dv * lax.rsqrt(var + LN_EPS) * lg_ref[...] + lb_ref[...]
        o_ref[pl.ds(r0, seg), :] = (v * _sigmoid(v)).astype(o_ref.dtype)
        return carry

    lax.fori_loop(0, nseg, body, 0)


def _conformer(u, row0, nrows, seg, conv_w, conv_b, ln_g, ln_b):
    tq = max(seg, min(256, nrows))
    blk0 = row0 // tq
    return pl.pallas_call(
        functools.partial(_conformer_kernel, seg=seg, nseg=tq // seg),
        grid=(nrows // tq,),
        in_specs=[pl.BlockSpec((tq, U_CONV_W), lambda i: (blk0 + i, U_CONV // U_CONV_W)),
                  _const_spec(conv_w.shape), _const_spec((1, CONV_DIM)),
                  _const_spec((1, CONV_DIM)), _const_spec((1, CONV_DIM))],
        out_specs=pl.BlockSpec((tq, CONV_DIM), lambda i: (i, 0)),
        out_shape=jax.ShapeDtypeStruct((nrows, CONV_DIM), BF16),
        scratch_shapes=[pltpu.VMEM((seg + 32, CONV_DIM), F32)],
        compiler_params=_params("parallel"),
    )(u, conv_w, conv_b, ln_g, ln_b)


def _dft_kernel(c_ref, s_ref, z_ref, o_ref):
    o_ref[...] = (jnp.dot(c_ref[...], z_ref[:, :FOURIER_DIM], preferred_element_type=F32)
                  + jnp.dot(s_ref[...], z_ref[:, FOURIER_DIM:], preferred_element_type=F32)).astype(o_ref.dtype)


def _dft_mats(n, scale):
    idx = jnp.arange(n, dtype=jnp.int32)
    ang = ((idx[:, None] * idx[None, :]) % n).astype(F32) * (2.0 * math.pi / n)
    return jnp.cos(ang) * scale, jnp.sin(ang) * scale


def _fourier(z, row0, nseq, seqlen):
    nrows = nseq * seqlen
    cl, sl = _dft_mats(seqlen, seqlen ** -0.5)
    cl, sl = cl.astype(BF16), sl.astype(BF16)
    tm = _tile(seqlen, 512)
    nt = seqlen // tm
    sblk0 = row0 // seqlen
    return pl.pallas_call(
        _dft_kernel,
        grid=(nt, nseq),
        in_specs=[pl.BlockSpec((tm, seqlen), lambda i, b: (i, 0)),
                  pl.BlockSpec((tm, seqlen), lambda i, b: (i, 0)),
                  pl.BlockSpec((seqlen, 2 * FOURIER_DIM), lambda i, b: (sblk0 + b, 0))],
        out_specs=pl.BlockSpec((tm, FOURIER_DIM), lambda i, b: (b * nt + i, 0)),
        out_shape=jax.ShapeDtypeStruct((nrows, FOURIER_DIM), BF16),
        compiler_params=_params("parallel", "arbitrary"),
    )(cl, sl, z)


def _halo_specs(tq, width, blk0, col_blk, nblk_total):
    hb = tq // SUBLANE
    cur = pl.BlockSpec((tq, width), lambda i: (blk0 + i, col_blk))
    prev = pl.BlockSpec((SUBLANE, width), lambda i: (jnp.maximum((blk0 + i) * hb - 1, 0), col_blk))
    nxt = pl.BlockSpec((SUBLANE, width),
                       lambda i: (jnp.minimum((blk0 + i + 1) * hb, nblk_total - 1), col_blk))
    return cur, prev, nxt


def _fill_halo(scr, cur_ref, prev_ref, next_ref, tq, tiles_per_seq):
    i = pl.program_id(0)
    first = (i % tiles_per_seq) == 0
    last = (i % tiles_per_seq) == tiles_per_seq - 1
    scr[0:SUBLANE, :] = jnp.where(first, 0.0, prev_ref[...])
    scr[SUBLANE:SUBLANE + tq, :] = cur_ref[...]
    scr[SUBLANE + tq:2 * SUBLANE + tq, :] = jnp.where(last, 0.0, next_ref[...])


def _ssd_conv_kernel(cur_ref, prev_ref, next_ref, w_ref, b_ref, xs_ref, bm_ref, cm_ref, scr, *, tq, tiles_per_seq):
    _fill_halo(scr, cur_ref, prev_ref, next_ref, tq, tiles_per_seq)
    half = (SSD_CONV - 1) // 2
    cw = 256
    for c0 in range(0, SSD_XBC, cw):
        acc = jnp.zeros((tq, cw), F32) + b_ref[:, c0:c0 + cw]
        for j in range(SSD_CONV):
            acc = acc + w_ref[j:j + 1, c0:c0 + cw] * scr[pl.ds(SUBLANE - half + j, tq), c0:c0 + cw]
        y = acc * _sigmoid(acc)
        if c0 < SSD_DIM:
            xs_ref[:, c0:c0 + cw] = y
        elif c0 < SSD_DIM + SSD_BC:
            bm_ref[:, c0 - SSD_DIM:c0 - SSD_DIM + cw] = y
        else:
            cm_ref[:, c0 - SSD_DIM - SSD_BC:c0 - SSD_DIM - SSD_BC + cw] = y


def _ssd_conv(u, row0, nseq, seqlen, conv_w, conv_b):
    nrows = nseq * seqlen
    tq = _tile(seqlen, 256)
    cur, prev, nxt = _halo_specs(tq, U_XBC_W, row0 // tq, U_XBC // U_XBC_W, u.shape[0] // SUBLANE)
    outs = [jax.ShapeDtypeStruct((nrows, w), F32) for w in (SSD_DIM, SSD_BC, SSD_BC)]
    return pl.pallas_call(
        functools.partial(_ssd_conv_kernel, tq=tq, tiles_per_seq=seqlen // tq),
        grid=(nrows // tq,),
        in_specs=[cur, prev, nxt, _const_spec(conv_w.shape), _const_spec((1, SSD_XBC))],
        out_specs=[pl.BlockSpec((tq, w), lambda i: (i, 0)) for w in (SSD_DIM, SSD_BC, SSD_BC)],
        out_shape=outs,
        scratch_shapes=[pltpu.VMEM((tq + 2 * SUBLANE, U_XBC_W), F32)],
        compiler_params=_params("parallel"),
    )(u, u, u, conv_w, conv_b)


def _ssd_scan_kernel(xs_ref, b_ref, c_ref, dt_ref, bias_ref, alog_ref, s0_ref, *rest, q, rev, direction, need_y):
    if need_y:
        y_ref, sfin_ref, st = rest
    else:
        sfin_ref, st = rest

    @pl.when(pl.program_id(1) == 0)
    def _():
        st[...] = s0_ref[...]

    dt = _softplus(dt_ref[...] + bias_ref[...])
    a = dt * (-jnp.exp(alog_ref[...]))
    ti = lax.broadcasted_iota(jnp.int32, (q, q), 0)
    tk = lax.broadcasted_iota(jnp.int32, (q, q), 1)
    before = (tk >= ti) if rev else (tk <= ti)
    cs = _dot_exact_lhs(jnp.where(before, 1.0, 0.0).astype(BF16), a)
    cs_t = cs.T
    tot = cs[0:1, :] if rev else cs[q - 1:q, :]
    for g in range(SSD_GROUPS):
        cg = c_ref[:, g * SSD_STATE:(g + 1) * SSD_STATE]
        bg = b_ref[:, g * SSD_STATE:(g + 1) * SSD_STATE]
        state = st[g]
        if need_y:
            cb = _dot_nt(cg, bg)
            y_off = _dot(cg, state)
        xdd, dec = [], []
        for j in range(SSD_HEADS_PER_GROUP):
            h = g * SSD_HEADS_PER_GROUP + j
            col = direction * SSD_HEADS + h
            cs_col = cs[:, col:col + 1]
            xd = xs_ref[:, h * SSD_HEAD_DIM:(h + 1) * SSD_HEAD_DIM] * dt[:, col:col + 1]
            tot_h = tot[:, col:col + 1]
            xdd.append(xd * jnp.exp(tot_h - cs_col))
            dec.append(jnp.broadcast_to(jnp.exp(tot_h), (1, SSD_HEAD_DIM)))
            if need_y:
                lm = jnp.where(before, jnp.exp(jnp.minimum(cs_col - cs_t[col:col + 1, :], 0.0)), 0.0)
                y_ref[:, h * SSD_HEAD_DIM:(h + 1) * SSD_HEAD_DIM] = (
                    _dot(cb * lm, xd) + y_off[:, j * SSD_HEAD_DIM:(j + 1) * SSD_HEAD_DIM] * jnp.exp(cs_col))
        st[g] = state * jnp.concatenate(dec, axis=1) + _dot(bg.T, jnp.concatenate(xdd, axis=1))
    sfin_ref[...] = st[...]


def _ssd_scan(xs, bm, cm, u, row0, nseq, seqlen, dt_bias, a_log, s0, rev, need_y):
    q = SSD_CHUNK
    nc = seqlen // q
    ublk0 = row0 // q
    pos = (lambda c: nc - 1 - c) if rev else (lambda c: c)
    row = lambda b, c: (b * nc + pos(c), 0)
    sshape = (SSD_GROUPS, SSD_STATE, SSD_HEADS_PER_GROUP * SSD_HEAD_DIM)
    sspec = pl.BlockSpec((None,) + sshape, lambda b, c: (b, 0, 0, 0))
    out_specs, out_shape = [sspec], [jax.ShapeDtypeStruct((nseq,) + sshape, F32)]
    if need_y:
        out_specs.insert(0, pl.BlockSpec((q, SSD_DIM), row))
        out_shape.insert(0, jax.ShapeDtypeStruct((nseq * seqlen, SSD_DIM), F32))
    res = pl.pallas_call(
        functools.partial(_ssd_scan_kernel, q=q, rev=rev, direction=int(rev), need_y=need_y),
        grid=(nseq, nc),
        in_specs=[pl.BlockSpec((q, SSD_DIM), row), pl.BlockSpec((q, SSD_BC), row), pl.BlockSpec((q, SSD_BC), row),
                  pl.BlockSpec((q, U_DT_W), lambda b, c: (ublk0 + b * nc + pos(c), U_DT // U_DT_W)),
                  _const_spec((1, U_DT_W)), _const_spec((1, U_DT_W)), sspec],
        out_specs=out_specs, out_shape=out_shape,
        scratch_shapes=[pltpu.VMEM(sshape, F32)],
        compiler_params=_params("parallel", "arbitrary"),
    )(xs, bm, cm, u, dt_bias, a_log, s0)
    return res if need_y else (None, res[0])


def _ssd_fin_kernel(yf_ref, yb_ref, xs_ref, z_ref, d_ref, g_ref, o_ref):
    z = z_ref[...]
    y = (yf_ref[...] + yb_ref[...] + d_ref[...] * xs_ref[...]) * (z * _sigmoid(z))
    o_ref[...] = _rms(y, g_ref[...]).astype(o_ref.dtype)


def _ssd_finish(yf, yb, xs, u, row0, d_skip, norm_g):
    nrows = yf.shape[0]
    tq = _tile(nrows, 512)
    blk0 = row0 // tq
    tok = pl.BlockSpec((tq, SSD_DIM), lambda i: (i, 0))
    return pl.pallas_call(
        _ssd_fin_kernel,
        grid=(nrows // tq,),
        in_specs=[tok, tok, tok, pl.BlockSpec((tq, U_Z_W), lambda i: (blk0 + i, U_Z // U_Z_W)),
                  _const_spec((1, SSD_DIM)), _const_spec((1, SSD_DIM))],
        out_specs=tok,
        out_shape=jax.ShapeDtypeStruct((nrows, SSD_DIM), BF16),
        compiler_params=_params("parallel"),
    )(yf, yb, xs, u, d_skip, norm_g)


def _rwkv_prep_kernel(cur_ref, prev_ref, next_ref, mu_ref, w0_ref, w2_ref, a0_ref, a2_ref, g2_ref,
                      kk_ref, ka_ref, rk_ref, bd_ref,
                      r_o, lwf_o, lwb_o, k_o, v_o, kk_o, ka_o, g_o, bonus_o, scr, *, tq, tiles_per_seq):
    _fill_halo(scr, cur_ref, prev_ref, next_ref, tq, tiles_per_seq)
    u = cur_ref[...]
    shift = 0.5 * (scr[pl.ds(SUBLANE - 1, tq), :] + scr[pl.ds(SUBLANE + 1, tq), :]) - u
    x = u + shift * mu_ref[...]
    r = x[:, 0:RWKV_DIM]
    k = x[:, RWKV_DIM:2 * RWKV_DIM]
    v = x[:, 2 * RWKV_DIM:3 * RWKV_DIM]
    bd = bd_ref[...]
    for d, (off, out) in enumerate(((RW_WF, lwf_o), (RW_WB, lwb_o))):
        lo = jnp.tanh(x[:, off:off + LANE])
        wl = -_softplus(-(w0_ref[d:d + 1, :] + _dot(lo, w2_ref[d]))) - 0.5
        out[...] = -jnp.exp(wl)
    a = _sigmoid(a0_ref[...] + _dot(x[:, RW_AL:RW_AL + LANE], a2_ref[...]))
    kk = k * kk_ref[...]
    kk = kk * lax.rsqrt(_blocksum(kk * kk, bd) + 1e-12)
    kmod = k * (1.0 + (a - 1.0) * ka_ref[...])
    r_o[...] = r
    k_o[...] = kmod
    v_o[...] = v
    kk_o[...] = kk
    ka_o[...] = kk * a
    g_o[...] = _dot(_sigmoid(x[:, RW_GL:RW_GL + GATE_LORA]), g2_ref[...])
    bonus_o[...] = _blocksum(r * kmod * rk_ref[...], bd) * v


def _rwkv_prep(u, row0, nseq, seqlen, p):
    nrows = nseq * seqlen
    tq = _tile(seqlen, 256)
    cur, prev, nxt = _halo_specs(tq, U_RWKV_W, row0 // tq, 0, u.shape[0] // SUBLANE)
    consts = [p["mu"], p["w0"], p["w2"], p["a0"], p["a2"], p["g2"], p["k_k"], p["k_a"], p["r_k"], p["bd"]]
    tok = pl.BlockSpec((tq, RWKV_DIM), lambda i: (i, 0))
    return pl.pallas_call(
        functools.partial(_rwkv_prep_kernel, tq=tq, tiles_per_seq=seqlen // tq),
        grid=(nrows // tq,),
        in_specs=[cur, prev, nxt] + [_const_spec(c.shape) for c in consts],
        out_specs=[tok] * 9,
        out_shape=[jax.ShapeDtypeStruct((nrows, RWKV_DIM), F32)] * 9,
        scratch_shapes=[pltpu.VMEM((tq + 2 * SUBLANE, U_RWKV_W), F32)],
        compiler_params=_params("parallel"),
    )(u, u, u, *consts)


def _rwkv_scan_kernel(r_ref, lw_ref, k_ref, v_ref, kk_ref, ka_ref, s0_ref, *rest, C, rev, need_y):
    if need_y:
        y_ref, sfin_ref, st = rest
    else:
        sfin_ref, st = rest

    @pl.when(pl.program_id(1) == 0)
    def _():
        st[...] = s0_ref[...]

    n2 = 2 * C
    row = lax.broadcasted_iota(jnp.int32, (n2, n2), 0)
    col = lax.broadcasted_iota(jnp.int32, (n2, n2), 1)
    t, s = row & (C - 1), col & (C - 1)
    strict = (s > t) if rev else (s < t)
    incl = (s >= t) if rev else (s <= t)
    eye = jnp.where(row == col, 1.0, 0.0)
    ti = lax.broadcasted_iota(jnp.int32, (C, C), 0)
    tk = lax.broadcasted_iota(jnp.int32, (C, C), 1)
    tri = jnp.where((tk >= ti) if rev else (tk <= ti), 1.0, 0.0).astype(BF16)
    head0 = lax.broadcasted_iota(jnp.int32, (C, LANE), 1) < RWKV_HEAD_DIM

    def stack(x):
        return jnp.concatenate([jnp.where(head0, x, 0.0), jnp.where(head0, 0.0, x)], axis=0)

    for p in range(RWKV_PAIRS):
        sl = slice(p * LANE, (p + 1) * LANE)
        lw = lw_ref[:, sl]
        cs = _dot_exact_lhs(tri, lw)
        tot = cs[0:1, :] if rev else cs[C - 1:C, :]
        ems = jnp.exp(-cs)
        edec = jnp.exp(tot - cs)
        kt = stack(kk_ref[:, sl] * jnp.exp(cs - lw))
        rt = stack(r_ref[:, sl] * jnp.exp(cs))
        at = stack(ka_ref[:, sl] * ems)
        kb = stack(k_ref[:, sl] * ems)
        at2 = stack(ka_ref[:, sl] * edec)
        kb2 = stack(k_ref[:, sl] * edec)
        vs = stack(v_ref[:, sl])
        big = _dot_nt(jnp.concatenate([kt, rt], axis=0), jnp.concatenate([at, kb], axis=0))
        a_m = jnp.where(strict, big[:n2, :n2], 0.0)
        b_m = jnp.where(strict, big[:n2, n2:], 0.0)
        npow = -a_m
        tinv = eye + npow
        for _ in range(int(math.log2(C)) - 1):
            npow = _dot(npow, npow)
            tinv = tinv + _dot(tinv, npow)
        state = st[p]
        u_m = -_dot(tinv, _dot_nt(kt, state) + _dot(b_m, vs))
        if need_y:
            g_m = jnp.where(incl, big[n2:, :n2], 0.0)
            e_m = jnp.where(incl, big[n2:, n2:], 0.0)
            y = _dot_nt(rt, state) + _dot(g_m, u_m) + _dot(e_m, vs)
            y_ref[:, sl] = y[:C] + y[C:]
        st[p] = state * jnp.exp(tot) + _dot(u_m.T, at2) + _dot(vs.T, kb2)
    sfin_ref[...] = st[...]


def _rwkv_scan(r, lw, k, v, kk, ka, nseq, seqlen, s0, rev, need_y):
    C = RWKV_CHUNK
    nc = seqlen // C
    pos = (lambda c: nc - 1 - c) if rev else (lambda c: c)
    tok = pl.BlockSpec((C, RWKV_DIM), lambda b, c: (b * nc + pos(c), 0))
    sshape = (RWKV_PAIRS, LANE, LANE)
    sspec = pl.BlockSpec((None,) + sshape, lambda b, c: (b, 0, 0, 0))
    out_specs, out_shape = [sspec], [jax.ShapeDtypeStruct((nseq,) + sshape, F32)]
    if need_y:
        out_specs.insert(0, tok)
        out_shape.insert(0, jax.ShapeDtypeStruct((nseq * seqlen, RWKV_DIM), F32))
    res = pl.pallas_call(
        functools.partial(_rwkv_scan_kernel, C=C, rev=rev, need_y=need_y),
        grid=(nseq, nc),
        in_specs=[tok] * 6 + [sspec],
        out_specs=out_specs, out_shape=out_shape,
        scratch_shapes=[pltpu.VMEM(sshape, F32)],
        compiler_params=_params("parallel", "arbitrary"),
    )(r, lw, k, v, kk, ka, s0)
    return res if need_y else (None, res[0])


def _rwkv_fin_kernel(yf_ref, yb_ref, bonus_ref, g_ref, lg_ref, lb_ref, bd_ref, o_ref):
    y = yf_ref[...] + yb_ref[...]
    bd = bd_ref[...]
    dv = y - _blocksum(y, bd) * (1.0 / RWKV_HEAD_DIM)
    var = _blocksum(dv * dv, bd) * (1.0 / RWKV_HEAD_DIM)
    yn = dv * lax.rsqrt(var + GN_EPS) * lg_ref[...] + lb_ref[...]
    o_ref[...] = ((yn + bonus_ref[...]) * g_ref[...]).astype(o_ref.dtype)


def _rwkv_finish(yf, yb, bonus, g, ln_g, ln_b, bd):
    nrows = yf.shape[0]
    tq = _tile(nrows, 512)
    tok = pl.BlockSpec((tq, RWKV_DIM), lambda i: (i, 0))
    return pl.pallas_call(
        _rwkv_fin_kernel,
        grid=(nrows // tq,),
        in_specs=[tok] * 4 + [_const_spec((1, RWKV_DIM))] * 2 + [_const_spec(bd.shape)],
        out_specs=tok,
        out_shape=jax.ShapeDtypeStruct((nrows, RWKV_DIM), BF16),
        compiler_params=_params("parallel"),
    )(yf, yb, bonus, g, ln_g, ln_b, bd)


def _to_cols(t, b, rows):
    return t.reshape(b, rows, GRID_W, -1).transpose(0, 2, 1, 3).reshape(b * rows * GRID_W, -1)


def _to_rows(t, b, rows):
    return t.reshape(b, GRID_W, rows, -1).transpose(0, 2, 1, 3).reshape(b * rows * GRID_W, -1)


def _pad_cols(w, n):
    return jnp.pad(w, [(0, 0)] * (w.ndim - 1) + [(0, n - w.shape[-1])])


def _pad_rows(w, n):
    return jnp.pad(w, [(0, 0)] * (w.ndim - 2) + [(0, n - w.shape[-2]), (0, 0)])


def kernel(x, c, ctx, c_ctx, mod_w, mod_b, norm_g, w_in, conv_w, conv_b, conv_ln_g, conv_ln_b, conv_out, ssd_conv_w, ssd_conv_b, ssd_A_log, ssd_dt_bias, ssd_D, ssd_norm_g, ssd_out, fourier_out, rwkv_mu, rwkv_w0, rwkv_w2, rwkv_a0, rwkv_a2, rwkv_g2, rwkv_k_k, rwkv_k_a, rwkv_r_k, rwkv_ln_g, rwkv_ln_b, rwkv_out, w_o, mlp_up, mlp_down):
    B, L, D = x.shape
    CL = ctx.shape[1]
    depth = mod_w.shape[0]
    rows = L // GRID_W
    n_lat, n_ctx = B * L, B * CL
    n_all = n_lat + n_ctx
    tm = _tile(math.gcd(L, n_ctx), 1024)
    tm_small = _tile(tm, 512)
    tm_merge = _tile(tm, 256)
    mod_idx = lambda t: (lambda i: jnp.minimum((i * t) // L, B))

    o_ssd, o_rwkv, o_conv, o_fft = SSD_IN, REC_IN, REC_IN + 2 * CONV_DIM, REC_IN + 2 * CONV_DIM + FOURIER_DIM
    rw = w_in[:, :, o_ssd:o_rwkv]
    rw_cols = jnp.concatenate(
        [rw[:, :, :3 * RWKV_DIM]]
        + [_pad_cols(rw[:, :, a:b], LANE) for a, b in ((1536, 1600), (1600, 1664), (1664, 1728))]
        + [rw[:, :, 1728:]], axis=2)
    w_u = jnp.concatenate(
        [rw_cols, w_in[:, :, o_rwkv:o_conv], w_in[:, :, o_conv:o_fft],
         w_in[:, :, SSD_DIM:SSD_DIM + SSD_XBC], w_in[:, :, :SSD_DIM],
         _pad_cols(w_in[:, :, SSD_DIM + SSD_XBC:SSD_IN], U_DT_W)], axis=2).astype(BF16)
    w_gate = w_in[:, :, o_fft:].astype(BF16)
    up_w, down_w, wo_w = mlp_up.astype(BF16), mlp_down.astype(BF16), w_o.astype(BF16)
    branch_w = [conv_out.astype(BF16), ssd_out.astype(BF16), fourier_out.astype(BF16), rwkv_out.astype(BF16)]
    mu = rwkv_mu
    mu_cols = jnp.concatenate(
        [mu[:, :3 * RWKV_DIM]] + [_pad_cols(mu[:, a:b], LANE) for a, b in ((1536, 1600), (1600, 1664), (1664, 1728))]
        + [mu[:, 1728:]], axis=1)
    head_of = jnp.arange(RWKV_DIM, dtype=jnp.int32) // RWKV_HEAD_DIM
    bd = (head_of[:, None] == head_of[None, :]).astype(BF16)
    gidx = jnp.arange(FOURIER_DIM, dtype=jnp.int32)
    cc, sc_ = _dft_mats(FOURIER_GROUP_DIM, FOURIER_GROUP_DIM ** -0.5)
    same_group = (gidx[:, None] // FOURIER_GROUP_DIM) == (gidx[None, :] // FOURIER_GROUP_DIM)
    tile_g = lambda m: jnp.where(same_group, jnp.tile(m, (FOURIER_GROUPS, FOURIER_GROUPS)), 0.0)
    chan_w = jnp.concatenate([tile_g(cc), -tile_g(sc_)], axis=1).astype(BF16)
    conv_w_p = _pad_rows(conv_w, 32)
    ssd_conv_w_p = _pad_rows(ssd_conv_w, SUBLANE)
    dt_bias_p = _pad_cols(ssd_dt_bias.reshape(depth, 1, 2 * SSD_HEADS), U_DT_W)
    a_log_p = _pad_cols(ssd_A_log.reshape(depth, 1, 2 * SSD_HEADS), U_DT_W)
    d_skip = jnp.repeat(ssd_D, SSD_HEAD_DIM, axis=1).reshape(depth, 1, SSD_DIM)

    cond = jnp.concatenate([c, c_ctx[None, :], jnp.zeros((SUBLANE - (B + 1) % SUBLANE, D), F32)], axis=0)
    mod = _mod_all(cond, mod_w, mod_b)

    xs = jnp.concatenate([x.reshape(n_lat, D), ctx.reshape(n_ctx, D)], axis=0)
    ssd_zero = jnp.zeros((B, SSD_GROUPS, SSD_STATE, SSD_HEADS_PER_GROUP * SSD_HEAD_DIM), F32)
    rwkv_zero = jnp.zeros((B, RWKV_PAIRS, LANE, LANE), F32)

    for i in range(depth):
        last = i == depth - 1
        m_out = n_lat if last else n_all
        sh1, sc1, g1, sh2, sc2, g2 = (mod[i, :, j * D:(j + 1) * D].reshape(-1, 1, D) for j in range(6))
        ng = norm_g[i].reshape(4, 1, D)

        h = _norm_mod(xs, ng[0], sc1, sh1, n_all, mod_idx(tm), tm)
        u = _matmul(h, w_u[i], n_all, tm, F32, tn_cap=896)
        gates = _matmul(h, w_gate[i], m_out, tm, BF16, epilogue="sigmoid")

        a_conv = _conformer(u, 0, n_lat, GRID_W, conv_w_p[i], conv_b[i][None], conv_ln_g[i][None], conv_ln_b[i][None])
        z = _matmul(u, chan_w, m_out, tm, BF16, a_col=U_FFT // U_FFT_W)
        a_fft = _fourier(z, 0, B, L)
        if not last:
            a_conv = jnp.concatenate(
                [a_conv, _conformer(u, n_lat, n_ctx, CL, conv_w_p[i], conv_b[i][None], conv_ln_g[i][None],
                                    conv_ln_b[i][None])], axis=0)
            a_fft = jnp.concatenate([a_fft, _fourier(z, n_lat, B, CL)], axis=0)

        ssd_args = (dt_bias_p[i], a_log_p[i])
        xs_c, bm_c, cm_c = _ssd_conv(u, n_lat, B, CL, ssd_conv_w_p[i], ssd_conv_b[i][None])
        yfc, sf = _ssd_scan(xs_c, bm_c, cm_c, u, n_lat, B, CL, *ssd_args, ssd_zero, False, not last)
        ybc, sb = _ssd_scan(xs_c, bm_c, cm_c, u, n_lat, B, CL, *ssd_args, ssd_zero, True, not last)
        xs_l, bm_l, cm_l = _ssd_conv(u, 0, B, L, ssd_conv_w_p[i], ssd_conv_b[i][None])
        yfl, _ = _ssd_scan(xs_l, bm_l, cm_l, u, 0, B, L, *ssd_args, sf, False, True)
        ybl, _ = _ssd_scan(xs_l, bm_l, cm_l, u, 0, B, L, *ssd_args, sb, True, True)
        a_ssd = _ssd_finish(yfl, ybl, xs_l, u, 0, d_skip[i], ssd_norm_g[i][None])
        if not last:
            a_ssd = jnp.concatenate(
                [a_ssd, _ssd_finish(yfc, ybc, xs_c, u, n_lat, d_skip[i], ssd_norm_g[i][None])], axis=0)

        rp = dict(mu=mu_cols[i][None], w0=rwkv_w0[i], w2=_pad_rows(rwkv_w2[i], LANE).astype(BF16),
                  a0=rwkv_a0[i][None], a2=_pad_rows(rwkv_a2[i], LANE).astype(BF16), g2=rwkv_g2[i].astype(BF16),
                  k_k=rwkv_k_k[i][None], k_a=rwkv_k_a[i][None], r_k=rwkv_r_k[i].reshape(1, RWKV_DIM), bd=bd)
        pc = _rwkv_prep(u, n_lat, B, CL, rp)
        r_, lwf, lwb, k_, v_, kk_, ka_, g_, bonus = pc
        yfc, s_f = _rwkv_scan(r_, lwf, k_, v_, kk_, ka_, B, CL, rwkv_zero, False, not last)
        ybc, s_b = _rwkv_scan(r_, lwb, k_, v_, kk_, ka_, B, CL, rwkv_zero, True, not last)
        if not last:
            a_rwkv_c = _rwkv_finish(yfc, ybc, bonus, g_, rwkv_ln_g[i][None], rwkv_ln_b[i][None], bd)
        u_cols = _to_cols(u[:n_lat, U_RWKV:U_RWKV + U_RWKV_W], B, rows)
        r_, lwf, lwb, k_, v_, kk_, ka_, g_, bonus = _rwkv_prep(u_cols, 0, B, L, rp)
        yfl, _ = _rwkv_scan(r_, lwf, k_, v_, kk_, ka_, B, L, s_f, False, True)
        ybl, _ = _rwkv_scan(r_, lwb, k_, v_, kk_, ka_, B, L, s_b, True, True)
        a_rwkv = _to_rows(_rwkv_finish(yfl, ybl, bonus, g_, rwkv_ln_g[i][None], rwkv_ln_b[i][None], bd), B, rows)
        if not last:
            a_rwkv = jnp.concatenate([a_rwkv, a_rwkv_c], axis=0)

        m = _merge(gates, [a_conv, a_ssd, a_fft, a_rwkv], [w[i] for w in branch_w], m_out, tm_merge)
        xs = _matmul_residual(m, wo_w[i], xs, g1, ng[1], m_out, mod_idx(tm_small), tm_small)

        h2 = _norm_mod(xs, ng[2], sc2, sh2, m_out, mod_idx(tm), tm)
        mid = _matmul(h2, up_w[i], m_out, tm, BF16, epilogue="relu2")
        xs = _matmul_residual(mid, down_w[i], xs, g2, ng[3], m_out, mod_idx(tm_small), tm_small)

    return xs[:n_lat].reshape(B, L, D)
```

```python
import functools
import math

import jax
import jax.numpy as jnp
from jax import lax
from jax.experimental import pallas as pl
from jax.experimental.pallas import tpu as pltpu

F32 = jnp.float32
BF16 = jnp.bfloat16

GRID_W = 64
RMS_EPS = 1e-6
LN_EPS = 1e-5
GN_EPS = 64e-5
CONV_DIM = 512
CONV_WIDTH = 31
SSD_HEADS = 12
SSD_HEAD_DIM = 64
SSD_DIM = SSD_HEADS * SSD_HEAD_DIM
SSD_GROUPS = 4
SSD_HEADS_PER_GROUP = SSD_HEADS // SSD_GROUPS
SSD_STATE = 128
SSD_CONV = 5
SSD_BC = SSD_GROUPS * SSD_STATE
SSD_XBC = SSD_DIM + 2 * SSD_BC
SSD_IN = SSD_DIM + SSD_XBC + 2 * SSD_HEADS
SSD_CHUNK = 128
FOURIER_GROUPS = 4
FOURIER_GROUP_DIM = 128
FOURIER_DIM = FOURIER_GROUPS * FOURIER_GROUP_DIM
RWKV_HEADS = 8
RWKV_HEAD_DIM = 64
RWKV_DIM = RWKV_HEADS * RWKV_HEAD_DIM
RWKV_PAIRS = RWKV_HEADS // 2
DECAY_LORA = 64
ICL_LORA = 64
GATE_LORA = 128
RWKV_IN = 3 * RWKV_DIM + 2 * DECAY_LORA + ICL_LORA + GATE_LORA
RWKV_CHUNK = 64
REC_IN = SSD_IN + RWKV_IN
N_BRANCH = 4

LANE = 128
SUBLANE = 8
VMEM_LIMIT = 56 * 1024 * 1024

U_XBC, U_XBC_W = 0, SSD_XBC
U_DT, U_DT_W = 1792, LANE
U_CONV, U_CONV_W = 2048, 2 * CONV_DIM
U_Z, U_Z_W = 3072, SSD_DIM
U_FFT, U_FFT_W = 4096, FOURIER_DIM
U_N = 4608
U_RWKV_W = 2048
RW_WF, RW_WB, RW_AL, RW_GL = 1536, 1664, 1792, 1920


def _params(*sem):
    return pltpu.CompilerParams(dimension_semantics=sem, vmem_limit_bytes=VMEM_LIMIT)


def _tile(n, cap, quantum=SUBLANE):
    if n <= cap:
        return n
    t = (cap // quantum) * quantum
    while t >= quantum:
        if n % t == 0:
            return t
        t -= quantum
    raise ValueError(f"no tile for {n} under {cap}")


def _const_spec(shape):
    nd = len(shape)
    return pl.BlockSpec(shape, lambda *_: (0,) * nd)


def _dot(a, b):
    return jnp.dot(a.astype(BF16), b.astype(BF16), preferred_element_type=F32)


def _dot_nt(a, b):
    return lax.dot_general(a.astype(BF16), b.astype(BF16), (((1,), (1,)), ((), ())),
                           preferred_element_type=F32)


def _split3(x):
    h1 = x.astype(BF16)
    r1 = x - h1.astype(F32)
    h2 = r1.astype(BF16)
    h3 = (r1 - h2.astype(F32)).astype(BF16)
    return h1, h2, h3


def _dot_exact_lhs(m, x):
    h1, h2, h3 = _split3(x)
    d = lambda h: jnp.dot(m, h, preferred_element_type=F32)
    return d(h1) + d(h2) + d(h3)


def _blocksum(x, bd):
    hi = x.astype(BF16)
    lo = (x - hi.astype(F32)).astype(BF16)
    return (jnp.dot(hi, bd, preferred_element_type=F32) + jnp.dot(lo, bd, preferred_element_type=F32))


def _sigmoid(x):
    return 1.0 / (1.0 + jnp.exp(-x))


def _softplus(x):
    return jnp.maximum(x, 0.0) + jnp.log1p(jnp.exp(-jnp.abs(x)))


def _rms(y, g):
    return y * lax.rsqrt(jnp.mean(y * y, axis=-1, keepdims=True) + RMS_EPS) * g


def _mod_kernel(a_ref, w_ref, b_ref, o_ref):
    a = a_ref[...]
    a = a * _sigmoid(a)
    o_ref[...] = _dot(a, w_ref[...]) + b_ref[...]


def _mod_all(cond, mod_w, mod_b):
    depth, d, n = mod_w.shape
    tn = _tile(n, 1024, LANE)
    return pl.pallas_call(
        _mod_kernel,
        grid=(depth, n // tn),
        in_specs=[_const_spec(cond.shape),
                  pl.BlockSpec((None, d, tn), lambda l, j: (l, 0, j)),
                  pl.BlockSpec((None, 1, tn), lambda l, j: (l, 0, j))],
        out_specs=pl.BlockSpec((None, cond.shape[0], tn), lambda l, j: (l, 0, j)),
        out_shape=jax.ShapeDtypeStruct((depth, cond.shape[0], n), F32),
        compiler_params=_params("parallel", "parallel"),
    )(cond, mod_w, mod_b.reshape(depth, 1, n))


def _norm_mod_kernel(x_ref, g_ref, sc_ref, sh_ref, o_ref):
    o_ref[...] = (_rms(x_ref[...], g_ref[...]) * (1.0 + sc_ref[...]) + sh_ref[...]).astype(o_ref.dtype)


def _norm_mod(x, g, sc, sh, m_rows, mod_idx, tm):
    d = x.shape[1]
    return pl.pallas_call(
        _norm_mod_kernel,
        grid=(m_rows // tm,),
        in_specs=[pl.BlockSpec((tm, d), lambda i: (i, 0)),
                  _const_spec((1, d)),
                  pl.BlockSpec((None, 1, d), lambda i: (mod_idx(i), 0, 0)),
                  pl.BlockSpec((None, 1, d), lambda i: (mod_idx(i), 0, 0))],
        out_specs=pl.BlockSpec((tm, d), lambda i: (i, 0)),
        out_shape=jax.ShapeDtypeStruct((m_rows, d), BF16),
        compiler_params=_params("parallel"),
    )(x, g, sc, sh)


def _mm_kernel(a_ref, w_ref, o_ref, *, epilogue):
    acc = _dot(a_ref[...], w_ref[...])
    if epilogue == "relu2":
        acc = jnp.square(jnp.maximum(acc, 0.0))
    elif epilogue == "sigmoid":
        acc = _sigmoid(acc)
    o_ref[...] = acc.astype(o_ref.dtype)


def _matmul(a, w, m_rows, tm, out_dtype, epilogue=None, a_col=0, a_row=0, tn_cap=1024):
    k, n = w.shape
    tn = _tile(n, tn_cap, LANE)
    return pl.pallas_call(
        functools.partial(_mm_kernel, epilogue=epilogue),
        grid=(m_rows // tm, n // tn),
        in_specs=[pl.BlockSpec((tm, k), lambda i, j: (a_row + i, a_col)),
                  pl.BlockSpec((k, tn), lambda i, j: (0, j))],
        out_specs=pl.BlockSpec((tm, tn), lambda i, j: (i, j)),
        out_shape=jax.ShapeDtypeStruct((m_rows, n), out_dtype),
        compiler_params=_params("parallel", "arbitrary"),
    )(a, w)


def _mm_res_kernel(a_ref, w_ref, x_ref, gate_ref, g_ref, o_ref, acc_ref):
    k = pl.program_id(1)

    @pl.when(k == 0)
    def _():
        acc_ref[...] = jnp.zeros_like(acc_ref)

    acc_ref[...] += _dot(a_ref[...], w_ref[...])

    @pl.when(k == pl.num_programs(1) - 1)
    def _():
        o_ref[...] = x_ref[...] + gate_ref[...] * _rms(acc_ref[...], g_ref[...])


def _matmul_residual(a, w, x, gate, g, m_rows, mod_idx, tm, tk_cap=1024):
    k, d = w.shape
    tk = _tile(k, tk_cap, LANE)
    return pl.pallas_call(
        _mm_res_kernel,
        grid=(m_rows // tm, k // tk),
        in_specs=[pl.BlockSpec((tm, tk), lambda i, kk: (i, kk)),
                  pl.BlockSpec((tk, d), lambda i, kk: (kk, 0)),
                  pl.BlockSpec((tm, d), lambda i, kk: (i, 0)),
                  pl.BlockSpec((None, 1, d), lambda i, kk: (mod_idx(i), 0, 0)),
                  _const_spec((1, d))],
        out_specs=pl.BlockSpec((tm, d), lambda i, kk: (i, 0)),
        out_shape=jax.ShapeDtypeStruct((m_rows, d), F32),
        scratch_shapes=[pltpu.VMEM((tm, d), F32)],
        compiler_params=_params("parallel", "arbitrary"),
    )(a, w, x, gate, g)


def _merge_kernel(gates_ref, a0, a1, a2, a3, w0, w1, w2, w3, o_ref):
    d = o_ref.shape[-1]
    m = None
    for i, (a, w) in enumerate(((a0, w0), (a1, w1), (a2, w2), (a3, w3))):
        t = gates_ref[:, i * d:(i + 1) * d].astype(F32) * _dot(a[...], w[...])
        m = t if m is None else m + t
    o_ref[...] = m.astype(o_ref.dtype)


def _merge(gates, acts, weights, m_rows, tm):
    d = weights[0].shape[1]
    return pl.pallas_call(
        _merge_kernel,
        grid=(m_rows // tm,),
        in_specs=[pl.BlockSpec((tm, N_BRANCH * d), lambda i: (i, 0))]
        + [pl.BlockSpec((tm, a.shape[1]), lambda i: (i, 0)) for a in acts]
        + [_const_spec(w.shape) for w in weights],
        out_specs=pl.BlockSpec((tm, d), lambda i: (i, 0)),
        out_shape=jax.ShapeDtypeStruct((m_rows, d), BF16),
        compiler_params=_params("parallel"),
    )(gates, *acts, *weights)


def _conformer_kernel(u_ref, w_ref, b_ref, lg_ref, lb_ref, o_ref, pad_ref, *, seg, nseg):
    pad = 16
    zeros = jnp.zeros((pad, CONV_DIM), F32)
    pad_ref[0:pad, :] = zeros
    pad_ref[pad + seg:2 * pad + seg, :] = zeros

    def body(s, carry):
        r0 = pl.multiple_of(s * seg, seg)
        u = u_ref[pl.ds(r0, seg), :]
        pad_ref[pad:pad + seg, :] = u[:, :CONV_DIM] * _sigmoid(u[:, CONV_DIM:])
        acc = jnp.zeros((seg, CONV_DIM), F32) + b_ref[...]
        for j in range(CONV_WIDTH):
            acc = acc + w_ref[j:j + 1, :] * pad_ref[pl.ds(pad - (CONV_WIDTH - 1) // 2 + j, seg), :]
        mu = jnp.mean(acc, axis=-1, keepdims=True)
        dv = acc - mu
        var = jnp.mean(dv * dv, axis=-1, keepdims=True)
        v = dv * lax.rsqrt(var + LN_EPS) * lg_ref[...] + lb_ref[...]
        o_ref[pl.ds(r0, seg), :] = (v * _sigmoid(v)).astype(o_ref.dtype)
        return carry

    lax.fori_loop(0, nseg, body, 0)


def _conformer(u, row0, nrows, seg, conv_w, conv_b, ln_g, ln_b):
    tq = max(seg, min(256, nrows))
    blk0 = row0 // tq
    return pl.pallas_call(
        functools.partial(_conformer_kernel, seg=seg, nseg=tq // seg),
        grid=(nrows // tq,),
        in_specs=[pl.BlockSpec((tq, U_CONV_W), lambda i: (blk0 + i, U_CONV // U_CONV_W)),
                  _const_spec(conv_w.shape), _const_spec((1, CONV_DIM)),
                  _const_spec((1, CONV_DIM)), _const_spec((1, CONV_DIM))],
        out_specs=pl.BlockSpec((tq, CONV_DIM), lambda i: (i, 0)),
        out_shape=jax.ShapeDtypeStruct((nrows, CONV_DIM), BF16),
        scratch_shapes=[pltpu.VMEM((seg + 32, CONV_DIM), F32)],
        compiler_params=_params("parallel"),
    )(u, conv_w, conv_b, ln_g, ln_b)


def _dft_kernel(c_ref, s_ref, z_ref, o_ref):
    o_ref[...] = (jnp.dot(c_ref[...], z_ref[:, :FOURIER_DIM], preferred_element_type=F32)
                  + jnp.dot(s_ref[...], z_ref[:, FOURIER_DIM:], preferred_element_type=F32)).astype(o_ref.dtype)


def _dft_mats(n, scale):
    idx = jnp.arange(n, dtype=jnp.int32)
    ang = ((idx[:, None] * idx[None, :]) % n).astype(F32) * (2.0 * math.pi / n)
    return jnp.cos(ang) * scale, jnp.sin(ang) * scale


def _fourier(z, row0, nseq, seqlen):
    nrows = nseq * seqlen
    cl, sl = _dft_mats(seqlen, seqlen ** -0.5)
    cl, sl = cl.astype(BF16), sl.astype(BF16)
    tm = _tile(seqlen, 512)
    nt = seqlen // tm
    sblk0 = row0 // seqlen
    return pl.pallas_call(
        _dft_kernel,
        grid=(nt, nseq),
        in_specs=[pl.BlockSpec((tm, seqlen), lambda i, b: (i, 0)),
                  pl.BlockSpec((tm, seqlen), lambda i, b: (i, 0)),
                  pl.BlockSpec((seqlen, 2 * FOURIER_DIM), lambda i, b: (sblk0 + b, 0))],
        out_specs=pl.BlockSpec((tm, FOURIER_DIM), lambda i, b: (b * nt + i, 0)),
        out_shape=jax.ShapeDtypeStruct((nrows, FOURIER_DIM), BF16),
        compiler_params=_params("parallel", "arbitrary"),
    )(cl, sl, z)


def _halo_specs(tq, width, blk0, col_blk, nblk_total):
    hb = tq // SUBLANE
    cur = pl.BlockSpec((tq, width), lambda i: (blk0 + i, col_blk))
    prev = pl.BlockSpec((SUBLANE, width), lambda i: (jnp.maximum((blk0 + i) * hb - 1, 0), col_blk))
    nxt = pl.BlockSpec((SUBLANE, width),
                       lambda i: (jnp.minimum((blk0 + i + 1) * hb, nblk_total - 1), col_blk))
    return cur, prev, nxt


def _fill_halo(scr, cur_ref, prev_ref, next_ref, tq, tiles_per_seq):
    i = pl.program_id(0)
    first = (i % tiles_per_seq) == 0
    last = (i % tiles_per_seq) == tiles_per_seq - 1
    scr[0:SUBLANE, :] = jnp.where(first, 0.0, prev_ref[...])
    scr[SUBLANE:SUBLANE + tq, :] = cur_ref[...]
    scr[SUBLANE + tq:2 * SUBLANE + tq, :] = jnp.where(last, 0.0, next_ref[...])


def _ssd_conv_kernel(cur_ref, prev_ref, next_ref, w_ref, b_ref, xs_ref, bm_ref, cm_ref, scr, *, tq, tiles_per_seq):
    _fill_halo(scr, cur_ref, prev_ref, next_ref, tq, tiles_per_seq)
    half = (SSD_CONV - 1) // 2
    cw = 256
    for c0 in range(0, SSD_XBC, cw):
        acc = jnp.zeros((tq, cw), F32) + b_ref[:, c0:c0 + cw]
        for j in range(SSD_CONV):
            acc = acc + w_ref[j:j + 1, c0:c0 + cw] * scr[pl.ds(SUBLANE - half + j, tq), c0:c0 + cw]
        y = acc * _sigmoid(acc)
        if c0 < SSD_DIM:
            xs_ref[:, c0:c0 + cw] = y
        elif c0 < SSD_DIM + SSD_BC:
            bm_ref[:, c0 - SSD_DIM:c0 - SSD_DIM + cw] = y
        else:
            cm_ref[:, c0 - SSD_DIM - SSD_BC:c0 - SSD_DIM - SSD_BC + cw] = y


def _ssd_conv(u, row0, nseq, seqlen, conv_w, conv_b):
    nrows = nseq * seqlen
    tq = _tile(seqlen, 256)
    cur, prev, nxt = _halo_specs(tq, U_XBC_W, row0 // tq, U_XBC // U_XBC_W, u.shape[0] // SUBLANE)
    outs = [jax.ShapeDtypeStruct((nrows, w), F32) for w in (SSD_DIM, SSD_BC, SSD_BC)]
    return pl.pallas_call(
        functools.partial(_ssd_conv_kernel, tq=tq, tiles_per_seq=seqlen // tq),
        grid=(nrows // tq,),
        in_specs=[cur, prev, nxt, _const_spec(conv_w.shape), _const_spec((1, SSD_XBC))],
        out_specs=[pl.BlockSpec((tq, w), lambda i: (i, 0)) for w in (SSD_DIM, SSD_BC, SSD_BC)],
        out_shape=outs,
        scratch_shapes=[pltpu.VMEM((tq + 2 * SUBLANE, U_XBC_W), F32)],
        compiler_params=_params("parallel"),
    )(u, u, u, conv_w, conv_b)


def _ssd_scan_kernel(xs_ref, b_ref, c_ref, dt_ref, bias_ref, alog_ref, s0_ref, *rest, q, rev, direction, need_y):
    if need_y:
        y_ref, sfin_ref, st = rest
    else:
        sfin_ref, st = rest

    @pl.when(pl.program_id(1) == 0)
    def _():
        st[...] = s0_ref[...]

    dt = _softplus(dt_ref[...] + bias_ref[...])
    a = dt * (-jnp.exp(alog_ref[...]))
    ti = lax.broadcasted_iota(jnp.int32, (q, q), 0)
    tk = lax.broadcasted_iota(jnp.int32, (q, q), 1)
    before = (tk >= ti) if rev else (tk <= ti)
    cs = _dot_exact_lhs(jnp.where(before, 1.0, 0.0).astype(BF16), a)
    cs_t = cs.T
    tot = cs[0:1, :] if rev else cs[q - 1:q, :]
    for g in range(SSD_GROUPS):
        cg = c_ref[:, g * SSD_STATE:(g + 1) * SSD_STATE]
        bg = b_ref[:, g * SSD_STATE:(g + 1) * SSD_STATE]
        state = st[g]
        if need_y:
            cb = _dot_nt(cg, bg)
            y_off = _dot(cg, state)
        xdd, dec = [], []
        for j in range(SSD_HEADS_PER_GROUP):
            h = g * SSD_HEADS_PER_GROUP + j
            col = direction * SSD_HEADS + h
            cs_col = cs[:, col:col + 1]
            xd = xs_ref[:, h * SSD_HEAD_DIM:(h + 1) * SSD_HEAD_DIM] * dt[:, col:col + 1]
            tot_h = tot[:, col:col + 1]
            xdd.append(xd * jnp.exp(tot_h - cs_col))
            dec.append(jnp.broadcast_to(jnp.exp(tot_h), (1, SSD_HEAD_DIM)))
            if need_y:
                lm = jnp.where(before, jnp.exp(jnp.minimum(cs_col - cs_t[col:col + 1, :], 0.0)), 0.0)
                y_ref[:, h * SSD_HEAD_DIM:(h + 1) * SSD_HEAD_DIM] = (
                    _dot(cb * lm, xd) + y_off[:, j * SSD_HEAD_DIM:(j + 1) * SSD_HEAD_DIM] * jnp.exp(cs_col))
        st[g] = state * jnp.concatenate(dec, axis=1) + _dot(bg.T, jnp.concatenate(xdd, axis=1))
    sfin_ref[...] = st[...]


def _ssd_scan(xs, bm, cm, u, row0, nseq, seqlen, dt_bias, a_log, s0, rev, need_y):
    q = SSD_CHUNK
    nc = seqlen // q
    ublk0 = row0 // q
    pos = (lambda c: nc - 1 - c) if rev else (lambda c: c)
    row = lambda b, c: (b * nc + pos(c), 0)
    sshape = (SSD_GROUPS, SSD_STATE, SSD_HEADS_PER_GROUP * SSD_HEAD_DIM)
    sspec = pl.BlockSpec((None,) + sshape, lambda b, c: (b, 0, 0, 0))
    out_specs, out_shape = [sspec], [jax.ShapeDtypeStruct((nseq,) + sshape, F32)]
    if need_y:
        out_specs.insert(0, pl.BlockSpec((q, SSD_DIM), row))
        out_shape.insert(0, jax.ShapeDtypeStruct((nseq * seqlen, SSD_DIM), F32))
    res = pl.pallas_call(
        functools.partial(_ssd_scan_kernel, q=q, rev=rev, direction=int(rev), need_y=need_y),
        grid=(nseq, nc),
        in_specs=[pl.BlockSpec((q, SSD_DIM), row), pl.BlockSpec((q, SSD_BC), row), pl.BlockSpec((q, SSD_BC), row),
                  pl.BlockSpec((q, U_DT_W), lambda b, c: (ublk0 + b * nc + pos(c), U_DT // U_DT_W)),
                  _const_spec((1, U_DT_W)), _const_spec((1, U_DT_W)), sspec],
        out_specs=out_specs, out_shape=out_shape,
        scratch_shapes=[pltpu.VMEM(sshape, F32)],
        compiler_params=_params("parallel", "arbitrary"),
    )(xs, bm, cm, u, dt_bias, a_log, s0)
    return res if need_y else (None, res[0])


def _ssd_fin_kernel(yf_ref, yb_ref, xs_ref, z_ref, d_ref, g_ref, o_ref):
    z = z_ref[...]
    y = (yf_ref[...] + yb_ref[...] + d_ref[...] * xs_ref[...]) * (z * _sigmoid(z))
    o_ref[...] = _rms(y, g_ref[...]).astype(o_ref.dtype)


def _ssd_finish(yf, yb, xs, u, row0, d_skip, norm_g):
    nrows = yf.shape[0]
    tq = _tile(nrows, 512)
    blk0 = row0 // tq
    tok = pl.BlockSpec((tq, SSD_DIM), lambda i: (i, 0))
    return pl.pallas_call(
        _ssd_fin_kernel,
        grid=(nrows // tq,),
        in_specs=[tok, tok, tok, pl.BlockSpec((tq, U_Z_W), lambda i: (blk0 + i, U_Z // U_Z_W)),
                  _const_spec((1, SSD_DIM)), _const_spec((1, SSD_DIM))],
        out_specs=tok,
        out_shape=jax.ShapeDtypeStruct((nrows, SSD_DIM), BF16),
        compiler_params=_params("parallel"),
    )(yf, yb, xs, u, d_skip, norm_g)


def _rwkv_prep_kernel(cur_ref, prev_ref, next_ref, mu_ref, w0_ref, w2_ref, a0_ref, a2_ref, g2_ref,
                      kk_ref, ka_ref, rk_ref, bd_ref,
                      r_o, lwf_o, lwb_o, k_o, v_o, kk_o, ka_o, g_o, bonus_o, scr, *, tq, tiles_per_seq):
    _fill_halo(scr, cur_ref, prev_ref, next_ref, tq, tiles_per_seq)
    u = cur_ref[...]
    shift = 0.5 * (scr[pl.ds(SUBLANE - 1, tq), :] + scr[pl.ds(SUBLANE + 1, tq), :]) - u
    x = u + shift * mu_ref[...]
    r = x[:, 0:RWKV_DIM]
    k = x[:, RWKV_DIM:2 * RWKV_DIM]
    v = x[:, 2 * RWKV_DIM:3 * RWKV_DIM]
    bd = bd_ref[...]
    for d, (off, out) in enumerate(((RW_WF, lwf_o), (RW_WB, lwb_o))):
        lo = jnp.tanh(x[:, off:off + LANE])
        wl = -_softplus(-(w0_ref[d:d + 1, :] + _dot(lo, w2_ref[d]))) - 0.5
        out[...] = -jnp.exp(wl)
    a = _sigmoid(a0_ref[...] + _dot(x[:, RW_AL:RW_AL + LANE], a2_ref[...]))
    kk = k * kk_ref[...]
    kk = kk * lax.rsqrt(_blocksum(kk * kk, bd) + 1e-12)
    kmod = k * (1.0 + (a - 1.0) * ka_ref[...])
    r_o[...] = r
    k_o[...] = kmod
    v_o[...] = v
    kk_o[...] = kk
    ka_o[...] = kk * a
    g_o[...] = _dot(_sigmoid(x[:, RW_GL:RW_GL + GATE_LORA]), g2_ref[...])
    bonus_o[...] = _blocksum(r * kmod * rk_ref[...], bd) * v


def _rwkv_prep(u, row0, nseq, seqlen, p):
    nrows = nseq * seqlen
    tq = _tile(seqlen, 256)
    cur, prev, nxt = _halo_specs(tq, U_RWKV_W, row0 // tq, 0, u.shape[0] // SUBLANE)
    consts = [p["mu"], p["w0"], p["w2"], p["a0"], p["a2"], p["g2"], p["k_k"], p["k_a"], p["r_k"], p["bd"]]
    tok = pl.BlockSpec((tq, RWKV_DIM), lambda i: (i, 0))
    return pl.pallas_call(
        functools.partial(_rwkv_prep_kernel, tq=tq, tiles_per_seq=seqlen // tq),
        grid=(nrows // tq,),
        in_specs=[cur, prev, nxt] + [_const_spec(c.shape) for c in consts],
        out_specs=[tok] * 9,
        out_shape=[jax.ShapeDtypeStruct((nrows, RWKV_DIM), F32)] * 9,
        scratch_shapes=[pltpu.VMEM((tq + 2 * SUBLANE, U_RWKV_W), F32)],
        compiler_params=_params("parallel"),
    )(u, u, u, *consts)


def _rwkv_scan_kernel(*refs, C, need_y):
    ins = (refs[0:6], refs[6:12])
    s0_refs = refs[12:14]
    if need_y:
        y_refs, sfin_refs, st = refs[14:16], refs[16:18], refs[18]
    else:
        sfin_refs, st = refs[14:16], refs[16]

    @pl.when(pl.program_id(1) == 0)
    def _():
        st[0] = s0_refs[0][...]
        st[1] = s0_refs[1][...]

    n2 = 2 * C
    row = lax.broadcasted_iota(jnp.int32, (n2, n2), 0)
    col = lax.broadcasted_iota(jnp.int32, (n2, n2), 1)
    t, s = row & (C - 1), col & (C - 1)
    strict = ((s < t), (s > t))
    incl = ((s <= t), (s >= t))
    eye = jnp.where(row == col, 1.0, 0.0)
    ti = lax.broadcasted_iota(jnp.int32, (C, C), 0)
    tk = lax.broadcasted_iota(jnp.int32, (C, C), 1)
    tri = (jnp.where(tk <= ti, 1.0, 0.0).astype(BF16), jnp.where(tk >= ti, 1.0, 0.0).astype(BF16))
    head0 = lax.broadcasted_iota(jnp.int32, (C, LANE), 1) < RWKV_HEAD_DIM

    def stack(x):
        return jnp.concatenate([jnp.where(head0, x, 0.0), jnp.where(head0, 0.0, x)], axis=0)

    probs = [(d, p) for d in (0, 1) for p in range(RWKV_PAIRS)]
    ld = lambda d, p, i: ins[d][i][:, p * LANE:(p + 1) * LANE]
    lw = [ld(d, p, 1) for d, p in probs]
    cs = [_dot_exact_lhs(tri[d], x) for (d, p), x in zip(probs, lw)]
    tot = [x[C - 1:C, :] if d == 0 else x[0:1, :] for (d, p), x in zip(probs, cs)]
    ems = [jnp.exp(-x) for x in cs]
    edec = [jnp.exp(a - x) for a, x in zip(tot, cs)]
    kt = [stack(ld(d, p, 4) * jnp.exp(x - w)) for (d, p), x, w in zip(probs, cs, lw)]
    rt = [stack(ld(d, p, 0) * jnp.exp(x)) for (d, p), x in zip(probs, cs)]
    at = [stack(ld(d, p, 5) * e) for (d, p), e in zip(probs, ems)]
    kb = [stack(ld(d, p, 2) * e) for (d, p), e in zip(probs, ems)]
    at2 = [stack(ld(d, p, 5) * e) for (d, p), e in zip(probs, edec)]
    kb2 = [stack(ld(d, p, 2) * e) for (d, p), e in zip(probs, edec)]
    vs_t = [stack(ld(d, p, 3)).T for d, p in probs]
    big = [_dot_nt(jnp.concatenate([a, b], axis=0), jnp.concatenate([c, e], axis=0))
           for a, b, c, e in zip(kt, rt, at, kb)]
    npow = [-jnp.where(strict[d], x[:n2, :n2], 0.0) for (d, p), x in zip(probs, big)]
    tinv = [eye + x for x in npow]
    for _ in range(int(math.log2(C)) - 1):
        npow = [_dot(x, x) for x in npow]
        tinv = [x + _dot(x, y) for x, y in zip(tinv, npow)]
    state = [st[d, p] for d, p in probs]
    lhs = [jnp.concatenate(
        [jnp.concatenate([a, jnp.where(strict[d], x[:n2, n2:], 0.0)], axis=1),
         jnp.concatenate([b, jnp.where(incl[d], x[n2:, n2:], 0.0)], axis=1)], axis=0)
        for (d, p), x, a, b in zip(probs, big, kt, rt)]
    xy = [_dot_nt(a, jnp.concatenate([x, v], axis=1)) for a, x, v in zip(lhs, state, vs_t)]
    u_m = [-_dot(x, y[:n2]) for x, y in zip(tinv, xy)]
    if need_y:
        for (d, p), x, u, y in zip(probs, big, u_m, xy):
            yy = y[n2:] + _dot(jnp.where(incl[d], x[n2:, :n2], 0.0), u)
            y_refs[d][:, p * LANE:(p + 1) * LANE] = yy[:C] + yy[C:]
    for (d, p), x, a, u, v, k2, a2 in zip(probs, state, tot, u_m, vs_t, kb2, at2):
        st[d, p] = x * jnp.exp(a) + _dot(jnp.concatenate([u.T, v], axis=1), jnp.concatenate([a2, k2], axis=0))
    sfin_refs[0][...] = st[0]
    sfin_refs[1][...] = st[1]


def _rwkv_scan(r, lwf, lwb, k, v, kk, ka, nseq, seqlen, s0f, s0b, need_y):
    C = RWKV_CHUNK
    nc = seqlen // C
    tok_f = pl.BlockSpec((C, RWKV_DIM), lambda b, c: (b * nc + c, 0))
    tok_b = pl.BlockSpec((C, RWKV_DIM), lambda b, c: (b * nc + nc - 1 - c, 0))
    sshape = (RWKV_PAIRS, LANE, LANE)
    sspec = pl.BlockSpec((None,) + sshape, lambda b, c: (b, 0, 0, 0))
    sstruct = jax.ShapeDtypeStruct((nseq,) + sshape, F32)
    out_specs, out_shape = [sspec, sspec], [sstruct, sstruct]
    if need_y:
        ystruct = jax.ShapeDtypeStruct((nseq * seqlen, RWKV_DIM), F32)
        out_specs, out_shape = [tok_f, tok_b] + out_specs, [ystruct, ystruct] + out_shape
    res = pl.pallas_call(
        functools.partial(_rwkv_scan_kernel, C=C, need_y=need_y),
        grid=(nseq, nc),
        in_specs=[tok_f] * 6 + [tok_b] * 6 + [sspec, sspec],
        out_specs=out_specs, out_shape=out_shape,
        scratch_shapes=[pltpu.VMEM((2,) + sshape, F32)],
        compiler_params=_params("parallel", "arbitrary"),
    )(r, lwf, k, v, kk, ka, r, lwb, k, v, kk, ka, s0f, s0b)
    return tuple(res) if need_y else (None, None) + tuple(res)


def _rwkv_fin_kernel(yf_ref, yb_ref, bonus_ref, g_ref, lg_ref, lb_ref, bd_ref, o_ref):
    y = yf_ref[...] + yb_ref[...]
    bd = bd_ref[...]
    dv = y - _blocksum(y, bd) * (1.0 / RWKV_HEAD_DIM)
    var = _blocksum(dv * dv, bd) * (1.0 / RWKV_HEAD_DIM)
    yn = dv * lax.rsqrt(var + GN_EPS) * lg_ref[...] + lb_ref[...]
    o_ref[...] = ((yn + bonus_ref[...]) * g_ref[...]).astype(o_ref.dtype)


def _rwkv_finish(yf, yb, bonus, g, ln_g, ln_b, bd):
    nrows = yf.shape[0]
    tq = _tile(nrows, 512)
    tok = pl.BlockSpec((tq, RWKV_DIM), lambda i: (i, 0))
    return pl.pallas_call(
        _rwkv_fin_kernel,
        grid=(nrows // tq,),
        in_specs=[tok] * 4 + [_const_spec((1, RWKV_DIM))] * 2 + [_const_spec(bd.shape)],
        out_specs=tok,
        out_shape=jax.ShapeDtypeStruct((nrows, RWKV_DIM), BF16),
        compiler_params=_params("parallel"),
    )(yf, yb, bonus, g, ln_g, ln_b, bd)


def _to_cols(t, b, rows):
    return t.reshape(b, rows, GRID_W, -1).transpose(0, 2, 1, 3).reshape(b * rows * GRID_W, -1)


def _to_rows(t, b, rows):
    return t.reshape(b, GRID_W, rows, -1).transpose(0, 2, 1, 3).reshape(b * rows * GRID_W, -1)


def _pad_cols(w, n):
    return jnp.pad(w, [(0, 0)] * (w.ndim - 1) + [(0, n - w.shape[-1])])


def _pad_rows(w, n):
    return jnp.pad(w, [(0, 0)] * (w.ndim - 2) + [(0, n - w.shape[-2]), (0, 0)])


def kernel(x, c, ctx, c_ctx, mod_w, mod_b, norm_g, w_in, conv_w, conv_b, conv_ln_g, conv_ln_b, conv_out, ssd_conv_w, ssd_conv_b, ssd_A_log, ssd_dt_bias, ssd_D, ssd_norm_g, ssd_out, fourier_out, rwkv_mu, rwkv_w0, rwkv_w2, rwkv_a0, rwkv_a2, rwkv_g2, rwkv_k_k, rwkv_k_a, rwkv_r_k, rwkv_ln_g, rwkv_ln_b, rwkv_out, w_o, mlp_up, mlp_down):
    B, L, D = x.shape
    CL = ctx.shape[1]
    depth = mod_w.shape[0]
    rows = L // GRID_W
    n_lat, n_ctx = B * L, B * CL
    n_all = n_lat + n_ctx
    tm = _tile(math.gcd(L, n_ctx), 1024)
    tm_small = _tile(tm, 512)
    tm_merge = _tile(tm, 256)
    mod_idx = lambda t: (lambda i: jnp.minimum((i * t) // L, B))

    o_ssd, o_rwkv, o_conv, o_fft = SSD_IN, REC_IN, REC_IN + 2 * CONV_DIM, REC_IN + 2 * CONV_DIM + FOURIER_DIM
    w_in16 = lax.optimization_barrier(w_in.astype(BF16))
    rw = w_in16[:, :, o_ssd:o_rwkv]
    w_rw = jnp.concatenate(
        [rw[:, :, :3 * RWKV_DIM]]
        + [_pad_cols(rw[:, :, a:b], LANE) for a, b in ((1536, 1600), (1600, 1664), (1664, 1728))]
        + [rw[:, :, 1728:]], axis=2)
    zcols = lambda n: jnp.zeros((depth, D, n), BF16)
    w_u = jnp.concatenate(
        [w_in16[:, :, SSD_DIM:SSD_DIM + SSD_XBC], _pad_cols(w_in16[:, :, SSD_DIM + SSD_XBC:SSD_IN], U_DT_W),
         zcols(U_CONV - U_DT - U_DT_W), w_in16[:, :, o_rwkv:o_conv], w_in16[:, :, :SSD_DIM],
         zcols(U_FFT - U_Z - U_Z_W), w_in16[:, :, o_conv:o_fft]], axis=2)
    w_gate = w_in16[:, :, o_fft:]
    up_w, down_w, wo_w = mlp_up.astype(BF16), mlp_down.astype(BF16), w_o.astype(BF16)
    branch_w = [conv_out.astype(BF16), ssd_out.astype(BF16), fourier_out.astype(BF16), rwkv_out.astype(BF16)]
    mu = rwkv_mu
    mu_cols = jnp.concatenate(
        [mu[:, :3 * RWKV_DIM]] + [_pad_cols(mu[:, a:b], LANE) for a, b in ((1536, 1600), (1600, 1664), (1664, 1728))]
        + [mu[:, 1728:]], axis=1)
    head_of = jnp.arange(RWKV_DIM, dtype=jnp.int32) // RWKV_HEAD_DIM
    bd = (head_of[:, None] == head_of[None, :]).astype(BF16)
    gidx = jnp.arange(FOURIER_DIM, dtype=jnp.int32)
    cc, sc_ = _dft_mats(FOURIER_GROUP_DIM, FOURIER_GROUP_DIM ** -0.5)
    same_group = (gidx[:, None] // FOURIER_GROUP_DIM) == (gidx[None, :] // FOURIER_GROUP_DIM)
    tile_g = lambda m: jnp.where(same_group, jnp.tile(m, (FOURIER_GROUPS, FOURIER_GROUPS)), 0.0)
    chan_w = jnp.concatenate([tile_g(cc), -tile_g(sc_)], axis=1).astype(BF16)
    conv_w_p = _pad_rows(conv_w, 32)
    ssd_conv_w_p = _pad_rows(ssd_conv_w, SUBLANE)
    dt_bias_p = _pad_cols(ssd_dt_bias.reshape(depth, 1, 2 * SSD_HEADS), U_DT_W)
    a_log_p = _pad_cols(ssd_A_log.reshape(depth, 1, 2 * SSD_HEADS), U_DT_W)
    d_skip = jnp.repeat(ssd_D, SSD_HEAD_DIM, axis=1).reshape(depth, 1, SSD_DIM)

    cond = jnp.concatenate([c, c_ctx[None, :], jnp.zeros((SUBLANE - (B + 1) % SUBLANE, D), F32)], axis=0)
    mod = _mod_all(cond, mod_w, mod_b)

    xs = jnp.concatenate([x.reshape(n_lat, D), ctx.reshape(n_ctx, D)], axis=0)
    ssd_zero = jnp.zeros((B, SSD_GROUPS, SSD_STATE, SSD_HEADS_PER_GROUP * SSD_HEAD_DIM), F32)
    rwkv_zero = jnp.zeros((B, RWKV_PAIRS, LANE, LANE), F32)

    for i in range(depth):
        last = i == depth - 1
        m_out = n_lat if last else n_all
        sh1, sc1, g1, sh2, sc2, g2 = (mod[i, :, j * D:(j + 1) * D].reshape(-1, 1, D) for j in range(6))
        ng = norm_g[i].reshape(4, 1, D)

        h = _norm_mod(xs, ng[0], sc1, sh1, n_all, mod_idx(tm), tm)
        u = _matmul(h, w_u[i], n_all, tm, F32, tn_cap=1536)
        gates = _matmul(h, w_gate[i], m_out, tm, BF16, epilogue="sigmoid")

        a_conv = _conformer(u, 0, n_lat, GRID_W, conv_w_p[i], conv_b[i][None], conv_ln_g[i][None], conv_ln_b[i][None])
        z = _matmul(u, chan_w, m_out, tm, BF16, a_col=U_FFT // U_FFT_W)
        a_fft = _fourier(z, 0, B, L)
        if not last:
            a_conv = jnp.concatenate(
                [a_conv, _conformer(u, n_lat, n_ctx, CL, conv_w_p[i], conv_b[i][None], conv_ln_g[i][None],
                                    conv_ln_b[i][None])], axis=0)
            a_fft = jnp.concatenate([a_fft, _fourier(z, n_lat, B, CL)], axis=0)

        ssd_args = (dt_bias_p[i], a_log_p[i])
        xs_c, bm_c, cm_c = _ssd_conv(u, n_lat, B, CL, ssd_conv_w_p[i], ssd_conv_b[i][None])
        yfc, sf = _ssd_scan(xs_c, bm_c, cm_c, u, n_lat, B, CL, *ssd_args, ssd_zero, False, not last)
        ybc, sb = _ssd_scan(xs_c, bm_c, cm_c, u, n_lat, B, CL, *ssd_args, ssd_zero, True, not last)
        xs_l, bm_l, cm_l = _ssd_conv(u, 0, B, L, ssd_conv_w_p[i], ssd_conv_b[i][None])
        yfl, _ = _ssd_scan(xs_l, bm_l, cm_l, u, 0, B, L, *ssd_args, sf, False, True)
        ybl, _ = _ssd_scan(xs_l, bm_l, cm_l, u, 0, B, L, *ssd_args, sb, True, True)
        a_ssd = _ssd_finish(yfl, ybl, xs_l, u, 0, d_skip[i], ssd_norm_g[i][None])
        if not last:
            a_ssd = jnp.concatenate(
                [a_ssd, _ssd_finish(yfc, ybc, xs_c, u, n_lat, d_skip[i], ssd_norm_g[i][None])], axis=0)

        rp = dict(mu=mu_cols[i][None], w0=rwkv_w0[i], w2=_pad_rows(rwkv_w2[i], LANE).astype(BF16),
                  a0=rwkv_a0[i][None], a2=_pad_rows(rwkv_a2[i], LANE).astype(BF16), g2=rwkv_g2[i].astype(BF16),
                  k_k=rwkv_k_k[i][None], k_a=rwkv_k_a[i][None], r_k=rwkv_r_k[i].reshape(1, RWKV_DIM), bd=bd)
        u_rc = _matmul(h, w_rw[i], n_ctx, tm, F32, a_row=n_lat // tm)
        r_, lwf, lwb, k_, v_, kk_, ka_, g_, bonus = _rwkv_prep(u_rc, 0, B, CL, rp)
        yfc, ybc, s_f, s_b = _rwkv_scan(r_, lwf, lwb, k_, v_, kk_, ka_, B, CL, rwkv_zero, rwkv_zero, not last)
        if not last:
            a_rwkv_c = _rwkv_finish(yfc, ybc, bonus, g_, rwkv_ln_g[i][None], rwkv_ln_b[i][None], bd)
        u_rl = _matmul(_to_cols(h[:n_lat], B, rows), w_rw[i], n_lat, tm, F32)
        r_, lwf, lwb, k_, v_, kk_, ka_, g_, bonus = _rwkv_prep(u_rl, 0, B, L, rp)
        yfl, ybl, _, _ = _rwkv_scan(r_, lwf, lwb, k_, v_, kk_, ka_, B, L, s_f, s_b, True)
        a_rwkv = _to_rows(_rwkv_finish(yfl, ybl, bonus, g_, rwkv_ln_g[i][None], rwkv_ln_b[i][None], bd), B, rows)
        if not last:
            a_rwkv = jnp.concatenate([a_rwkv, a_rwkv_c], axis=0)

        m = _merge(gates, [a_conv, a_ssd, a_fft, a_rwkv], [w[i] for w in branch_w], m_out, tm_merge)
        xs = _matmul_residual(m, wo_w[i], xs, g1, ng[1], m_out, mod_idx(tm_small), tm_small)

        h2 = _norm_mod(xs, ng[2], sc2, sh2, m_out, mod_idx(tm), tm)
        mid = _matmul(h2, up_w[i], m_out, tm, BF16, epilogue="relu2")
        xs = _matmul_residual(mid, down_w[i], xs, g2, ng[3], m_out, mod_idx(tm_small), tm_small)

    return xs[:n_lat].reshape(B, L, D)
```

```python
import functools
import math

import jax
import jax.numpy as jnp
from jax import lax
from jax.experimental import pallas as pl
from jax.experimental.pallas import tpu as pltpu

F32 = jnp.float32
BF16 = jnp.bfloat16

GRID_W = 64
RMS_EPS = 1e-6
LN_EPS = 1e-5
GN_EPS = 64e-5
CONV_DIM = 512
CONV_WIDTH = 31
SSD_HEADS = 12
SSD_HEAD_DIM = 64
SSD_DIM = SSD_HEADS * SSD_HEAD_DIM
SSD_GROUPS = 4
SSD_HEADS_PER_GROUP = SSD_HEADS // SSD_GROUPS
SSD_STATE = 128
SSD_CONV = 5
SSD_BC = SSD_GROUPS * SSD_STATE
SSD_XBC = SSD_DIM + 2 * SSD_BC
SSD_IN = SSD_DIM + SSD_XBC + 2 * SSD_HEADS
SSD_CHUNK = 128
FOURIER_GROUPS = 4
FOURIER_GROUP_DIM = 128
FOURIER_DIM = FOURIER_GROUPS * FOURIER_GROUP_DIM
RWKV_HEADS = 8
RWKV_HEAD_DIM = 64
RWKV_DIM = RWKV_HEADS * RWKV_HEAD_DIM
RWKV_PAIRS = RWKV_HEADS // 2
DECAY_LORA = 64
ICL_LORA = 64
GATE_LORA = 128
RWKV_IN = 3 * RWKV_DIM + 2 * DECAY_LORA + ICL_LORA + GATE_LORA
RWKV_CHUNK = 64
REC_IN = SSD_IN + RWKV_IN
N_BRANCH = 4

LANE = 128
SUBLANE = 8
VMEM_LIMIT = 56 * 1024 * 1024

U_XBC, U_XBC_W = 0, SSD_XBC
U_DT, U_DT_W = 1792, LANE
U_CONV, U_CONV_W = 2048, 2 * CONV_DIM
U_Z, U_Z_W = 3072, SSD_DIM
U_FFT, U_FFT_W = 4096, FOURIER_DIM
U_N = 4608
U_RWKV_W = 2048
RW_WF, RW_WB, RW_AL, RW_GL = 1536, 1664, 1792, 1920


def _params(*sem):
    return pltpu.CompilerParams(dimension_semantics=sem, vmem_limit_bytes=VMEM_LIMIT)


def _tile(n, cap, quantum=SUBLANE):
    if n <= cap:
        return n
    t = (cap // quantum) * quantum
    while t >= quantum:
        if n % t == 0:
            return t
        t -= quantum
    raise ValueError(f"no tile for {n} under {cap}")


def _const_spec(shape):
    nd = len(shape)
    return pl.BlockSpec(shape, lambda *_: (0,) * nd)


def _dot(a, b):
    return jnp.dot(a.astype(BF16), b.astype(BF16), preferred_element_type=F32)


def _dot_nt(a, b):
    return lax.dot_general(a.astype(BF16), b.astype(BF16), (((1,), (1,)), ((), ())),
                           preferred_element_type=F32)


def _split3(x):
    h1 = x.astype(BF16)
    r1 = x - h1.astype(F32)
    h2 = r1.astype(BF16)
    h3 = (r1 - h2.astype(F32)).astype(BF16)
    return h1, h2, h3


def _dot_exact_lhs(m, x):
    h1, h2, h3 = _split3(x)
    d = lambda h: jnp.dot(m, h, preferred_element_type=F32)
    return d(h1) + d(h2) + d(h3)


def _blocksum(x, bd):
    hi = x.astype(BF16)
    lo = (x - hi.astype(F32)).astype(BF16)
    return (jnp.dot(hi, bd, preferred_element_type=F32) + jnp.dot(lo, bd, preferred_element_type=F32))


def _sigmoid(x):
    return 1.0 / (1.0 + jnp.exp(-x))


def _softplus(x):
    return jnp.maximum(x, 0.0) + jnp.log1p(jnp.exp(-jnp.abs(x)))


def _rms(y, g):
    return y * lax.rsqrt(jnp.mean(y * y, axis=-1, keepdims=True) + RMS_EPS) * g


def _mod_kernel(a_ref, w_ref, b_ref, o_ref):
    a = a_ref[...]
    a = a * _sigmoid(a)
    o_ref[...] = _dot(a, w_ref[...]) + b_ref[...]


def _mod_all(cond, mod_w, mod_b):
    depth, d, n = mod_w.shape
    tn = _tile(n, 1024, LANE)
    return pl.pallas_call(
        _mod_kernel,
        grid=(depth, n // tn),
        in_specs=[_const_spec(cond.shape),
                  pl.BlockSpec((None, d, tn), lambda l, j: (l, 0, j)),
                  pl.BlockSpec((None, 1, tn), lambda l, j: (l, 0, j))],
        out_specs=pl.BlockSpec((None, cond.shape[0], tn), lambda l, j: (l, 0, j)),
        out_shape=jax.ShapeDtypeStruct((depth, cond.shape[0], n), F32),
        compiler_params=_params("parallel", "parallel"),
    )(cond, mod_w, mod_b.reshape(depth, 1, n))


def _norm_mod_kernel(x_ref, g_ref, sc_ref, sh_ref, o_ref):
    o_ref[...] = (_rms(x_ref[...], g_ref[...]) * (1.0 + sc_ref[...]) + sh_ref[...]).astype(o_ref.dtype)


def _norm_mod(x, g, sc, sh, m_rows, mod_idx, tm):
    d = x.shape[1]
    return pl.pallas_call(
        _norm_mod_kernel,
        grid=(m_rows // tm,),
        in_specs=[pl.BlockSpec((tm, d), lambda i: (i, 0)),
                  _const_spec((1, d)),
                  pl.BlockSpec((None, 1, d), lambda i: (mod_idx(i), 0, 0)),
                  pl.BlockSpec((None, 1, d), lambda i: (mod_idx(i), 0, 0))],
        out_specs=pl.BlockSpec((tm, d), lambda i: (i, 0)),
        out_shape=jax.ShapeDtypeStruct((m_rows, d), BF16),
        compiler_params=_params("parallel"),
    )(x, g, sc, sh)


def _mm_kernel(a_ref, w_ref, o_ref, *, epilogue):
    acc = _dot(a_ref[...], w_ref[...])
    if epilogue == "relu2":
        acc = jnp.square(jnp.maximum(acc, 0.0))
    elif epilogue == "sigmoid":
        acc = _sigmoid(acc)
    o_ref[...] = acc.astype(o_ref.dtype)


def _matmul(a, w, m_rows, tm, out_dtype, epilogue=None, a_col=0, a_row=0, tn_cap=1024):
    k, n = w.shape
    tn = _tile(n, tn_cap, LANE)
    return pl.pallas_call(
        functools.partial(_mm_kernel, epilogue=epilogue),
        grid=(m_rows // tm, n // tn),
        in_specs=[pl.BlockSpec((tm, k), lambda i, j: (a_row + i, a_col)),
                  pl.BlockSpec((k, tn), lambda i, j: (0, j))],
        out_specs=pl.BlockSpec((tm, tn), lambda i, j: (i, j)),
        out_shape=jax.ShapeDtypeStruct((m_rows, n), out_dtype),
        compiler_params=_params("parallel", "arbitrary"),
    )(a, w)


def _mm_res_kernel(a_ref, w_ref, x_ref, gate_ref, g_ref, *rest, with_next):
    if with_next:
        g2_ref, sc_ref, sh_ref, o_ref, h_ref = rest
    else:
        (o_ref,) = rest
    k = pl.program_id(1)
    part = _dot(a_ref[...], w_ref[...])

    @pl.when(k == 0)
    def _():
        o_ref[...] = part

    @pl.when(k > 0)
    def _():
        o_ref[...] += part

    @pl.when(k == pl.num_programs(1) - 1)
    def _():
        xn = x_ref[...] + gate_ref[...] * _rms(o_ref[...], g_ref[...])
        o_ref[...] = xn
        if with_next:
            h_ref[...] = (_rms(xn, g2_ref[...]) * (1.0 + sc_ref[...]) + sh_ref[...]).astype(h_ref.dtype)


def _matmul_residual(a, w, x, gate, g, m_rows, mod_idx, tm, nxt=None, tk_cap=2048):
    k, d = w.shape
    tk = _tile(k, tk_cap, LANE)
    mod_spec = pl.BlockSpec((None, 1, d), lambda i, kk: (mod_idx(i), 0, 0))
    tile_spec = pl.BlockSpec((tm, d), lambda i, kk: (i, 0))
    in_specs = [pl.BlockSpec((tm, tk), lambda i, kk: (i, kk)), pl.BlockSpec((tk, d), lambda i, kk: (kk, 0)),
                tile_spec, mod_spec, _const_spec((1, d))]
    out_specs, out_shape, args = tile_spec, jax.ShapeDtypeStruct((m_rows, d), F32), (a, w, x, gate, g)
    if nxt is not None:
        in_specs += [_const_spec((1, d)), mod_spec, mod_spec]
        out_specs, out_shape = [tile_spec, tile_spec], [out_shape, jax.ShapeDtypeStruct((m_rows, d), BF16)]
        args += tuple(nxt)
    return pl.pallas_call(
        functools.partial(_mm_res_kernel, with_next=nxt is not None),
        grid=(m_rows // tm, k // tk),
        in_specs=in_specs, out_specs=out_specs, out_shape=out_shape,
        compiler_params=_params("parallel", "arbitrary"),
    )(*args)


def _merge_kernel(gates_ref, a0, a1, a2, a3, w0, w1, w2, w3, o_ref):
    d = o_ref.shape[-1]
    m = None
    for i, (a, w) in enumerate(((a0, w0), (a1, w1), (a2, w2), (a3, w3))):
        t = gates_ref[:, i * d:(i + 1) * d].astype(F32) * _dot(a[...], w[...])
        m = t if m is None else m + t
    o_ref[...] = m.astype(o_ref.dtype)


def _merge(gates, acts, weights, m_rows, tm):
    d = weights[0].shape[1]
    return pl.pallas_call(
        _merge_kernel,
        grid=(m_rows // tm,),
        in_specs=[pl.BlockSpec((tm, N_BRANCH * d), lambda i: (i, 0))]
        + [pl.BlockSpec((tm, a.shape[1]), lambda i: (i, 0)) for a in acts]
        + [_const_spec(w.shape) for w in weights],
        out_specs=pl.BlockSpec((tm, d), lambda i: (i, 0)),
        out_shape=jax.ShapeDtypeStruct((m_rows, d), BF16),
        compiler_params=_params("parallel"),
    )(gates, *acts, *weights)


def _conformer_kernel(u_ref, w_ref, b_ref, lg_ref, lb_ref, o_ref, pad_ref, *, seg, nseg):
    pad = 16
    zeros = jnp.zeros((pad, CONV_DIM), F32)
    pad_ref[0:pad, :] = zeros
    pad_ref[pad + seg:2 * pad + seg, :] = zeros

    def body(s, carry):
        r0 = pl.multiple_of(s * seg, seg)
        u = u_ref[pl.ds(r0, seg), :]
        pad_ref[pad:pad + seg, :] = u[:, :CONV_DIM] * _sigmoid(u[:, CONV_DIM:])
        acc = jnp.zeros((seg, CONV_DIM), F32) + b_ref[...]
        for j in range(CONV_WIDTH):
            acc = acc + w_ref[j:j + 1, :] * pad_ref[pl.ds(pad - (CONV_WIDTH - 1) // 2 + j, seg), :]
        mu = jnp.mean(acc, axis=-1, keepdims=True)
        dv = acc - mu
        var = jnp.mean(dv * dv, axis=-1, keepdims=True)
        v = dv * lax.rsqrt(var + LN_EPS) * lg_ref[...] + lb_ref[...]
        o_ref[pl.ds(r0, seg), :] = (v * _sigmoid(v)).astype(o_ref.dtype)
        return carry

    lax.fori_loop(0, nseg, body, 0)


def _conformer(u, row0, nrows, seg, conv_w, conv_b, ln_g, ln_b):
    tq = max(seg, min(256, nrows))
    blk0 = row0 // tq
    return pl.pallas_call(
        functools.partial(_conformer_kernel, seg=seg, nseg=tq // seg),
        grid=(nrows // tq,),
        in_specs=[pl.BlockSpec((tq, U_CONV_W), lambda i: (blk0 + i, U_CONV // U_CONV_W)),
                  _const_spec(conv_w.shape), _const_spec((1, CONV_DIM)),
                  _const_spec((1, CONV_DIM)), _const_spec((1, CONV_DIM))],
        out_specs=pl.BlockSpec((tq, CONV_DIM), lambda i: (i, 0)),
        out_shape=jax.ShapeDtypeStruct((nrows, CONV_DIM), BF16),
        scratch_shapes=[pltpu.VMEM((seg + 32, CONV_DIM), F32)],
        compiler_params=_params("parallel"),
    )(u, conv_w, conv_b, ln_g, ln_b)


def _dft_kernel(c_ref, s_ref, z_ref, o_ref):
    o_ref[...] = (jnp.dot(c_ref[...], z_ref[:, :FOURIER_DIM], preferred_element_type=F32)
                  + jnp.dot(s_ref[...], z_ref[:, FOURIER_DIM:], preferred_element_type=F32)).astype(o_ref.dtype)


def _dft_mats(n, scale):
    idx = jnp.arange(n, dtype=jnp.int32)
    ang = ((idx[:, None] * idx[None, :]) % n).astype(F32) * (2.0 * math.pi / n)
    return jnp.cos(ang) * scale, jnp.sin(ang) * scale


def _fourier(z, row0, nseq, seqlen):
    nrows = nseq * seqlen
    cl, sl = _dft_mats(seqlen, seqlen ** -0.5)
    cl, sl = cl.astype(BF16), sl.astype(BF16)
    tm = _tile(seqlen, 512)
    nt = seqlen // tm
    sblk0 = row0 // seqlen
    return pl.pallas_call(
        _dft_kernel,
        grid=(nt, nseq),
        in_specs=[pl.BlockSpec((tm, seqlen), lambda i, b: (i, 0)),
                  pl.BlockSpec((tm, seqlen), lambda i, b: (i, 0)),
                  pl.BlockSpec((seqlen, 2 * FOURIER_DIM), lambda i, b: (sblk0 + b, 0))],
        out_specs=pl.BlockSpec((tm, FOURIER_DIM), lambda i, b: (b * nt + i, 0)),
        out_shape=jax.ShapeDtypeStruct((nrows, FOURIER_DIM), BF16),
        compiler_params=_params("parallel", "arbitrary"),
    )(cl, sl, z)


def _halo_specs(tq, width, blk0, col_blk, nblk_total):
    hb = tq // SUBLANE
    cur = pl.BlockSpec((tq, width), lambda i: (blk0 + i, col_blk))
    prev = pl.BlockSpec((SUBLANE, width), lambda i: (jnp.maximum((blk0 + i) * hb - 1, 0), col_blk))
    nxt = pl.BlockSpec((SUBLANE, width),
                       lambda i: (jnp.minimum((blk0 + i + 1) * hb, nblk_total - 1), col_blk))
    return cur, prev, nxt


def _fill_halo(scr, cur_ref, prev_ref, next_ref, tq, tiles_per_seq):
    i = pl.program_id(0)
    first = (i % tiles_per_seq) == 0
    last = (i % tiles_per_seq) == tiles_per_seq - 1
    scr[0:SUBLANE, :] = jnp.where(first, 0.0, prev_ref[...])
    scr[SUBLANE:SUBLANE + tq, :] = cur_ref[...]
    scr[SUBLANE + tq:2 * SUBLANE + tq, :] = jnp.where(last, 0.0, next_ref[...])


def _ssd_conv_kernel(cur_ref, prev_ref, next_ref, w_ref, b_ref, xs_ref, bm_ref, cm_ref, scr, *, tq, tiles_per_seq):
    _fill_halo(scr, cur_ref, prev_ref, next_ref, tq, tiles_per_seq)
    half = (SSD_CONV - 1) // 2
    cw = 256
    for c0 in range(0, SSD_XBC, cw):
        acc = jnp.zeros((tq, cw), F32) + b_ref[:, c0:c0 + cw]
        for j in range(SSD_CONV):
            acc = acc + w_ref[j:j + 1, c0:c0 + cw] * scr[pl.ds(SUBLANE - half + j, tq), c0:c0 + cw]
        y = acc * _sigmoid(acc)
        if c0 < SSD_DIM:
            xs_ref[:, c0:c0 + cw] = y
        elif c0 < SSD_DIM + SSD_BC:
            bm_ref[:, c0 - SSD_DIM:c0 - SSD_DIM + cw] = y
        else:
            cm_ref[:, c0 - SSD_DIM - SSD_BC:c0 - SSD_DIM - SSD_BC + cw] = y


def _ssd_conv(u, row0, nseq, seqlen, conv_w, conv_b):
    nrows = nseq * seqlen
    tq = _tile(seqlen, 256)
    cur, prev, nxt = _halo_specs(tq, U_XBC_W, row0 // tq, U_XBC // U_XBC_W, u.shape[0] // SUBLANE)
    outs = [jax.ShapeDtypeStruct((nrows, w), F32) for w in (SSD_DIM, SSD_BC, SSD_BC)]
    return pl.pallas_call(
        functools.partial(_ssd_conv_kernel, tq=tq, tiles_per_seq=seqlen // tq),
        grid=(nrows // tq,),
        in_specs=[cur, prev, nxt, _const_spec(conv_w.shape), _const_spec((1, SSD_XBC))],
        out_specs=[pl.BlockSpec((tq, w), lambda i: (i, 0)) for w in (SSD_DIM, SSD_BC, SSD_BC)],
        out_shape=outs,
        scratch_shapes=[pltpu.VMEM((tq + 2 * SUBLANE, U_XBC_W), F32)],
        compiler_params=_params("parallel"),
    )(u, u, u, conv_w, conv_b)


def _ssd_scan_kernel(xs_ref, b_ref, c_ref, dt_ref, bias_ref, alog_ref, sel_ref, s0_ref, *rest,
                     q, rev, direction, need_y):
    if need_y:
        y_ref, sfin_ref, st = rest
    else:
        sfin_ref, st = rest

    @pl.when(pl.program_id(1) == 0)
    def _():
        st[...] = s0_ref[...]

    hp, gw = SSD_HEAD_DIM, SSD_HEADS_PER_GROUP * SSD_HEAD_DIM
    dt = _softplus(dt_ref[...] + bias_ref[...])
    a = dt * (-jnp.exp(alog_ref[...]))
    ti = lax.broadcasted_iota(jnp.int32, (q, q), 0)
    tk = lax.broadcasted_iota(jnp.int32, (q, q), 1)
    before = (tk >= ti) if rev else (tk <= ti)
    cs = _dot_exact_lhs(jnp.where(before, 1.0, 0.0).astype(BF16), a)
    sel = sel_ref[...]
    spread = lambda parts: sum(jnp.dot(p, sel, preferred_element_type=F32) for p in parts)
    dt_b = spread(_split3(dt)[:2])
    cs_b = spread(_split3(cs))
    tot_b = cs_b[0:1, :] if rev else cs_b[q - 1:q, :]
    xd = xs_ref[...] * dt_b
    xdd = xd * jnp.exp(tot_b - cs_b)
    dec = jnp.exp(tot_b)
    if need_y:
        ecs = jnp.exp(cs_b)
        cs_t = cs.T
        lane0 = lax.broadcasted_iota(jnp.int32, (q, LANE), 1) < hp
    for g in range(SSD_GROUPS):
        cg = c_ref[:, g * SSD_STATE:(g + 1) * SSD_STATE]
        bg = b_ref[:, g * SSD_STATE:(g + 1) * SSD_STATE]
        state = st[g]
        if need_y:
            cb = _dot_nt(cg, bg)
            y_off = _dot(cg, state) * ecs[:, g * gw:(g + 1) * gw]
            for j in range(SSD_HEADS_PER_GROUP):
                h = g * SSD_HEADS_PER_GROUP + j
                col = direction * SSD_HEADS + h
                pair = cs_b[:, (h // 2) * LANE:(h // 2 + 1) * LANE]
                swapped = pltpu.roll(pair, hp, axis=1)
                cs_col = jnp.where(lane0, pair, swapped) if h % 2 == 0 else jnp.where(lane0, swapped, pair)
                lm = jnp.where(before, jnp.exp(jnp.minimum(cs_col - cs_t[col:col + 1, :], 0.0)), 0.0)
                y_ref[:, h * hp:(h + 1) * hp] = (_dot(cb * lm, xd[:, h * hp:(h + 1) * hp])
                                                 + y_off[:, j * hp:(j + 1) * hp])
        st[g] = state * dec[:, g * gw:(g + 1) * gw] + _dot(bg.T, xdd[:, g * gw:(g + 1) * gw])
    sfin_ref[...] = st[...]


def _ssd_scan(xs, bm, cm, u, row0, nseq, seqlen, dt_bias, a_log, s0, rev, need_y):
    q = SSD_CHUNK
    nc = seqlen // q
    ublk0 = row0 // q
    pos = (lambda c: nc - 1 - c) if rev else (lambda c: c)
    row = lambda b, c: (b * nc + pos(c), 0)
    sshape = (SSD_GROUPS, SSD_STATE, SSD_HEADS_PER_GROUP * SSD_HEAD_DIM)
    sspec = pl.BlockSpec((None,) + sshape, lambda b, c: (b, 0, 0, 0))
    head_col = int(rev) * SSD_HEADS + jnp.arange(SSD_DIM, dtype=jnp.int32) // SSD_HEAD_DIM
    sel = (jnp.arange(U_DT_W, dtype=jnp.int32)[:, None] == head_col[None, :]).astype(BF16)
    out_specs, out_shape = [sspec], [jax.ShapeDtypeStruct((nseq,) + sshape, F32)]
    if need_y:
        out_specs.insert(0, pl.BlockSpec((q, SSD_DIM), row))
        out_shape.insert(0, jax.ShapeDtypeStruct((nseq * seqlen, SSD_DIM), F32))
    res = pl.pallas_call(
        functools.partial(_ssd_scan_kernel, q=q, rev=rev, direction=int(rev), need_y=need_y),
        grid=(nseq, nc),
        in_specs=[pl.BlockSpec((q, SSD_DIM), row), pl.BlockSpec((q, SSD_BC), row), pl.BlockSpec((q, SSD_BC), row),
                  pl.BlockSpec((q, U_DT_W), lambda b, c: (ublk0 + b * nc + pos(c), U_DT // U_DT_W)),
                  _const_spec((1, U_DT_W)), _const_spec((1, U_DT_W)), _const_spec(sel.shape), sspec],
        out_specs=out_specs, out_shape=out_shape,
        scratch_shapes=[pltpu.VMEM(sshape, F32)],
        compiler_params=_params("parallel", "arbitrary"),
    )(xs, bm, cm, u, dt_bias, a_log, sel, s0)
    return res if need_y else (None, res[0])


def _ssd_fin_kernel(yf_ref, yb_ref, xs_ref, z_ref, d_ref, g_ref, o_ref):
    z = z_ref[...]
    y = (yf_ref[...] + yb_ref[...] + d_ref[...] * xs_ref[...]) * (z * _sigmoid(z))
    o_ref[...] = _rms(y, g_ref[...]).astype(o_ref.dtype)


def _ssd_finish(yf, yb, xs, u, row0, d_skip, norm_g):
    nrows = yf.shape[0]
    tq = _tile(nrows, 512)
    blk0 = row0 // tq
    tok = pl.BlockSpec((tq, SSD_DIM), lambda i: (i, 0))
    return pl.pallas_call(
        _ssd_fin_kernel,
        grid=(nrows // tq,),
        in_specs=[tok, tok, tok, pl.BlockSpec((tq, U_Z_W), lambda i: (blk0 + i, U_Z // U_Z_W)),
                  _const_spec((1, SSD_DIM)), _const_spec((1, SSD_DIM))],
        out_specs=tok,
        out_shape=jax.ShapeDtypeStruct((nrows, SSD_DIM), BF16),
        compiler_params=_params("parallel"),
    )(yf, yb, xs, u, d_skip, norm_g)


def _rwkv_prep_kernel(cur_ref, prev_ref, next_ref, mu_ref, w0_ref, w2_ref, a0_ref, a2_ref, g2_ref,
                      kk_ref, ka_ref, rk_ref, bd_ref,
                      r_o, lwf_o, lwb_o, k_o, v_o, kk_o, ka_o, g_o, bonus_o, scr, *, tq, tiles_per_seq):
    _fill_halo(scr, cur_ref, prev_ref, next_ref, tq, tiles_per_seq)
    u = cur_ref[...]
    shift = 0.5 * (scr[pl.ds(SUBLANE - 1, tq), :] + scr[pl.ds(SUBLANE + 1, tq), :]) - u
    x = u + shift * mu_ref[...]
    r = x[:, 0:RWKV_DIM]
    k = x[:, RWKV_DIM:2 * RWKV_DIM]
    v = x[:, 2 * RWKV_DIM:3 * RWKV_DIM]
    bd = bd_ref[...]
    for d, (off, out) in enumerate(((RW_WF, lwf_o), (RW_WB, lwb_o))):
        lo = jnp.tanh(x[:, off:off + LANE])
        wl = -_softplus(-(w0_ref[d:d + 1, :] + _dot(lo, w2_ref[d]))) - 0.5
        out[...] = -jnp.exp(wl)
    a = _sigmoid(a0_ref[...] + _dot(x[:, RW_AL:RW_AL + LANE], a2_ref[...]))
    kk = k * kk_ref[...]
    kk = kk * lax.rsqrt(_blocksum(kk * kk, bd) + 1e-12)
    kmod = k * (1.0 + (a - 1.0) * ka_ref[...])
    r_o[...] = r
    k_o[...] = kmod
    v_o[...] = v
    kk_o[...] = kk
    ka_o[...] = kk * a
    g_o[...] = _dot(_sigmoid(x[:, RW_GL:RW_GL + GATE_LORA]), g2_ref[...])
    bonus_o[...] = _blocksum(r * kmod * rk_ref[...], bd) * v


def _rwkv_prep(u, row0, nseq, seqlen, p):
    nrows = nseq * seqlen
    tq = _tile(seqlen, 256)
    cur, prev, nxt = _halo_specs(tq, U_RWKV_W, row0 // tq, 0, u.shape[0] // SUBLANE)
    consts = [p["mu"], p["w0"], p["w2"], p["a0"], p["a2"], p["g2"], p["k_k"], p["k_a"], p["r_k"], p["bd"]]
    tok = pl.BlockSpec((tq, RWKV_DIM), lambda i: (i, 0))
    return pl.pallas_call(
        functools.partial(_rwkv_prep_kernel, tq=tq, tiles_per_seq=seqlen // tq),
        grid=(nrows // tq,),
        in_specs=[cur, prev, nxt] + [_const_spec(c.shape) for c in consts],
        out_specs=[tok] * 9,
        out_shape=[jax.ShapeDtypeStruct((nrows, RWKV_DIM), F32)] * 9,
        scratch_shapes=[pltpu.VMEM((tq + 2 * SUBLANE, U_RWKV_W), F32)],
        compiler_params=_params("parallel"),
    )(u, u, u, *consts)


def _rwkv_scan_kernel(*refs, C, need_y):
    ins = (refs[0:6], refs[6:12])
    s0_refs = refs[12:14]
    if need_y:
        y_refs, sfin_refs, st = refs[14:16], refs[16:18], refs[18]
    else:
        sfin_refs, st = refs[14:16], refs[16]

    @pl.when(pl.program_id(1) == 0)
    def _():
        st[0] = s0_refs[0][...]
        st[1] = s0_refs[1][...]

    n2 = 2 * C
    row = lax.broadcasted_iota(jnp.int32, (n2, n2), 0)
    col = lax.broadcasted_iota(jnp.int32, (n2, n2), 1)
    t, s = row & (C - 1), col & (C - 1)
    strict = ((s < t), (s > t))
    incl = ((s <= t), (s >= t))
    eye = jnp.where(row == col, 1.0, 0.0)
    ti = lax.broadcasted_iota(jnp.int32, (C, C), 0)
    tk = lax.broadcasted_iota(jnp.int32, (C, C), 1)
    tri = (jnp.where(tk <= ti, 1.0, 0.0).astype(BF16), jnp.where(tk >= ti, 1.0, 0.0).astype(BF16))
    head0 = lax.broadcasted_iota(jnp.int32, (C, LANE), 1) < RWKV_HEAD_DIM

    def stack(x):
        return jnp.concatenate([jnp.where(head0, x, 0.0), jnp.where(head0, 0.0, x)], axis=0)

    probs = [(d, p) for d in (0, 1) for p in range(RWKV_PAIRS)]
    ld = lambda d, p, i: ins[d][i][:, p * LANE:(p + 1) * LANE]
    lw = [ld(d, p, 1) for d, p in probs]
    cs = [_dot_exact_lhs(tri[d], x) for (d, p), x in zip(probs, lw)]
    tot = [x[C - 1:C, :] if d == 0 else x[0:1, :] for (d, p), x in zip(probs, cs)]
    ems = [jnp.exp(-x) for x in cs]
    edec = [jnp.exp(a - x) for a, x in zip(tot, cs)]
    kt = [stack(ld(d, p, 4) * jnp.exp(x - w)) for (d, p), x, w in zip(probs, cs, lw)]
    rt = [stack(ld(d, p, 0) * jnp.exp(x)) for (d, p), x in zip(probs, cs)]
    at = [stack(ld(d, p, 5) * e) for (d, p), e in zip(probs, ems)]
    kb = [stack(ld(d, p, 2) * e) for (d, p), e in zip(probs, ems)]
    at2 = [stack(ld(d, p, 5) * e) for (d, p), e in zip(probs, edec)]
    kb2 = [stack(ld(d, p, 2) * e) for (d, p), e in zip(probs, edec)]
    vs_t = [stack(ld(d, p, 3)).T for d, p in probs]
    big = [_dot_nt(jnp.concatenate([a, b], axis=0), jnp.concatenate([c, e], axis=0))
           for a, b, c, e in zip(kt, rt, at, kb)]
    npow = [-jnp.where(strict[d], x[:n2, :n2], 0.0) for (d, p), x in zip(probs, big)]
    tinv = [eye + x for x in npow]
    for _ in range(int(math.log2(C)) - 1):
        npow = [_dot(x, x) for x in npow]
        tinv = [x + _dot(x, y) for x, y in zip(tinv, npow)]
    state = [st[d, p] for d, p in probs]
    lhs = [jnp.concatenate(
        [jnp.concatenate([a, jnp.where(strict[d], x[:n2, n2:], 0.0)], axis=1),
         jnp.concatenate([b, jnp.where(incl[d], x[n2:, n2:], 0.0)], axis=1)], axis=0)
        for (d, p), x, a, b in zip(probs, big, kt, rt)]
    xy = [_dot_nt(a, jnp.concatenate([x, v], axis=1)) for a, x, v in zip(lhs, state, vs_t)]
    u_m = [-_dot(x, y[:n2]) for x, y in zip(tinv, xy)]
    if need_y:
        for (d, p), x, u, y in zip(probs, big, u_m, xy):
            yy = y[n2:] + _dot(jnp.where(incl[d], x[n2:, :n2], 0.0), u)
            y_refs[d][:, p * LANE:(p + 1) * LANE] = yy[:C] + yy[C:]
    for (d, p), x, a, u, v, k2, a2 in zip(probs, state, tot, u_m, vs_t, kb2, at2):
        st[d, p] = x * jnp.exp(a) + _dot(jnp.concatenate([u.T, v], axis=1), jnp.concatenate([a2, k2], axis=0))
    sfin_refs[0][...] = st[0]
    sfin_refs[1][...] = st[1]


def _rwkv_scan(r, lwf, lwb, k, v, kk, ka, nseq, seqlen, s0f, s0b, need_y):
    C = RWKV_CHUNK
    nc = seqlen // C
    tok_f = pl.BlockSpec((C, RWKV_DIM), lambda b, c: (b * nc + c, 0))
    tok_b = pl.BlockSpec((C, RWKV_DIM), lambda b, c: (b * nc + nc - 1 - c, 0))
    sshape = (RWKV_PAIRS, LANE, LANE)
    sspec = pl.BlockSpec((None,) + sshape, lambda b, c: (b, 0, 0, 0))
    sstruct = jax.ShapeDtypeStruct((nseq,) + sshape, F32)
    out_specs, out_shape = [sspec, sspec], [sstruct, sstruct]
    if need_y:
        ystruct = jax.ShapeDtypeStruct((nseq * seqlen, RWKV_DIM), F32)
        out_specs, out_shape = [tok_f, tok_b] + out_specs, [ystruct, ystruct] + out_shape
    res = pl.pallas_call(
        functools.partial(_rwkv_scan_kernel, C=C, need_y=need_y),
        grid=(nseq, nc),
        in_specs=[tok_f] * 6 + [tok_b] * 6 + [sspec, sspec],
        out_specs=out_specs, out_shape=out_shape,
        scratch_shapes=[pltpu.VMEM((2,) + sshape, F32)],
        compiler_params=_params("parallel", "arbitrary"),
    )(r, lwf, k, v, kk, ka, r, lwb, k, v, kk, ka, s0f, s0b)
    return tuple(res) if need_y else (None, None) + tuple(res)


def _rwkv_fin_kernel(yf_ref, yb_ref, bonus_ref, g_ref, lg_ref, lb_ref, bd_ref, o_ref):
    y = yf_ref[...] + yb_ref[...]
    bd = bd_ref[...]
    dv = y - _blocksum(y, bd) * (1.0 / RWKV_HEAD_DIM)
    var = _blocksum(dv * dv, bd) * (1.0 / RWKV_HEAD_DIM)
    yn = dv * lax.rsqrt(var + GN_EPS) * lg_ref[...] + lb_ref[...]
    o_ref[...] = ((yn + bonus_ref[...]) * g_ref[...]).astype(o_ref.dtype)


def _rwkv_finish(yf, yb, bonus, g, ln_g, ln_b, bd):
    nrows = yf.shape[0]
    tq = _tile(nrows, 512)
    tok = pl.BlockSpec((tq, RWKV_DIM), lambda i: (i, 0))
    return pl.pallas_call(
        _rwkv_fin_kernel,
        grid=(nrows // tq,),
        in_specs=[tok] * 4 + [_const_spec((1, RWKV_DIM))] * 2 + [_const_spec(bd.shape)],
        out_specs=tok,
        out_shape=jax.ShapeDtypeStruct((nrows, RWKV_DIM), BF16),
        compiler_params=_params("parallel"),
    )(yf, yb, bonus, g, ln_g, ln_b, bd)


def _to_cols(t, b, rows):
    return t.reshape(b, rows, GRID_W, -1).transpose(0, 2, 1, 3).reshape(b * rows * GRID_W, -1)


def _to_rows(t, b, rows):
    return t.reshape(b, GRID_W, rows, -1).transpose(0, 2, 1, 3).reshape(b * rows * GRID_W, -1)


def _pad_cols(w, n):
    return jnp.pad(w, [(0, 0)] * (w.ndim - 1) + [(0, n - w.shape[-1])])


def _pad_rows(w, n):
    return jnp.pad(w, [(0, 0)] * (w.ndim - 2) + [(0, n - w.shape[-2]), (0, 0)])


def kernel(x, c, ctx, c_ctx, mod_w, mod_b, norm_g, w_in, conv_w, conv_b, conv_ln_g, conv_ln_b, conv_out, ssd_conv_w, ssd_conv_b, ssd_A_log, ssd_dt_bias, ssd_D, ssd_norm_g, ssd_out, fourier_out, rwkv_mu, rwkv_w0, rwkv_w2, rwkv_a0, rwkv_a2, rwkv_g2, rwkv_k_k, rwkv_k_a, rwkv_r_k, rwkv_ln_g, rwkv_ln_b, rwkv_out, w_o, mlp_up, mlp_down):
    B, L, D = x.shape
    CL = ctx.shape[1]
    depth = mod_w.shape[0]
    rows = L // GRID_W
    n_lat, n_ctx = B * L, B * CL
    n_all = n_lat + n_ctx
    tm = _tile(math.gcd(L, n_ctx), 1024)
    tm_small = _tile(tm, 512)
    tm_merge = _tile(tm, 256)
    mod_idx = lambda t: (lambda i: jnp.minimum((i * t) // L, B))

    o_ssd, o_rwkv, o_conv, o_fft = SSD_IN, REC_IN, REC_IN + 2 * CONV_DIM, REC_IN + 2 * CONV_DIM + FOURIER_DIM
    w_in16 = lax.optimization_barrier(w_in.astype(BF16))
    rw = w_in16[:, :, o_ssd:o_rwkv]
    w_rw = jnp.concatenate(
        [rw[:, :, :3 * RWKV_DIM]]
        + [_pad_cols(rw[:, :, a:b], LANE) for a, b in ((1536, 1600), (1600, 1664), (1664, 1728))]
        + [rw[:, :, 1728:]], axis=2)
    zcols = lambda n: jnp.zeros((depth, D, n), BF16)
    w_u = jnp.concatenate(
        [w_in16[:, :, SSD_DIM:SSD_DIM + SSD_XBC], _pad_cols(w_in16[:, :, SSD_DIM + SSD_XBC:SSD_IN], U_DT_W),
         zcols(U_CONV - U_DT - U_DT_W), w_in16[:, :, o_rwkv:o_conv], w_in16[:, :, :SSD_DIM],
         zcols(U_FFT - U_Z - U_Z_W), w_in16[:, :, o_conv:o_fft]], axis=2)
    w_gate = w_in16[:, :, o_fft:]
    up_w, down_w, wo_w = mlp_up.astype(BF16), mlp_down.astype(BF16), w_o.astype(BF16)
    branch_w = [conv_out.astype(BF16), ssd_out.astype(BF16), fourier_out.astype(BF16), rwkv_out.astype(BF16)]
    mu = rwkv_mu
    mu_cols = jnp.concatenate(
        [mu[:, :3 * RWKV_DIM]] + [_pad_cols(mu[:, a:b], LANE) for a, b in ((1536, 1600), (1600, 1664), (1664, 1728))]
        + [mu[:, 1728:]], axis=1)
    head_of = jnp.arange(RWKV_DIM, dtype=jnp.int32) // RWKV_HEAD_DIM
    bd = (head_of[:, None] == head_of[None, :]).astype(BF16)
    gidx = jnp.arange(FOURIER_DIM, dtype=jnp.int32)
    cc, sc_ = _dft_mats(FOURIER_GROUP_DIM, FOURIER_GROUP_DIM ** -0.5)
    same_group = (gidx[:, None] // FOURIER_GROUP_DIM) == (gidx[None, :] // FOURIER_GROUP_DIM)
    tile_g = lambda m: jnp.where(same_group, jnp.tile(m, (FOURIER_GROUPS, FOURIER_GROUPS)), 0.0)
    chan_w = jnp.concatenate([tile_g(cc), -tile_g(sc_)], axis=1).astype(BF16)
    conv_w_p = _pad_rows(conv_w, 32)
    ssd_conv_w_p = _pad_rows(ssd_conv_w, SUBLANE)
    dt_bias_p = _pad_cols(ssd_dt_bias.reshape(depth, 1, 2 * SSD_HEADS), U_DT_W)
    a_log_p = _pad_cols(ssd_A_log.reshape(depth, 1, 2 * SSD_HEADS), U_DT_W)
    d_skip = jnp.repeat(ssd_D, SSD_HEAD_DIM, axis=1).reshape(depth, 1, SSD_DIM)

    cond = jnp.concatenate([c, c_ctx[None, :], jnp.zeros((SUBLANE - (B + 1) % SUBLANE, D), F32)], axis=0)
    mod = _mod_all(cond, mod_w, mod_b)

    xs = jnp.concatenate([x.reshape(n_lat, D), ctx.reshape(n_ctx, D)], axis=0)
    ssd_zero = jnp.zeros((B, SSD_GROUPS, SSD_STATE, SSD_HEADS_PER_GROUP * SSD_HEAD_DIM), F32)
    rwkv_zero = jnp.zeros((B, RWKV_PAIRS, LANE, LANE), F32)

    mods = lambda l: tuple(mod[l, :, j * D:(j + 1) * D].reshape(-1, 1, D) for j in range(6))
    h = _norm_mod(xs, norm_g[0, 0].reshape(1, D), *mods(0)[1::-1], n_all, mod_idx(tm), tm)
    for i in range(depth):
        last = i == depth - 1
        m_out = n_lat if last else n_all
        sh1, sc1, g1, sh2, sc2, g2 = mods(i)
        ng = norm_g[i].reshape(4, 1, D)

        u =_matmul(h, w_u[i], n_all, tm, F32, tn_cap=1536)
        gates = _matmul(h, w_gate[i], m_out, tm, BF16, epilogue="sigmoid", tn_cap=2048)

        a_conv = _conformer(u, 0, n_lat, GRID_W, conv_w_p[i], conv_b[i][None], conv_ln_g[i][None], conv_ln_b[i][None])
        z = _matmul(u, chan_w, m_out, tm, BF16, a_col=U_FFT // U_FFT_W)
        a_fft = _fourier(z, 0, B, L)
        if not last:
            a_conv = jnp.concatenate(
                [a_conv, _conformer(u, n_lat, n_ctx, CL, conv_w_p[i], conv_b[i][None], conv_ln_g[i][None],
                                    conv_ln_b[i][None])], axis=0)
            a_fft = jnp.concatenate([a_fft, _fourier(z, n_lat, B, CL)], axis=0)

        ssd_args = (dt_bias_p[i], a_log_p[i])
        xs_c, bm_c, cm_c = _ssd_conv(u, n_lat, B, CL, ssd_conv_w_p[i], ssd_conv_b[i][None])
        yfc, sf = _ssd_scan(xs_c, bm_c, cm_c, u, n_lat, B, CL, *ssd_args, ssd_zero, False, not last)
        ybc, sb = _ssd_scan(xs_c, bm_c, cm_c, u, n_lat, B, CL, *ssd_args, ssd_zero, True, not last)
        xs_l, bm_l, cm_l = _ssd_conv(u, 0, B, L, ssd_conv_w_p[i], ssd_conv_b[i][None])
        yfl, _ = _ssd_scan(xs_l, bm_l, cm_l, u, 0, B, L, *ssd_args, sf, False, True)
        ybl, _ = _ssd_scan(xs_l, bm_l, cm_l, u, 0, B, L, *ssd_args, sb, True, True)
        a_ssd = _ssd_finish(yfl, ybl, xs_l, u, 0, d_skip[i], ssd_norm_g[i][None])
        if not last:
            a_ssd = jnp.concatenate(
                [a_ssd, _ssd_finish(yfc, ybc, xs_c, u, n_lat, d_skip[i], ssd_norm_g[i][None])], axis=0)

        rp = dict(mu=mu_cols[i][None], w0=rwkv_w0[i], w2=_pad_rows(rwkv_w2[i], LANE).astype(BF16),
                  a0=rwkv_a0[i][None], a2=_pad_rows(rwkv_a2[i], LANE).astype(BF16), g2=rwkv_g2[i].astype(BF16),
                  k_k=rwkv_k_k[i][None], k_a=rwkv_k_a[i][None], r_k=rwkv_r_k[i].reshape(1, RWKV_DIM), bd=bd)
        u_rc = _matmul(h, w_rw[i], n_ctx, tm, F32, a_row=n_lat // tm)
        r_, lwf, lwb, k_, v_, kk_, ka_, g_, bonus = _rwkv_prep(u_rc, 0, B, CL, rp)
        yfc, ybc, s_f, s_b = _rwkv_scan(r_, lwf, lwb, k_, v_, kk_, ka_, B, CL, rwkv_zero, rwkv_zero, not last)
        if not last:
            a_rwkv_c = _rwkv_finish(yfc, ybc, bonus, g_, rwkv_ln_g[i][None], rwkv_ln_b[i][None], bd)
        u_rl = _matmul(_to_cols(h[:n_lat], B, rows), w_rw[i], n_lat, tm, F32)
        r_, lwf, lwb, k_, v_, kk_, ka_, g_, bonus = _rwkv_prep(u_rl, 0, B, L, rp)
        yfl, ybl, _, _ = _rwkv_scan(r_, lwf, lwb, k_, v_, kk_, ka_, B, L, s_f, s_b, True)
        a_rwkv = _to_rows(_rwkv_finish(yfl, ybl, bonus, g_, rwkv_ln_g[i][None], rwkv_ln_b[i][None], bd), B, rows)
        if not last:
            a_rwkv = jnp.concatenate([a_rwkv, a_rwkv_c], axis=0)

        m = _merge(gates, [a_conv, a_ssd, a_fft, a_rwkv], [w[i] for w in branch_w], m_out, tm_merge)
        xs, h2 = _matmul_residual(m, wo_w[i], xs, g1, ng[1], m_out, mod_idx(tm_small), tm_small,
                                  nxt=(ng[2], sc2, sh2))

        mid = _matmul(h2, up_w[i], m_out, tm, BF16, epilogue="relu2", tn_cap=2048)
        if last:
            xs = _matmul_residual(mid, down_w[i], xs, g2, ng[3], m_out, mod_idx(tm_small), tm_small)
        else:
            xs, h = _matmul_residual(mid, down_w[i], xs, g2, ng[3], m_out, mod_idx(tm_small), tm_small,
                                     nxt=(norm_g[i + 1, 0].reshape(1, D), *mods(i + 1)[1::-1]))

    return xs[:n_lat].reshape(B, L, D)
```

```python
import functools
import math

import jax
import jax.numpy as jnp
from jax import lax
from jax.experimental import pallas as pl
from jax.experimental.pallas import tpu as pltpu

F32 = jnp.float32
BF16 = jnp.bfloat16

GRID_W = 64
RMS_EPS = 1e-6
LN_EPS = 1e-5
GN_EPS = 64e-5
CONV_DIM = 512
CONV_WIDTH = 31
SSD_HEADS = 12
SSD_HEAD_DIM = 64
SSD_DIM = SSD_HEADS * SSD_HEAD_DIM
SSD_GROUPS = 4
SSD_HEADS_PER_GROUP = SSD_HEADS // SSD_GROUPS
SSD_STATE = 128
SSD_CONV = 5
SSD_BC = SSD_GROUPS * SSD_STATE
SSD_XBC = SSD_DIM + 2 * SSD_BC
SSD_IN = SSD_DIM + SSD_XBC + 2 * SSD_HEADS
SSD_CHUNK = 128
FOURIER_GROUPS = 4
FOURIER_GROUP_DIM = 128
FOURIER_DIM = FOURIER_GROUPS * FOURIER_GROUP_DIM
RWKV_HEADS = 8
RWKV_HEAD_DIM = 64
RWKV_DIM = RWKV_HEADS * RWKV_HEAD_DIM
RWKV_PAIRS = RWKV_HEADS // 2
DECAY_LORA = 64
ICL_LORA = 64
GATE_LORA = 128
RWKV_IN = 3 * RWKV_DIM + 2 * DECAY_LORA + ICL_LORA + GATE_LORA
RWKV_CHUNK = 64
REC_IN = SSD_IN + RWKV_IN
N_BRANCH = 4

LANE = 128
SUBLANE = 8
VMEM_LIMIT = 56 * 1024 * 1024

U_XBC, U_XBC_W = 0, SSD_XBC
U_DT, U_DT_W = 1792, LANE
U_CONV, U_CONV_W = 2048, 2 * CONV_DIM
U_Z, U_Z_W = 3072, SSD_DIM
U_FFT, U_FFT_W = 4096, FOURIER_DIM
U_N = 4608
U_RWKV_W = 2048
RW_WF, RW_WB, RW_AL, RW_GL = 1536, 1664, 1792, 1920


def _params(*sem):
    return pltpu.CompilerParams(dimension_semantics=sem, vmem_limit_bytes=VMEM_LIMIT)


def _tile(n, cap, quantum=SUBLANE):
    if n <= cap:
        return n
    t = (cap // quantum) * quantum
    while t >= quantum:
        if n % t == 0:
            return t
        t -= quantum
    raise ValueError(f"no tile for {n} under {cap}")


def _const_spec(shape):
    nd = len(shape)
    return pl.BlockSpec(shape, lambda *_: (0,) * nd)


def _dot(a, b):
    return jnp.dot(a.astype(BF16), b.astype(BF16), preferred_element_type=F32)


def _dot_nt(a, b):
    return lax.dot_general(a.astype(BF16), b.astype(BF16), (((1,), (1,)), ((), ())),
                           preferred_element_type=F32)


def _split3(x):
    h1 = x.astype(BF16)
    r1 = x - h1.astype(F32)
    h2 = r1.astype(BF16)
    h3 = (r1 - h2.astype(F32)).astype(BF16)
    return h1, h2, h3


def _dot_exact_lhs(m, x):
    h1, h2, h3 = _split3(x)
    d = lambda h: jnp.dot(m, h, preferred_element_type=F32)
    return d(h1) + d(h2) + d(h3)


def _blocksum(x, bd):
    hi = x.astype(BF16)
    lo = (x - hi.astype(F32)).astype(BF16)
    return (jnp.dot(hi, bd, preferred_element_type=F32) + jnp.dot(lo, bd, preferred_element_type=F32))


def _sigmoid(x):
    return 1.0 / (1.0 + jnp.exp(-x))


def _softplus(x):
    return jnp.maximum(x, 0.0) + jnp.log1p(jnp.exp(-jnp.abs(x)))


def _rms(y, g):
    return y * lax.rsqrt(jnp.mean(y * y, axis=-1, keepdims=True) + RMS_EPS) * g


_O_RWKV, _O_CONV = SSD_IN, REC_IN
_O_FFT, _O_GATE = REC_IN + 2 * CONV_DIM, REC_IN + 2 * CONV_DIM + FOURIER_DIM
_W_IN_PIECES = (
    ("rw", 0, _O_RWKV, 3 * RWKV_DIM), ("rw", RW_WF, _O_RWKV + 1536, DECAY_LORA),
    ("rw", RW_WB, _O_RWKV + 1600, DECAY_LORA), ("rw", RW_AL, _O_RWKV + 1664, ICL_LORA),
    ("rw", RW_GL, _O_RWKV + 1728, GATE_LORA),
    ("u", U_XBC, SSD_DIM, SSD_XBC), ("u", U_DT, SSD_DIM + SSD_XBC, 2 * SSD_HEADS),
    ("u", U_CONV, _O_CONV, 2 * CONV_DIM), ("u", U_Z, 0, SSD_DIM), ("u", U_FFT, _O_FFT, FOURIER_DIM),
)
_RELAYOUT_CHUNK = 1024


def _split_w_in_kernel(w_ref, u_ref, rw_ref, g_ref):
    n_in = w_ref.shape[-1]
    dst = {"u": u_ref, "rw": rw_ref, "g": g_ref}
    u_ref[...] = jnp.zeros_like(u_ref)
    rw_ref[...] = jnp.zeros_like(rw_ref)
    pieces = _W_IN_PIECES + (("g", 0, _O_GATE, g_ref.shape[-1]),)
    for name, dcol, scol, width in pieces:
        for off in range(0, width, _RELAYOUT_CHUNK):
            w = min(_RELAYOUT_CHUNK, width - off)
            lo = scol + off
            lo_al = (lo // LANE) * LANE
            hi_al = min(-(-(lo + w) // LANE) * LANE, n_in)
            x = w_ref[:, lo_al:hi_al]
            dst[name][:, dcol + off:dcol + off + w] = x[:, lo - lo_al:lo - lo_al + w].astype(BF16)


def _split_w_in(w_in):
    depth, d, n_in = w_in.shape
    n_gate = n_in - _O_GATE
    tk = _tile(d, 128)
    outs = (U_N, U_RWKV_W, n_gate)
    return pl.pallas_call(
        _split_w_in_kernel,
        grid=(depth, d // tk),
        in_specs=[pl.BlockSpec((None, tk, n_in), lambda l, i: (l, i, 0))],
        out_specs=[pl.BlockSpec((None, tk, n), lambda l, i: (l, i, 0)) for n in outs],
        out_shape=[jax.ShapeDtypeStruct((depth, d, n), BF16) for n in outs],
        compiler_params=_params("parallel", "parallel"),
    )(w_in)


def _mod_kernel(a_ref, w_ref, b_ref, o_ref):
    a = a_ref[...]
    a = a * _sigmoid(a)
    o_ref[...] = _dot(a, w_ref[...]) + b_ref[...]


def _mod_all(cond, mod_w, mod_b):
    depth, d, n = mod_w.shape
    tn = _tile(n, 1024, LANE)
    return pl.pallas_call(
        _mod_kernel,
        grid=(depth, n // tn),
        in_specs=[_const_spec(cond.shape),
                  pl.BlockSpec((None, d, tn), lambda l, j: (l, 0, j)),
                  pl.BlockSpec((None, 1, tn), lambda l, j: (l, 0, j))],
        out_specs=pl.BlockSpec((None, cond.shape[0], tn), lambda l, j: (l, 0, j)),
        out_shape=jax.ShapeDtypeStruct((depth, cond.shape[0], n), F32),
        compiler_params=_params("parallel", "parallel"),
    )(cond, mod_w, mod_b.reshape(depth, 1, n))


def _norm_mod_kernel(x_ref, g_ref, sc_ref, sh_ref, o_ref):
    o_ref[...] = (_rms(x_ref[...], g_ref[...]) * (1.0 + sc_ref[...]) + sh_ref[...]).astype(o_ref.dtype)


def _norm_mod(x, g, sc, sh, m_rows, mod_idx, tm):
    d = x.shape[1]
    return pl.pallas_call(
        _norm_mod_kernel,
        grid=(m_rows // tm,),
        in_specs=[pl.BlockSpec((tm, d), lambda i: (i, 0)),
                  _const_spec((1, d)),
                  pl.BlockSpec((None, 1, d), lambda i: (mod_idx(i), 0, 0)),
                  pl.BlockSpec((None, 1, d), lambda i: (mod_idx(i), 0, 0))],
        out_specs=pl.BlockSpec((tm, d), lambda i: (i, 0)),
        out_shape=jax.ShapeDtypeStruct((m_rows, d), BF16),
        compiler_params=_params("parallel"),
    )(x, g, sc, sh)


def _mm_kernel(a_ref, w_ref, o_ref, *, epilogue):
    acc = _dot(a_ref[...], w_ref[...])
    if epilogue == "relu2":
        acc = jnp.square(jnp.maximum(acc, 0.0))
    elif epilogue == "sigmoid":
        acc = _sigmoid(acc)
    o_ref[...] = acc.astype(o_ref.dtype)


def _layer_spec(w, layer, block, index):
    if w.ndim == 2:
        return pl.BlockSpec(block, index)
    return pl.BlockSpec((None,) + block, lambda *ids: (layer,) + index(*ids))


def _matmul(a, w, m_rows, tm, out_dtype, epilogue=None, a_col=0, a_row=0, tn_cap=1024, layer=0):
    k, n = w.shape[-2:]
    tn = _tile(n, tn_cap, LANE)
    return pl.pallas_call(
        functools.partial(_mm_kernel, epilogue=epilogue),
        grid=(m_rows // tm, n // tn),
        in_specs=[pl.BlockSpec((tm, k), lambda i, j: (a_row + i, a_col)),
                  _layer_spec(w, layer, (k, tn), lambda i, j: (0, j))],
        out_specs=pl.BlockSpec((tm, tn), lambda i, j: (i, j)),
        out_shape=jax.ShapeDtypeStruct((m_rows, n), out_dtype),
        compiler_params=_params("parallel", "arbitrary"),
    )(a, w)


def _mm_wcast_kernel(a_ref, w_ref, o_ref, wb_ref, *, epilogue):
    @pl.when(pl.program_id(1) == 0)
    def _():
        wb_ref[...] = w_ref[...].astype(wb_ref.dtype)

    _mm_kernel(a_ref, wb_ref, o_ref, epilogue=epilogue)


def _matmul_f32w(a, w, m_rows, tm, out_dtype, epilogue=None, tn_cap=1024, layer=0):
    k, n = w.shape[-2:]
    tn = _tile(n, tn_cap, LANE)
    return pl.pallas_call(
        functools.partial(_mm_wcast_kernel, epilogue=epilogue),
        grid=(n // tn, m_rows // tm),
        in_specs=[pl.BlockSpec((tm, k), lambda j, i: (i, 0)),
                  _layer_spec(w, layer, (k, tn), lambda j, i: (0, j))],
        out_specs=pl.BlockSpec((tm, tn), lambda j, i: (i, j)),
        out_shape=jax.ShapeDtypeStruct((m_rows, n), out_dtype),
        scratch_shapes=[pltpu.VMEM((k, tn), BF16)],
        compiler_params=_params("parallel", "arbitrary"),
    )(a, w)


def _mm_res_kernel(a_ref, w_ref, x_ref, gate_ref, g_ref, *rest, with_next, nk):
    if with_next:
        g2_ref, sc_ref, sh_ref, o_ref, h_ref = rest
    else:
        (o_ref,) = rest
    def finish(y):
        xn = x_ref[...] + gate_ref[...] * _rms(y, g_ref[...])
        o_ref[...] = xn
        if with_next:
            h_ref[...] = (_rms(xn, g2_ref[...]) * (1.0 + sc_ref[...]) + sh_ref[...]).astype(h_ref.dtype)

    if nk == 1:
        finish(_dot(a_ref[...], w_ref[...]))
        return
    k = pl.program_id(1)

    @pl.when(k == 0)
    def _():
        o_ref[...] = jnp.zeros_like(o_ref)

    o_ref[...] += _dot(a_ref[...], w_ref[...])

    @pl.when(k == nk - 1)
    def _():
        finish(o_ref[...])


def _matmul_residual(a, w, x, gate, g, m_rows, mod_idx, tm, nxt=None, tk_cap=2048, layer=0):
    k, d = w.shape[-2:]
    tk = _tile(k, tk_cap, LANE)
    mod_spec = pl.BlockSpec((None, 1, d), lambda i, kk: (mod_idx(i), 0, 0))
    tile_spec = pl.BlockSpec((tm, d), lambda i, kk: (i, 0))
    in_specs = [pl.BlockSpec((tm, tk), lambda i, kk: (i, kk)), _layer_spec(w, layer, (tk, d), lambda i, kk: (kk, 0)),
                tile_spec, mod_spec, _const_spec((1, d))]
    out_specs, out_shape, args = tile_spec, jax.ShapeDtypeStruct((m_rows, d), F32), (a, w, x, gate, g)
    if nxt is not None:
        in_specs += [_const_spec((1, d)), mod_spec, mod_spec]
        out_specs, out_shape = [tile_spec, tile_spec], [out_shape, jax.ShapeDtypeStruct((m_rows, d), BF16)]
        args += tuple(nxt)
    return pl.pallas_call(
        functools.partial(_mm_res_kernel, with_next=nxt is not None, nk=k // tk),
        grid=(m_rows // tm, k // tk),
        in_specs=in_specs, out_specs=out_specs, out_shape=out_shape,
        compiler_params=_params("parallel", "arbitrary"),
    )(*args)


def _merge_kernel(gates_ref, a0, a1, a2, a3, w0, w1, w2, w3, o_ref):
    d = o_ref.shape[-1]
    m = None
    for i, (a, w) in enumerate(((a0, w0), (a1, w1), (a2, w2), (a3, w3))):
        t = gates_ref[:, i * d:(i + 1) * d].astype(F32) * _dot(a[...], w[...])
        m = t if m is None else m + t
    o_ref[...] = m.astype(o_ref.dtype)


def _merge(gates, acts, weights, m_rows, tm, layer):
    d = weights[0].shape[-1]
    return pl.pallas_call(
        _merge_kernel,
        grid=(m_rows // tm,),
        in_specs=[pl.BlockSpec((tm, N_BRANCH * d), lambda i: (i, 0))]
        + [pl.BlockSpec((tm, a.shape[1]), lambda i: (i, 0)) for a in acts]
        + [_layer_spec(w, layer, w.shape[-2:], lambda i: (0, 0)) for w in weights],
        out_specs=pl.BlockSpec((tm, d), lambda i: (i, 0)),
        out_shape=jax.ShapeDtypeStruct((m_rows, d), BF16),
        compiler_params=_params("parallel"),
    )(gates, *acts, *weights)


def _conformer_kernel(u_ref, w_ref, b_ref, lg_ref, lb_ref, o_ref, pad_ref, *, seg, nseg):
    pad = 16
    zeros = jnp.zeros((pad, CONV_DIM), F32)
    pad_ref[0:pad, :] = zeros
    pad_ref[pad + seg:2 * pad + seg, :] = zeros

    def body(s, carry):
        r0 = pl.multiple_of(s * seg, seg)
        u = u_ref[pl.ds(r0, seg), :]
        pad_ref[pad:pad + seg, :] = u[:, :CONV_DIM] * _sigmoid(u[:, CONV_DIM:])
        acc = jnp.zeros((seg, CONV_DIM), F32) + b_ref[...]
        for j in range(CONV_WIDTH):
            acc = acc + w_ref[j:j + 1, :] * pad_ref[pl.ds(pad - (CONV_WIDTH - 1) // 2 + j, seg), :]
        mu = jnp.mean(acc, axis=-1, keepdims=True)
        dv = acc - mu
        var = jnp.mean(dv * dv, axis=-1, keepdims=True)
        v = dv * lax.rsqrt(var + LN_EPS) * lg_ref[...] + lb_ref[...]
        o_ref[pl.ds(r0, seg), :] = (v * _sigmoid(v)).astype(o_ref.dtype)
        return carry

    lax.fori_loop(0, nseg, body, 0)


def _conformer(u, row0, nrows, seg, conv_w, conv_b, ln_g, ln_b):
    tq = max(seg, min(256, nrows))
    blk0 = row0 // tq
    return pl.pallas_call(
        functools.partial(_conformer_kernel, seg=seg, nseg=tq // seg),
        grid=(nrows // tq,),
        in_specs=[pl.BlockSpec((tq, U_CONV_W), lambda i: (blk0 + i, U_CONV // U_CONV_W)),
                  _const_spec(conv_w.shape), _const_spec((1, CONV_DIM)),
                  _const_spec((1, CONV_DIM)), _const_spec((1, CONV_DIM))],
        out_specs=pl.BlockSpec((tq, CONV_DIM), lambda i: (i, 0)),
        out_shape=jax.ShapeDtypeStruct((nrows, CONV_DIM), BF16),
        scratch_shapes=[pltpu.VMEM((seg + 32, CONV_DIM), F32)],
        compiler_params=_params("parallel"),
    )(u, conv_w, conv_b, ln_g, ln_b)


def _dft_kernel(c_ref, s_ref, z_ref, o_ref):
    o_ref[...] = (jnp.dot(c_ref[...], z_ref[:, :FOURIER_DIM], preferred_element_type=F32)
                  + jnp.dot(s_ref[...], z_ref[:, FOURIER_DIM:], preferred_element_type=F32)).astype(o_ref.dtype)


def _dft_mats(n, scale):
    idx = jnp.arange(n, dtype=jnp.int32)
    ang = ((idx[:, None] * idx[None, :]) % n).astype(F32) * (2.0 * math.pi / n)
    return jnp.cos(ang) * scale, jnp.sin(ang) * scale


def _fourier(z, row0, nseq, seqlen):
    nrows = nseq * seqlen
    cl, sl = _dft_mats(seqlen, seqlen ** -0.5)
    cl, sl = cl.astype(BF16), sl.astype(BF16)
    tm = _tile(seqlen, 512)
    nt = seqlen // tm
    sblk0 = row0 // seqlen
    return pl.pallas_call(
        _dft_kernel,
        grid=(nt, nseq),
        in_specs=[pl.BlockSpec((tm, seqlen), lambda i, b: (i, 0)),
                  pl.BlockSpec((tm, seqlen), lambda i, b: (i, 0)),
                  pl.BlockSpec((seqlen, 2 * FOURIER_DIM), lambda i, b: (sblk0 + b, 0))],
        out_specs=pl.BlockSpec((tm, FOURIER_DIM), lambda i, b: (b * nt + i, 0)),
        out_shape=jax.ShapeDtypeStruct((nrows, FOURIER_DIM), BF16),
        compiler_params=_params("parallel", "arbitrary"),
    )(cl, sl, z)


def _halo_specs(tq, width, blk0, col_blk, nblk_total):
    hb = tq // SUBLANE
    cur = pl.BlockSpec((tq, width), lambda i: (blk0 + i, col_blk))
    prev = pl.BlockSpec((SUBLANE, width), lambda i: (jnp.maximum((blk0 + i) * hb - 1, 0), col_blk))
    nxt = pl.BlockSpec((SUBLANE, width),
                       lambda i: (jnp.minimum((blk0 + i + 1) * hb, nblk_total - 1), col_blk))
    return cur, prev, nxt


def _fill_halo(scr, cur_ref, prev_ref, next_ref, tq, tiles_per_seq):
    i = pl.program_id(0)
    first = (i % tiles_per_seq) == 0
    last = (i % tiles_per_seq) == tiles_per_seq - 1
    scr[0:SUBLANE, :] = jnp.where(first, 0.0, prev_ref[...])
    scr[SUBLANE:SUBLANE + tq, :] = cur_ref[...]
    scr[SUBLANE + tq:2 * SUBLANE + tq, :] = jnp.where(last, 0.0, next_ref[...])


def _ssd_conv_kernel(cur_ref, prev_ref, next_ref, w_ref, b_ref, xs_ref, bm_ref, cm_ref, scr, *, tq, tiles_per_seq):
    _fill_halo(scr, cur_ref, prev_ref, next_ref, tq, tiles_per_seq)
    half = (SSD_CONV - 1) // 2
    cw = 256
    for c0 in range(0, SSD_XBC, cw):
        acc = jnp.zeros((tq, cw), F32) + b_ref[:, c0:c0 + cw]
        for j in range(SSD_CONV):
            acc = acc + w_ref[j:j + 1, c0:c0 + cw] * scr[pl.ds(SUBLANE - half + j, tq), c0:c0 + cw]
        y = acc * _sigmoid(acc)
        if c0 < SSD_DIM:
            xs_ref[:, c0:c0 + cw] = y
        elif c0 < SSD_DIM + SSD_BC:
            bm_ref[:, c0 - SSD_DIM:c0 - SSD_DIM + cw] = y
        else:
            cm_ref[:, c0 - SSD_DIM - SSD_BC:c0 - SSD_DIM - SSD_BC + cw] = y


def _ssd_conv(u, row0, nseq, seqlen, conv_w, conv_b):
    nrows = nseq * seqlen
    tq = _tile(seqlen, 256)
    cur, prev, nxt = _halo_specs(tq, U_XBC_W, row0 // tq, U_XBC // U_XBC_W, u.shape[0] // SUBLANE)
    outs = [jax.ShapeDtypeStruct((nrows, w), F32) for w in (SSD_DIM, SSD_BC, SSD_BC)]
    return pl.pallas_call(
        functools.partial(_ssd_conv_kernel, tq=tq, tiles_per_seq=seqlen // tq),
        grid=(nrows // tq,),
        in_specs=[cur, prev, nxt, _const_spec(conv_w.shape), _const_spec((1, SSD_XBC))],
        out_specs=[pl.BlockSpec((tq, w), lambda i: (i, 0)) for w in (SSD_DIM, SSD_BC, SSD_BC)],
        out_shape=outs,
        scratch_shapes=[pltpu.VMEM((tq + 2 * SUBLANE, U_XBC_W), F32)],
        compiler_params=_params("parallel"),
    )(u, u, u, conv_w, conv_b)


def _ssd_scan_kernel(xs_ref, b_ref, c_ref, dt_ref, bias_ref, alog_ref, sel_ref, s0_ref, *rest,
                     q, rev, direction, need_y):
    if need_y:
        y_ref, sfin_ref, st = rest
    else:
        sfin_ref, st = rest

    @pl.when(pl.program_id(1) == 0)
    def _():
        st[...] = s0_ref[...]

    hp, gw = SSD_HEAD_DIM, SSD_HEADS_PER_GROUP * SSD_HEAD_DIM
    dt = _softplus(dt_ref[...] + bias_ref[...])
    a = dt * (-jnp.exp(alog_ref[...]))
    ti = lax.broadcasted_iota(jnp.int32, (q, q), 0)
    tk = lax.broadcasted_iota(jnp.int32, (q, q), 1)
    before = (tk >= ti) if rev else (tk <= ti)
    cs = _dot_exact_lhs(jnp.where(before, 1.0, 0.0).astype(BF16), a)
    sel = sel_ref[...]
    spread = lambda parts: sum(jnp.dot(p, sel, preferred_element_type=F32) for p in parts)
    dt_b = spread(_split3(dt)[:2])
    cs_b = spread(_split3(cs))
    tot_b = cs_b[0:1, :] if rev else cs_b[q - 1:q, :]
    xd = xs_ref[...] * dt_b
    xdd = xd * jnp.exp(tot_b - cs_b)
    dec = jnp.exp(tot_b)
    if need_y:
        ecs = jnp.exp(cs_b)
        cs_t = cs.T
        lane0 = lax.broadcasted_iota(jnp.int32, (q, LANE), 1) < hp
    for g in range(SSD_GROUPS):
        cg = c_ref[:, g * SSD_STATE:(g + 1) * SSD_STATE]
        bg = b_ref[:, g * SSD_STATE:(g + 1) * SSD_STATE]
        state = st[g]
        if need_y:
            cb = _dot_nt(cg, bg)
            y_off = _dot(cg, state) * ecs[:, g * gw:(g + 1) * gw]
            for j in range(SSD_HEADS_PER_GROUP):
                h = g * SSD_HEADS_PER_GROUP + j
                col = direction * SSD_HEADS + h
                pair = cs_b[:, (h // 2) * LANE:(h // 2 + 1) * LANE]
                swapped = pltpu.roll(pair, hp, axis=1)
                cs_col = jnp.where(lane0, pair, swapped) if h % 2 == 0 else jnp.where(lane0, swapped, pair)
                lm = jnp.where(before, jnp.exp(jnp.minimum(cs_col - cs_t[col:col + 1, :], 0.0)), 0.0)
                y_ref[:, h * hp:(h + 1) * hp] = (_dot(cb * lm, xd[:, h * hp:(h + 1) * hp])
                                                 + y_off[:, j * hp:(j + 1) * hp])
        st[g] = state * dec[:, g * gw:(g + 1) * gw] + _dot(bg.T, xdd[:, g * gw:(g + 1) * gw])
    sfin_ref[...] = st[...]


def _ssd_scan(xs, bm, cm, u, row0, nseq, seqlen, dt_bias, a_log, s0, rev, need_y):
    q = SSD_CHUNK
    nc = seqlen // q
    ublk0 = row0 // q
    pos = (lambda c: nc - 1 - c) if rev else (lambda c: c)
    row = lambda b, c: (b * nc + pos(c), 0)
    sshape = (SSD_GROUPS, SSD_STATE, SSD_HEADS_PER_GROUP * SSD_HEAD_DIM)
    sspec = pl.BlockSpec((None,) + sshape, lambda b, c: (b, 0, 0, 0))
    head_col = int(rev) * SSD_HEADS + jnp.arange(SSD_DIM, dtype=jnp.int32) // SSD_HEAD_DIM
    sel = (jnp.arange(U_DT_W, dtype=jnp.int32)[:, None] == head_col[None, :]).astype(BF16)
    out_specs, out_shape = [sspec], [jax.ShapeDtypeStruct((nseq,) + sshape, F32)]
    if need_y:
        out_specs.insert(0, pl.BlockSpec((q, SSD_DIM), row))
        out_shape.insert(0, jax.ShapeDtypeStruct((nseq * seqlen, SSD_DIM), F32))
    res = pl.pallas_call(
        functools.partial(_ssd_scan_kernel, q=q, rev=rev, direction=int(rev), need_y=need_y),
        grid=(nseq, nc),
        in_specs=[pl.BlockSpec((q, SSD_DIM), row), pl.BlockSpec((q, SSD_BC), row), pl.BlockSpec((q, SSD_BC), row),
                  pl.BlockSpec((q, U_DT_W), lambda b, c: (ublk0 + b * nc + pos(c), U_DT // U_DT_W)),
                  _const_spec((1, U_DT_W)), _const_spec((1, U_DT_W)), _const_spec(sel.shape), sspec],
        out_specs=out_specs, out_shape=out_shape,
        scratch_shapes=[pltpu.VMEM(sshape, F32)],
        compiler_params=_params("parallel", "arbitrary"),
    )(xs, bm, cm, u, dt_bias, a_log, sel, s0)
    return res if need_y else (None, res[0])


def _ssd_fin_kernel(yf_ref, yb_ref, xs_ref, z_ref, d_ref, g_ref, o_ref):
    z = z_ref[...]
    y = (yf_ref[...] + yb_ref[...] + d_ref[...] * xs_ref[...]) * (z * _sigmoid(z))
    o_ref[...] = _rms(y, g_ref[...]).astype(o_ref.dtype)


def _ssd_finish(yf, yb, xs, u, row0, d_skip, norm_g):
    nrows = yf.shape[0]
    tq = _tile(nrows, 512)
    blk0 = row0 // tq
    tok = pl.BlockSpec((tq, SSD_DIM), lambda i: (i, 0))
    return pl.pallas_call(
        _ssd_fin_kernel,
        grid=(nrows // tq,),
        in_specs=[tok, tok, tok, pl.BlockSpec((tq, U_Z_W), lambda i: (blk0 + i, U_Z // U_Z_W)),
                  _const_spec((1, SSD_DIM)), _const_spec((1, SSD_DIM))],
        out_specs=tok,
        out_shape=jax.ShapeDtypeStruct((nrows, SSD_DIM), BF16),
        compiler_params=_params("parallel"),
    )(yf, yb, xs, u, d_skip, norm_g)


def _rwkv_prep_kernel(cur_ref, prev_ref, next_ref, mu_ref, w0_ref, w2_ref, a0_ref, a2_ref, g2_ref,
                      kk_ref, ka_ref, rk_ref, bd_ref,
                      r_o, lwf_o, lwb_o, k_o, v_o, kk_o, ka_o, g_o, bonus_o, scr, *, tq, tiles_per_seq):
    _fill_halo(scr, cur_ref, prev_ref, next_ref, tq, tiles_per_seq)
    u = cur_ref[...]
    shift = 0.5 * (scr[pl.ds(SUBLANE - 1, tq), :] + scr[pl.ds(SUBLANE + 1, tq), :]) - u
    x = u + shift * mu_ref[...]
    r = x[:, 0:RWKV_DIM]
    k = x[:, RWKV_DIM:2 * RWKV_DIM]
    v = x[:, 2 * RWKV_DIM:3 * RWKV_DIM]
    bd = bd_ref[...]
    for d, (off, out) in enumerate(((RW_WF, lwf_o), (RW_WB, lwb_o))):
        lo = jnp.tanh(x[:, off:off + LANE])
        wl = -_softplus(-(w0_ref[d:d + 1, :] + _dot(lo, w2_ref[d]))) - 0.5
        out[...] = -jnp.exp(wl)
    a = _sigmoid(a0_ref[...] + _dot(x[:, RW_AL:RW_AL + LANE], a2_ref[...]))
    kk = k * kk_ref[...]
    kk = kk * lax.rsqrt(_blocksum(kk * kk, bd) + 1e-12)
    kmod = k * (1.0 + (a - 1.0) * ka_ref[...])
    r_o[...] = r
    k_o[...] = kmod
    v_o[...] = v
    kk_o[...] = kk
    ka_o[...] = kk * a
    g_o[...] = _dot(_sigmoid(x[:, RW_GL:RW_GL + GATE_LORA]), g2_ref[...])
    bonus_o[...] = _blocksum(r * kmod * rk_ref[...], bd) * v


def _rwkv_prep(u, row0, nseq, seqlen, p):
    nrows = nseq * seqlen
    tq = _tile(seqlen, 256)
    cur, prev, nxt = _halo_specs(tq, U_RWKV_W, row0 // tq, 0, u.shape[0] // SUBLANE)
    consts = [p["mu"], p["w0"], p["w2"], p["a0"], p["a2"], p["g2"], p["k_k"], p["k_a"], p["r_k"], p["bd"]]
    tok = pl.BlockSpec((tq, RWKV_DIM), lambda i: (i, 0))
    return pl.pallas_call(
        functools.partial(_rwkv_prep_kernel, tq=tq, tiles_per_seq=seqlen // tq),
        grid=(nrows // tq,),
        in_specs=[cur, prev, nxt] + [_const_spec(c.shape) for c in consts],
        out_specs=[tok] * 9,
        out_shape=[jax.ShapeDtypeStruct((nrows, RWKV_DIM), F32)] * 9,
        scratch_shapes=[pltpu.VMEM((tq + 2 * SUBLANE, U_RWKV_W), F32)],
        compiler_params=_params("parallel"),
    )(u, u, u, *consts)


def _rwkv_scan_kernel(*refs, C, need_y):
    ins = (refs[0:6], refs[6:12])
    s0_refs = refs[12:14]
    if need_y:
        y_refs, sfin_refs, st = refs[14:16], refs[16:18], refs[18]
    else:
        sfin_refs, st = refs[14:16], refs[16]

    @pl.when(pl.program_id(1) == 0)
    def _():
        st[0] = s0_refs[0][...]
        st[1] = s0_refs[1][...]

    n2 = 2 * C
    row = lax.broadcasted_iota(jnp.int32, (n2, n2), 0)
    col = lax.broadcasted_iota(jnp.int32, (n2, n2), 1)
    t, s = row & (C - 1), col & (C - 1)
    strict = ((s < t), (s > t))
    incl = ((s <= t), (s >= t))
    eye = jnp.where(row == col, 1.0, 0.0)
    ti = lax.broadcasted_iota(jnp.int32, (C, C), 0)
    tk = lax.broadcasted_iota(jnp.int32, (C, C), 1)
    tri = (jnp.where(tk <= ti, 1.0, 0.0).astype(BF16), jnp.where(tk >= ti, 1.0, 0.0).astype(BF16))
    head0 = lax.broadcasted_iota(jnp.int32, (C, LANE), 1) < RWKV_HEAD_DIM

    def stack(x):
        return jnp.concatenate([jnp.where(head0, x, 0.0), jnp.where(head0, 0.0, x)], axis=0)

    probs = [(d, p) for d in (0, 1) for p in range(RWKV_PAIRS)]
    ld = lambda d, p, i: ins[d][i][:, p * LANE:(p + 1) * LANE]
    lw = [ld(d, p, 1) for d, p in probs]
    cs = [_dot_exact_lhs(tri[d], x) for (d, p), x in zip(probs, lw)]
    tot = [x[C - 1:C, :] if d == 0 else x[0:1, :] for (d, p), x in zip(probs, cs)]
    ems = [jnp.exp(-x) for x in cs]
    edec = [jnp.exp(a - x) for a, x in zip(tot, cs)]
    kt = [stack(ld(d, p, 4) * jnp.exp(x - w)) for (d, p), x, w in zip(probs, cs, lw)]
    rt = [stack(ld(d, p, 0) * jnp.exp(x)) for (d, p), x in zip(probs, cs)]
    at = [stack(ld(d, p, 5) * e) for (d, p), e in zip(probs, ems)]
    kb = [stack(ld(d, p, 2) * e) for (d, p), e in zip(probs, ems)]
    at2 = [stack(ld(d, p, 5) * e) for (d, p), e in zip(probs, edec)]
    kb2 = [stack(ld(d, p, 2) * e) for (d, p), e in zip(probs, edec)]
    vs_t = [stack(ld(d, p, 3)).T for d, p in probs]
    big = [_dot_nt(jnp.concatenate([a, b], axis=0), jnp.concatenate([c, e], axis=0))
           for a, b, c, e in zip(kt, rt, at, kb)]
    npow = [-jnp.where(strict[d], x[:n2, :n2], 0.0) for (d, p), x in zip(probs, big)]
    tinv = [eye + x for x in npow]
    for _ in range(int(math.log2(C)) - 1):
        npow = [_dot(x, x) for x in npow]
        tinv = [x + _dot(x, y) for x, y in zip(tinv, npow)]
    state = [st[d, p] for d, p in probs]
    lhs = [jnp.concatenate(
        [jnp.concatenate([a, jnp.where(strict[d], x[:n2, n2:], 0.0)], axis=1),
         jnp.concatenate([b, jnp.where(incl[d], x[n2:, n2:], 0.0)], axis=1)], axis=0)
        for (d, p), x, a, b in zip(probs, big, kt, rt)]
    xy = [_dot_nt(a, jnp.concatenate([x, v], axis=1)) for a, x, v in zip(lhs, state, vs_t)]
    u_m = [-_dot(x, y[:n2]) for x, y in zip(tinv, xy)]
    if need_y:
        for (d, p), x, u, y in zip(probs, big, u_m, xy):
            yy = y[n2:] + _dot(jnp.where(incl[d], x[n2:, :n2], 0.0), u)
            y_refs[d][:, p * LANE:(p + 1) * LANE] = yy[:C] + yy[C:]
    for (d, p), x, a, u, v, k2, a2 in zip(probs, state, tot, u_m, vs_t, kb2, at2):
        st[d, p] = x * jnp.exp(a) + _dot(jnp.concatenate([u.T, v], axis=1), jnp.concatenate([a2, k2], axis=0))
    sfin_refs[0][...] = st[0]
    sfin_refs[1][...] = st[1]


def _rwkv_scan(r, lwf, lwb, k, v, kk, ka, nseq, seqlen, s0f, s0b, need_y):
    C = RWKV_CHUNK
    nc = seqlen // C
    tok_f = pl.BlockSpec((C, RWKV_DIM), lambda b, c: (b * nc + c, 0))
    tok_b = pl.BlockSpec((C, RWKV_DIM), lambda b, c: (b * nc + nc - 1 - c, 0))
    sshape = (RWKV_PAIRS, LANE, LANE)
    sspec = pl.BlockSpec((None,) + sshape, lambda b, c: (b, 0, 0, 0))
    sstruct = jax.ShapeDtypeStruct((nseq,) + sshape, F32)
    out_specs, out_shape = [sspec, sspec], [sstruct, sstruct]
    if need_y:
        ystruct = jax.ShapeDtypeStruct((nseq * seqlen, RWKV_DIM), F32)
        out_specs, out_shape = [tok_f, tok_b] + out_specs, [ystruct, ystruct] + out_shape
    res = pl.pallas_call(
        functools.partial(_rwkv_scan_kernel, C=C, need_y=need_y),
        grid=(nseq, nc),
        in_specs=[tok_f] * 6 + [tok_b] * 6 + [sspec, sspec],
        out_specs=out_specs, out_shape=out_shape,
        scratch_shapes=[pltpu.VMEM((2,) + sshape, F32)],
        compiler_params=_params("parallel", "arbitrary"),
    )(r, lwf, k, v, kk, ka, r, lwb, k, v, kk, ka, s0f, s0b)
    return tuple(res) if need_y else (None, None) + tuple(res)


def _rwkv_fin_kernel(yf_ref, yb_ref, bonus_ref, g_ref, lg_ref, lb_ref, bd_ref, o_ref):
    y = yf_ref[...] + yb_ref[...]
    bd = bd_ref[...]
    dv = y - _blocksum(y, bd) * (1.0 / RWKV_HEAD_DIM)
    var = _blocksum(dv * dv, bd) * (1.0 / RWKV_HEAD_DIM)
    yn = dv * lax.rsqrt(var + GN_EPS) * lg_ref[...] + lb_ref[...]
    o_ref[...] = ((yn + bonus_ref[...]) * g_ref[...]).astype(o_ref.dtype)


def _rwkv_finish(yf, yb, bonus, g, ln_g, ln_b, bd):
    nrows = yf.shape[0]
    tq = _tile(nrows, 512)
    tok = pl.BlockSpec((tq, RWKV_DIM), lambda i: (i, 0))
    return pl.pallas_call(
        _rwkv_fin_kernel,
        grid=(nrows // tq,),
        in_specs=[tok] * 4 + [_const_spec((1, RWKV_DIM))] * 2 + [_const_spec(bd.shape)],
        out_specs=tok,
        out_shape=jax.ShapeDtypeStruct((nrows, RWKV_DIM), BF16),
        compiler_params=_params("parallel"),
    )(yf, yb, bonus, g, ln_g, ln_b, bd)


def _to_cols(t, b, rows):
    return t.reshape(b, rows, GRID_W, -1).transpose(0, 2, 1, 3).reshape(b * rows * GRID_W, -1)


def _to_rows(t, b, rows):
    return t.reshape(b, GRID_W, rows, -1).transpose(0, 2, 1, 3).reshape(b * rows * GRID_W, -1)


def _pad_cols(w, n):
    return jnp.pad(w, [(0, 0)] * (w.ndim - 1) + [(0, n - w.shape[-1])])


def _pad_rows(w, n):
    return jnp.pad(w, [(0, 0)] * (w.ndim - 2) + [(0, n - w.shape[-2]), (0, 0)])


def kernel(x, c, ctx, c_ctx, mod_w, mod_b, norm_g, w_in, conv_w, conv_b, conv_ln_g, conv_ln_b, conv_out, ssd_conv_w, ssd_conv_b, ssd_A_log, ssd_dt_bias, ssd_D, ssd_norm_g, ssd_out, fourier_out, rwkv_mu, rwkv_w0, rwkv_w2, rwkv_a0, rwkv_a2, rwkv_g2, rwkv_k_k, rwkv_k_a, rwkv_r_k, rwkv_ln_g, rwkv_ln_b, rwkv_out, w_o, mlp_up, mlp_down):
    B, L, D = x.shape
    CL = ctx.shape[1]
    depth = mod_w.shape[0]
    rows = L // GRID_W
    n_lat, n_ctx = B * L, B * CL
    n_all = n_lat + n_ctx
    tm = _tile(math.gcd(L, n_ctx), 1024)
    tm_small = _tile(tm, 512)
    tm_merge = _tile(tm, 256)
    mod_idx = lambda t: (lambda i: jnp.minimum((i * t) // L, B))

    w_u, w_rw, w_gate = _split_w_in(w_in)
    down_w, wo_w = mlp_down.astype(BF16), w_o.astype(BF16)
    branch_w = [conv_out.astype(BF16), ssd_out.astype(BF16), fourier_out.astype(BF16), rwkv_out.astype(BF16)]
    mu = rwkv_mu
    mu_cols = jnp.concatenate(
        [mu[:, :3 * RWKV_DIM]] + [_pad_cols(mu[:, a:b], LANE) for a, b in ((1536, 1600), (1600, 1664), (1664, 1728))]
        + [mu[:, 1728:]], axis=1)
    head_of = jnp.arange(RWKV_DIM, dtype=jnp.int32) // RWKV_HEAD_DIM
    bd = (head_of[:, None] == head_of[None, :]).astype(BF16)
    gidx = jnp.arange(FOURIER_DIM, dtype=jnp.int32)
    cc, sc_ = _dft_mats(FOURIER_GROUP_DIM, FOURIER_GROUP_DIM ** -0.5)
    same_group = (gidx[:, None] // FOURIER_GROUP_DIM) == (gidx[None, :] // FOURIER_GROUP_DIM)
    tile_g = lambda m: jnp.where(same_group, jnp.tile(m, (FOURIER_GROUPS, FOURIER_GROUPS)), 0.0)
    chan_w = jnp.concatenate([tile_g(cc), -tile_g(sc_)], axis=1).astype(BF16)
    conv_w_p = _pad_rows(conv_w, 32)
    ssd_conv_w_p = _pad_rows(ssd_conv_w, SUBLANE)
    dt_bias_p = _pad_cols(ssd_dt_bias.reshape(depth, 1, 2 * SSD_HEADS), U_DT_W)
    a_log_p = _pad_cols(ssd_A_log.reshape(depth, 1, 2 * SSD_HEADS), U_DT_W)
    d_skip = jnp.repeat(ssd_D, SSD_HEAD_DIM, axis=1).reshape(depth, 1, SSD_DIM)

    cond = jnp.concatenate([c, c_ctx[None, :], jnp.zeros((SUBLANE - (B + 1) % SUBLANE, D), F32)], axis=0)
    mod = _mod_all(cond, mod_w, mod_b)

    xs = jnp.concatenate([x.reshape(n_lat, D), ctx.reshape(n_ctx, D)], axis=0)
    ssd_zero = jnp.zeros((B, SSD_GROUPS, SSD_STATE, SSD_HEADS_PER_GROUP * SSD_HEAD_DIM), F32)
    rwkv_zero = jnp.zeros((B, RWKV_PAIRS, LANE, LANE), F32)

    mods = lambda l: tuple(mod[l, :, j * D:(j + 1) * D].reshape(-1, 1, D) for j in range(6))
    h = _norm_mod(xs, norm_g[0, 0].reshape(1, D), *mods(0)[1::-1], n_all, mod_idx(tm), tm)
    for i in range(depth):
        last = i == depth - 1
        m_out = n_lat if last else n_all
        sh1, sc1, g1, sh2, sc2, g2 = mods(i)
        ng = norm_g[i].reshape(4, 1, D)

        u = _matmul(h, w_u, n_all, tm, F32, tn_cap=1536, layer=i)
        gates = _matmul(h, w_gate, m_out, tm, BF16, epilogue="sigmoid", tn_cap=2048, layer=i)

        a_conv = _conformer(u, 0, n_lat, GRID_W, conv_w_p[i], conv_b[i][None], conv_ln_g[i][None], conv_ln_b[i][None])
        z = _matmul(u, chan_w, m_out, tm, BF16, a_col=U_FFT // U_FFT_W)
        a_fft = _fourier(z, 0, B, L)
        if not last:
            a_conv = jnp.concatenate(
                [a_conv, _conformer(u, n_lat, n_ctx, CL, conv_w_p[i], conv_b[i][None], conv_ln_g[i][None],
                                    conv_ln_b[i][None])], axis=0)
            a_fft = jnp.concatenate([a_fft, _fourier(z, n_lat, B, CL)], axis=0)

        ssd_args = (dt_bias_p[i], a_log_p[i])
        xs_c, bm_c, cm_c = _ssd_conv(u, n_lat, B, CL, ssd_conv_w_p[i], ssd_conv_b[i][None])
        yfc, sf = _ssd_scan(xs_c, bm_c, cm_c, u, n_lat, B, CL, *ssd_args, ssd_zero, False, not last)
        ybc, sb = _ssd_scan(xs_c, bm_c, cm_c, u, n_lat, B, CL, *ssd_args, ssd_zero, True, not last)
        xs_l, bm_l, cm_l = _ssd_conv(u, 0, B, L, ssd_conv_w_p[i], ssd_conv_b[i][None])
        yfl, _ = _ssd_scan(xs_l, bm_l, cm_l, u, 0, B, L, *ssd_args, sf, False, True)
        ybl, _ = _ssd_scan(xs_l, bm_l, cm_l, u, 0, B, L, *ssd_args, sb, True, True)
        a_ssd = _ssd_finish(yfl, ybl, xs_l, u, 0, d_skip[i], ssd_norm_g[i][None])
        if not last:
            a_ssd = jnp.concatenate(
                [a_ssd, _ssd_finish(yfc, ybc, xs_c, u, n_lat, d_skip[i], ssd_norm_g[i][None])], axis=0)

        rp = dict(mu=mu_cols[i][None], w0=rwkv_w0[i], w2=_pad_rows(rwkv_w2[i], LANE).astype(BF16),
                  a0=rwkv_a0[i][None], a2=_pad_rows(rwkv_a2[i], LANE).astype(BF16), g2=rwkv_g2[i].astype(BF16),
                  k_k=rwkv_k_k[i][None], k_a=rwkv_k_a[i][None], r_k=rwkv_r_k[i].reshape(1, RWKV_DIM), bd=bd)
        u_rc = _matmul(h, w_rw, n_ctx, tm, F32, a_row=n_lat // tm, layer=i)
        r_, lwf, lwb, k_, v_, kk_, ka_, g_, bonus = _rwkv_prep(u_rc, 0, B, CL, rp)
        yfc, ybc, s_f, s_b = _rwkv_scan(r_, lwf, lwb, k_, v_, kk_, ka_, B, CL, rwkv_zero, rwkv_zero, not last)
        if not last:
            a_rwkv_c = _rwkv_finish(yfc, ybc, bonus, g_, rwkv_ln_g[i][None], rwkv_ln_b[i][None], bd)
        u_rl = _matmul(_to_cols(h[:n_lat], B, rows), w_rw, n_lat, tm, F32, layer=i)
        r_, lwf, lwb, k_, v_, kk_, ka_, g_, bonus = _rwkv_prep(u_rl, 0, B, L, rp)
        yfl, ybl, _, _ = _rwkv_scan(r_, lwf, lwb, k_, v_, kk_, ka_, B, L, s_f, s_b, True)
        a_rwkv = _to_rows(_rwkv_finish(yfl, ybl, bonus, g_, rwkv_ln_g[i][None], rwkv_ln_b[i][None], bd), B, rows)
        if not last:
            a_rwkv = jnp.concatenate([a_rwkv, a_rwkv_c], axis=0)

        m = _merge(gates, [a_conv, a_ssd, a_fft, a_rwkv], branch_w, m_out, tm_merge, i)
        xs, h2 = _matmul_residual(m, wo_w, xs, g1, ng[1], m_out, mod_idx(tm_small), tm_small,
                                  nxt=(ng[2], sc2, sh2), layer=i)

        mid = _matmul_f32w(h2, mlp_up, m_out, tm, BF16, epilogue="relu2", layer=i)
        if last:
            xs = _matmul_residual(mid, down_w, xs, g2, ng[3], m_out, mod_idx(tm_small), tm_small, layer=i)
        else:
            xs, h = _matmul_residual(mid, down_w, xs, g2, ng[3], m_out, mod_idx(tm_small), tm_small,
                                     nxt=(norm_g[i + 1, 0].reshape(1, D), *mods(i + 1)[1::-1]), layer=i)

    return xs[:n_lat].reshape(B, L, D)
```

```python
import functools
import math

import jax
import jax.numpy as jnp
import numpy as np
from jax import lax
from jax.experimental import pallas as pl
from jax.experimental.pallas import tpu as pltpu

F32 = jnp.float32
BF16 = jnp.bfloat16

GRID_W = 64
RMS_EPS = 1e-6
LN_EPS = 1e-5
GN_EPS = 64e-5
CONV_DIM = 512
CONV_WIDTH = 31
SSD_HEADS = 12
SSD_HEAD_DIM = 64
SSD_DIM = SSD_HEADS * SSD_HEAD_DIM
SSD_GROUPS = 4
SSD_HEADS_PER_GROUP = SSD_HEADS // SSD_GROUPS
SSD_STATE = 128
SSD_CONV = 5
SSD_BC = SSD_GROUPS * SSD_STATE
SSD_XBC = SSD_DIM + 2 * SSD_BC
SSD_IN = SSD_DIM + SSD_XBC + 2 * SSD_HEADS
SSD_CHUNK = 128
FOURIER_GROUPS = 4
FOURIER_GROUP_DIM = 128
FOURIER_DIM = FOURIER_GROUPS * FOURIER_GROUP_DIM
RWKV_HEADS = 8
RWKV_HEAD_DIM = 64
RWKV_DIM = RWKV_HEADS * RWKV_HEAD_DIM
RWKV_PAIRS = RWKV_HEADS // 2
DECAY_LORA = 64
ICL_LORA = 64
GATE_LORA = 128
RWKV_IN = 3 * RWKV_DIM + 2 * DECAY_LORA + ICL_LORA + GATE_LORA
RWKV_CHUNK = 64
REC_IN = SSD_IN + RWKV_IN
N_BRANCH = 4

LANE = 128
SUBLANE = 8
VMEM_LIMIT = 56 * 1024 * 1024

U_XBC, U_XBC_W = 0, SSD_XBC
U_DT, U_DT_W = 1792, LANE
U_CONV, U_CONV_W = 2048, 2 * CONV_DIM
U_Z, U_Z_W = 3072, SSD_DIM
U_FFT, U_FFT_W = 4096, FOURIER_DIM
U_N = 4608
U_RWKV_W = 2048
RW_WF, RW_WB, RW_AL, RW_GL = 1536, 1664, 1792, 1920
RP_R, RP_LWF, RP_LWB, RP_K, RP_V, RP_KK, RP_KA, RP_N = 0, 1, 2, 3, 4, 5, 6, 7


def _params(*sem):
    return pltpu.CompilerParams(dimension_semantics=sem, vmem_limit_bytes=VMEM_LIMIT)


def _tile(n, cap, quantum=SUBLANE):
    if n <= cap:
        return n
    t = (cap // quantum) * quantum
    while t >= quantum:
        if n % t == 0:
            return t
        t -= quantum
    raise ValueError(f"no tile for {n} under {cap}")


def _const_spec(shape):
    nd = len(shape)
    return pl.BlockSpec(shape, lambda *_: (0,) * nd)


def _dot(a, b):
    return jnp.dot(a.astype(BF16), b.astype(BF16), preferred_element_type=F32)


def _dot_nt(a, b):
    return lax.dot_general(a.astype(BF16), b.astype(BF16), (((1,), (1,)), ((), ())),
                           preferred_element_type=F32)


def _split3(x):
    h1 = x.astype(BF16)
    r1 = x - h1.astype(F32)
    h2 = r1.astype(BF16)
    h3 = (r1 - h2.astype(F32)).astype(BF16)
    return h1, h2, h3


def _dot_exact_lhs(m, x):
    h1, h2, h3 = _split3(x)
    d = lambda h: jnp.dot(m, h, preferred_element_type=F32)
    return d(h1) + d(h2) + d(h3)


def _blocksum(x, bd):
    hi = x.astype(BF16)
    lo = (x - hi.astype(F32)).astype(BF16)
    return (jnp.dot(hi, bd, preferred_element_type=F32) + jnp.dot(lo, bd, preferred_element_type=F32))


def _sigmoid(x):
    return 1.0 / (1.0 + jnp.exp(-x))


def _softplus(x):
    return jnp.maximum(x, 0.0) + jnp.log1p(jnp.exp(-jnp.abs(x)))


def _rms(y, g):
    return y * lax.rsqrt(jnp.mean(y * y, axis=-1, keepdims=True) + RMS_EPS) * g


_O_RWKV, _O_CONV = SSD_IN, REC_IN
_O_FFT, _O_GATE = REC_IN + 2 * CONV_DIM, REC_IN + 2 * CONV_DIM + FOURIER_DIM
_W_IN_PIECES = (
    ("rw", 0, _O_RWKV, 3 * RWKV_DIM), ("rw", RW_WF, _O_RWKV + 1536, DECAY_LORA),
    ("rw", RW_WB, _O_RWKV + 1600, DECAY_LORA), ("rw", RW_AL, _O_RWKV + 1664, ICL_LORA),
    ("rw", RW_GL, _O_RWKV + 1728, GATE_LORA),
    ("u", U_XBC, SSD_DIM, SSD_XBC), ("u", U_DT, SSD_DIM + SSD_XBC, 2 * SSD_HEADS),
    ("u", U_CONV, _O_CONV, 2 * CONV_DIM), ("u", U_Z, 0, SSD_DIM), ("u", U_FFT, _O_FFT, FOURIER_DIM),
)
_RELAYOUT_CHUNK = 1024


def _split_w_in_kernel(w_ref, u_ref, rw_ref, g_ref):
    n_in = w_ref.shape[-1]
    dst = {"u": u_ref, "rw": rw_ref, "g": g_ref}
    u_ref[...] = jnp.zeros_like(u_ref)
    rw_ref[...] = jnp.zeros_like(rw_ref)
    pieces = _W_IN_PIECES + (("g", 0, _O_GATE, g_ref.shape[-1]),)
    for name, dcol, scol, width in pieces:
        for off in range(0, width, _RELAYOUT_CHUNK):
            w = min(_RELAYOUT_CHUNK, width - off)
            lo = scol + off
            lo_al = (lo // LANE) * LANE
            hi_al = min(-(-(lo + w) // LANE) * LANE, n_in)
            x = w_ref[:, lo_al:hi_al]
            dst[name][:, dcol + off:dcol + off + w] = x[:, lo - lo_al:lo - lo_al + w].astype(BF16)


def _split_w_in(w_in):
    depth, d, n_in = w_in.shape
    n_gate = n_in - _O_GATE
    tk = _tile(d, 128)
    outs = (U_N, U_RWKV_W, n_gate)
    return pl.pallas_call(
        _split_w_in_kernel,
        grid=(depth, d // tk),
        in_specs=[pl.BlockSpec((None, tk, n_in), lambda l, i: (l, i, 0))],
        out_specs=[pl.BlockSpec((None, tk, n), lambda l, i: (l, i, 0)) for n in outs],
        out_shape=[jax.ShapeDtypeStruct((depth, d, n), BF16) for n in outs],
        compiler_params=_params("parallel", "parallel"),
    )(w_in)


def _mod_kernel(a_ref, w_ref, b_ref, o_ref):
    a = a_ref[...]
    a = a * _sigmoid(a)
    o_ref[...] = _dot(a, w_ref[...]) + b_ref[...]


def _mod_all(cond, mod_w, mod_b):
    depth, d, n = mod_w.shape
    tn = _tile(n, 1024, LANE)
    return pl.pallas_call(
        _mod_kernel,
        grid=(depth, n // tn),
        in_specs=[_const_spec(cond.shape),
                  pl.BlockSpec((None, d, tn), lambda l, j: (l, 0, j)),
                  pl.BlockSpec((None, 1, tn), lambda l, j: (l, 0, j))],
        out_specs=pl.BlockSpec((None, cond.shape[0], tn), lambda l, j: (l, 0, j)),
        out_shape=jax.ShapeDtypeStruct((depth, cond.shape[0], n), F32),
        compiler_params=_params("parallel", "parallel"),
    )(cond, mod_w, mod_b.reshape(depth, 1, n))


def _norm_mod_kernel(x_ref, g_ref, sc_ref, sh_ref, o_ref):
    o_ref[...] = (_rms(x_ref[...], g_ref[...]) * (1.0 + sc_ref[...]) + sh_ref[...]).astype(o_ref.dtype)


def _norm_mod(x, g, sc, sh, m_rows, mod_idx, tm):
    d = x.shape[1]
    return pl.pallas_call(
        _norm_mod_kernel,
        grid=(m_rows // tm,),
        in_specs=[pl.BlockSpec((tm, d), lambda i: (i, 0)),
                  _const_spec((1, d)),
                  pl.BlockSpec((None, 1, d), lambda i: (mod_idx(i), 0, 0)),
                  pl.BlockSpec((None, 1, d), lambda i: (mod_idx(i), 0, 0))],
        out_specs=pl.BlockSpec((tm, d), lambda i: (i, 0)),
        out_shape=jax.ShapeDtypeStruct((m_rows, d), BF16),
        compiler_params=_params("parallel"),
    )(x, g, sc, sh)


def _mm_kernel(a_ref, w_ref, o_ref, *, epilogue):
    acc = _dot(a_ref[...], w_ref[...])
    if epilogue == "relu2":
        acc = jnp.square(jnp.maximum(acc, 0.0))
    elif epilogue == "sigmoid":
        acc = _sigmoid(acc)
    o_ref[...] = acc.astype(o_ref.dtype)


def _layer_spec(w, layer, block, index):
    if w.ndim == 2:
        return pl.BlockSpec(block, index)
    return pl.BlockSpec((None,) + block, lambda *ids: (layer,) + index(*ids))


def _matmul(a, w, m_rows, tm, out_dtype, epilogue=None, a_col=0, a_row=0, tn_cap=1024, layer=0):
    k, n = w.shape[-2:]
    tn = _tile(n, tn_cap, LANE)
    return pl.pallas_call(
        functools.partial(_mm_kernel, epilogue=epilogue),
        grid=(m_rows // tm, n // tn),
        in_specs=[pl.BlockSpec((tm, k), lambda i, j: (a_row + i, a_col)),
                  _layer_spec(w, layer, (k, tn), lambda i, j: (0, j))],
        out_specs=pl.BlockSpec((tm, tn), lambda i, j: (i, j)),
        out_shape=jax.ShapeDtypeStruct((m_rows, n), out_dtype),
        compiler_params=_params("parallel", "arbitrary"),
    )(a, w)


def _mm_wcast_kernel(a_ref, w_ref, o_ref, wb_ref, *, epilogue):
    @pl.when(pl.program_id(1) == 0)
    def _():
        wb_ref[...] = w_ref[...].astype(wb_ref.dtype)

    _mm_kernel(a_ref, wb_ref, o_ref, epilogue=epilogue)


def _matmul_f32w(a, w, m_rows, tm, out_dtype, epilogue=None, tn_cap=1024, layer=0):
    k, n = w.shape[-2:]
    tn = _tile(n, tn_cap, LANE)
    return pl.pallas_call(
        functools.partial(_mm_wcast_kernel, epilogue=epilogue),
        grid=(n // tn, m_rows // tm),
        in_specs=[pl.BlockSpec((tm, k), lambda j, i: (i, 0)),
                  _layer_spec(w, layer, (k, tn), lambda j, i: (0, j))],
        out_specs=pl.BlockSpec((tm, tn), lambda j, i: (i, j)),
        out_shape=jax.ShapeDtypeStruct((m_rows, n), out_dtype),
        scratch_shapes=[pltpu.VMEM((k, tn), BF16)],
        compiler_params=_params("parallel", "arbitrary"),
    )(a, w)


def _mm_res_kernel(a_ref, w_ref, x_ref, gate_ref, g_ref, *rest, with_next, nk):
    if with_next:
        g2_ref, sc_ref, sh_ref, o_ref, h_ref = rest
    else:
        (o_ref,) = rest
    def finish(y):
        xn = x_ref[...] + gate_ref[...] * _rms(y, g_ref[...])
        o_ref[...] = xn
        if with_next:
            h_ref[...] = (_rms(xn, g2_ref[...]) * (1.0 + sc_ref[...]) + sh_ref[...]).astype(h_ref.dtype)

    if nk == 1:
        finish(_dot(a_ref[...], w_ref[...]))
        return
    k = pl.program_id(1)

    @pl.when(k == 0)
    def _():
        o_ref[...] = jnp.zeros_like(o_ref)

    o_ref[...] += _dot(a_ref[...], w_ref[...])

    @pl.when(k == nk - 1)
    def _():
        finish(o_ref[...])


def _matmul_residual(a, w, x, gate, g, m_rows, mod_idx, tm, nxt=None, tk_cap=2048, layer=0):
    k, d = w.shape[-2:]
    tk = _tile(k, tk_cap, LANE)
    mod_spec = pl.BlockSpec((None, 1, d), lambda i, kk: (mod_idx(i), 0, 0))
    tile_spec = pl.BlockSpec((tm, d), lambda i, kk: (i, 0))
    in_specs = [pl.BlockSpec((tm, tk), lambda i, kk: (i, kk)), _layer_spec(w, layer, (tk, d), lambda i, kk: (kk, 0)),
                tile_spec, mod_spec, _const_spec((1, d))]
    out_specs, out_shape, args = tile_spec, jax.ShapeDtypeStruct((m_rows, d), F32), (a, w, x, gate, g)
    if nxt is not None:
        in_specs += [_const_spec((1, d)), mod_spec, mod_spec]
        out_specs, out_shape = [tile_spec, tile_spec], [out_shape, jax.ShapeDtypeStruct((m_rows, d), BF16)]
        args += tuple(nxt)
    return pl.pallas_call(
        functools.partial(_mm_res_kernel, with_next=nxt is not None, nk=k // tk),
        grid=(m_rows // tm, k // tk),
        in_specs=in_specs, out_specs=out_specs, out_shape=out_shape,
        compiler_params=_params("parallel", "arbitrary"),
    )(*args)


def _merge_kernel(gates_ref, a0, a1, a2, a3, w0, w1, w2, w3, o_ref):
    d = o_ref.shape[-1]
    m = None
    for i, (a, w) in enumerate(((a0, w0), (a1, w1), (a2, w2), (a3, w3))):
        t = gates_ref[:, i * d:(i + 1) * d].astype(F32) * _dot(a[...], w[...])
        m = t if m is None else m + t
    o_ref[...] = m.astype(o_ref.dtype)


def _merge(gates, acts, weights, m_rows, tm, layer):
    d = weights[0].shape[-1]
    return pl.pallas_call(
        _merge_kernel,
        grid=(m_rows // tm,),
        in_specs=[pl.BlockSpec((tm, N_BRANCH * d), lambda i: (i, 0))]
        + [pl.BlockSpec((tm, a.shape[1]), lambda i: (i, 0)) for a in acts]
        + [_layer_spec(w, layer, w.shape[-2:], lambda i: (0, 0)) for w in weights],
        out_specs=pl.BlockSpec((tm, d), lambda i: (i, 0)),
        out_shape=jax.ShapeDtypeStruct((m_rows, d), BF16),
        compiler_params=_params("parallel"),
    )(gates, *acts, *weights)


def _conformer_kernel(u_ref, w_ref, b_ref, lg_ref, lb_ref, o_ref, sh_ref, *, seg, nseg):
    pad = 16
    first = pad - (CONV_WIDTH - 1) // 2
    span = seg + 2 * pad - SUBLANE
    zeros = jnp.zeros((pad, CONV_DIM), F32)
    sh_ref[0, 0:pad, :] = zeros
    sh_ref[0, pad + seg:2 * pad + seg, :] = zeros

    def body(s, carry):
        r0 = pl.multiple_of(s * seg, seg)
        u = u_ref[pl.ds(r0, seg), :]
        sh_ref[0, pad:pad + seg, :] = u[:, :CONV_DIM] * _sigmoid(u[:, CONV_DIM:])
        for r in range(1, SUBLANE):
            sh_ref[r, 0:span, :] = sh_ref[0, pl.ds(r, span), :]
        acc = jnp.zeros((seg, CONV_DIM), F32) + b_ref[...]
        for j in range(CONV_WIDTH):
            start = first + j
            acc = acc + w_ref[j:j + 1, :] * sh_ref[start % SUBLANE, pl.ds(start - start % SUBLANE, seg), :]
        mu = jnp.mean(acc, axis=-1, keepdims=True)
        dv = acc - mu
        var = jnp.mean(dv * dv, axis=-1, keepdims=True)
        v = dv * lax.rsqrt(var + LN_EPS) * lg_ref[...] + lb_ref[...]
        o_ref[pl.ds(r0, seg), :] = (v * _sigmoid(v)).astype(o_ref.dtype)
        return carry

    lax.fori_loop(0, nseg, body, 0)


def _conformer(u, row0, nrows, seg, conv_w, conv_b, ln_g, ln_b):
    tq = max(seg, min(256, nrows))
    blk0 = row0 // tq
    return pl.pallas_call(
        functools.partial(_conformer_kernel, seg=seg, nseg=tq // seg),
        grid=(nrows // tq,),
        in_specs=[pl.BlockSpec((tq, U_CONV_W), lambda i: (blk0 + i, U_CONV // U_CONV_W)),
                  _const_spec(conv_w.shape), _const_spec((1, CONV_DIM)),
                  _const_spec((1, CONV_DIM)), _const_spec((1, CONV_DIM))],
        out_specs=pl.BlockSpec((tq, CONV_DIM), lambda i: (i, 0)),
        out_shape=jax.ShapeDtypeStruct((nrows, CONV_DIM), BF16),
        scratch_shapes=[pltpu.VMEM((SUBLANE, seg + 32, CONV_DIM), F32)],
        compiler_params=_params("parallel"),
    )(u, conv_w, conv_b, ln_g, ln_b)


def _dft_kernel(c_ref, s_ref, z_ref, o_ref):
    o_ref[...] = (jnp.dot(c_ref[...], z_ref[:, :FOURIER_DIM], preferred_element_type=F32)
                  + jnp.dot(s_ref[...], z_ref[:, FOURIER_DIM:], preferred_element_type=F32)).astype(o_ref.dtype)


def _dft_mats(n, scale):
    idx = np.arange(n, dtype=np.int64)
    ang = ((idx[:, None] * idx[None, :]) % n).astype(np.float64) * (2.0 * math.pi / n)
    return (jnp.asarray((np.cos(ang) * scale).astype(np.float32)),
            jnp.asarray((np.sin(ang) * scale).astype(np.float32)))


def _fourier(z, row0, nseq, seqlen):
    nrows = nseq * seqlen
    cl, sl = _dft_mats(seqlen, seqlen ** -0.5)
    cl, sl = cl.astype(BF16), sl.astype(BF16)
    tm = _tile(seqlen, 512)
    nt = seqlen // tm
    sblk0 = row0 // seqlen
    return pl.pallas_call(
        _dft_kernel,
        grid=(nt, nseq),
        in_specs=[pl.BlockSpec((tm, seqlen), lambda i, b: (i, 0)),
                  pl.BlockSpec((tm, seqlen), lambda i, b: (i, 0)),
                  pl.BlockSpec((seqlen, 2 * FOURIER_DIM), lambda i, b: (sblk0 + b, 0))],
        out_specs=pl.BlockSpec((tm, FOURIER_DIM), lambda i, b: (b * nt + i, 0)),
        out_shape=jax.ShapeDtypeStruct((nrows, FOURIER_DIM), BF16),
        compiler_params=_params("parallel", "arbitrary"),
    )(cl, sl, z)


def _halo_specs(tq, width, blk0, col_blk, nblk_total):
    hb = tq // SUBLANE
    cur = pl.BlockSpec((tq, width), lambda i: (blk0 + i, col_blk))
    prev = pl.BlockSpec((SUBLANE, width), lambda i: (jnp.maximum((blk0 + i) * hb - 1, 0), col_blk))
    nxt = pl.BlockSpec((SUBLANE, width),
                       lambda i: (jnp.minimum((blk0 + i + 1) * hb, nblk_total - 1), col_blk))
    return cur, prev, nxt


def _fill_halo(scr, cur_ref, prev_ref, next_ref, tq, tiles_per_seq):
    i = pl.program_id(0)
    first = (i % tiles_per_seq) == 0
    last = (i % tiles_per_seq) == tiles_per_seq - 1
    scr[0:SUBLANE, :] = jnp.where(first, 0.0, prev_ref[...])
    scr[SUBLANE:SUBLANE + tq, :] = cur_ref[...]
    scr[SUBLANE + tq:2 * SUBLANE + tq, :] = jnp.where(last, 0.0, next_ref[...])


def _ssd_conv_kernel(cur_ref, prev_ref, next_ref, w_ref, b_ref, xs_ref, bm_ref, cm_ref, scr, *, tq, tiles_per_seq):
    _fill_halo(scr, cur_ref, prev_ref, next_ref, tq, tiles_per_seq)
    half = (SSD_CONV - 1) // 2
    cw = 256
    for c0 in range(0, SSD_XBC, cw):
        acc = jnp.zeros((tq, cw), F32) + b_ref[:, c0:c0 + cw]
        for j in range(SSD_CONV):
            acc = acc + w_ref[j:j + 1, c0:c0 + cw] * scr[pl.ds(SUBLANE - half + j, tq), c0:c0 + cw]
        y = acc * _sigmoid(acc)
        if c0 < SSD_DIM:
            xs_ref[:, c0:c0 + cw] = y
        elif c0 < SSD_DIM + SSD_BC:
            bm_ref[:, c0 - SSD_DIM:c0 - SSD_DIM + cw] = y
        else:
            cm_ref[:, c0 - SSD_DIM - SSD_BC:c0 - SSD_DIM - SSD_BC + cw] = y


def _ssd_conv(u, row0, nseq, seqlen, conv_w, conv_b):
    nrows = nseq * seqlen
    tq = _tile(seqlen, 256)
    cur, prev, nxt = _halo_specs(tq, U_XBC_W, row0 // tq, U_XBC // U_XBC_W, u.shape[0] // SUBLANE)
    outs = [jax.ShapeDtypeStruct((nrows, w), F32) for w in (SSD_DIM, SSD_BC, SSD_BC)]
    return pl.pallas_call(
        functools.partial(_ssd_conv_kernel, tq=tq, tiles_per_seq=seqlen // tq),
        grid=(nrows // tq,),
        in_specs=[cur, prev, nxt, _const_spec(conv_w.shape), _const_spec((1, SSD_XBC))],
        out_specs=[pl.BlockSpec((tq, w), lambda i: (i, 0)) for w in (SSD_DIM, SSD_BC, SSD_BC)],
        out_shape=outs,
        scratch_shapes=[pltpu.VMEM((tq + 2 * SUBLANE, U_XBC_W), F32)],
        compiler_params=_params("parallel"),
    )(u, u, u, conv_w, conv_b)


def _ssd_scan_kernel(xs_ref, b_ref, c_ref, dt_ref, bias_ref, alog_ref, sel_ref, s0_ref, *rest,
                     q, rev, direction, need_y):
    if need_y:
        y_ref, sfin_ref, st = rest
    else:
        sfin_ref, st = rest

    @pl.when(pl.program_id(1) == 0)
    def _():
        st[...] = s0_ref[...]

    hp, gw = SSD_HEAD_DIM, SSD_HEADS_PER_GROUP * SSD_HEAD_DIM
    dt = _softplus(dt_ref[...] + bias_ref[...])
    a = dt * (-jnp.exp(alog_ref[...]))
    ti = lax.broadcasted_iota(jnp.int32, (q, q), 0)
    tk = lax.broadcasted_iota(jnp.int32, (q, q), 1)
    before = (tk >= ti) if rev else (tk <= ti)
    cs = _dot_exact_lhs(jnp.where(before, 1.0, 0.0).astype(BF16), a)
    sel = sel_ref[...]
    spread = lambda parts: sum(jnp.dot(p, sel, preferred_element_type=F32) for p in parts)
    dt_b = spread(_split3(dt)[:2])
    cs_b = spread(_split3(cs))
    tot_b = cs_b[0:1, :] if rev else cs_b[q - 1:q, :]
    xd = xs_ref[...] * dt_b
    xdd = xd * jnp.exp(tot_b - cs_b)
    dec = jnp.exp(tot_b)
    if need_y:
        ecs = jnp.exp(cs_b)
        cs_t = cs.T
        lane0 = lax.broadcasted_iota(jnp.int32, (q, LANE), 1) < hp
    for g in range(SSD_GROUPS):
        cg = c_ref[:, g * SSD_STATE:(g + 1) * SSD_STATE]
        bg = b_ref[:, g * SSD_STATE:(g + 1) * SSD_STATE]
        state = st[g]
        if need_y:
            cb = _dot_nt(cg, bg)
            y_off = _dot(cg, state) * ecs[:, g * gw:(g + 1) * gw]
            for j in range(SSD_HEADS_PER_GROUP):
                h = g * SSD_HEADS_PER_GROUP + j
                col = direction * SSD_HEADS + h
                pair = cs_b[:, (h // 2) * LANE:(h // 2 + 1) * LANE]
                swapped = pltpu.roll(pair, hp, axis=1)
                cs_col = jnp.where(lane0, pair, swapped) if h % 2 == 0 else jnp.where(lane0, swapped, pair)
                lm = jnp.where(before, jnp.exp(jnp.minimum(cs_col - cs_t[col:col + 1, :], 0.0)), 0.0)
                y_ref[:, h * hp:(h + 1) * hp] = (_dot(cb * lm, xd[:, h * hp:(h + 1) * hp])
                                                 + y_off[:, j * hp:(j + 1) * hp])
        st[g] = state * dec[:, g * gw:(g + 1) * gw] + _dot(bg.T, xdd[:, g * gw:(g + 1) * gw])
    sfin_ref[...] = st[...]


def _ssd_scan(xs, bm, cm, u, row0, nseq, seqlen, dt_bias, a_log, s0, rev, need_y):
    q = SSD_CHUNK
    nc = seqlen // q
    ublk0 = row0 // q
    pos = (lambda c: nc - 1 - c) if rev else (lambda c: c)
    row = lambda b, c: (b * nc + pos(c), 0)
    sshape = (SSD_GROUPS, SSD_STATE, SSD_HEADS_PER_GROUP * SSD_HEAD_DIM)
    sspec = pl.BlockSpec((None,) + sshape, lambda b, c: (b, 0, 0, 0))
    head_col = int(rev) * SSD_HEADS + jnp.arange(SSD_DIM, dtype=jnp.int32) // SSD_HEAD_DIM
    sel = (jnp.arange(U_DT_W, dtype=jnp.int32)[:, None] == head_col[None, :]).astype(BF16)
    out_specs, out_shape = [sspec], [jax.ShapeDtypeStruct((nseq,) + sshape, F32)]
    if need_y:
        out_specs.insert(0, pl.BlockSpec((q, SSD_DIM), row))
        out_shape.insert(0, jax.ShapeDtypeStruct((nseq * seqlen, SSD_DIM), F32))
    res = pl.pallas_call(
        functools.partial(_ssd_scan_kernel, q=q, rev=rev, direction=int(rev), need_y=need_y),
        grid=(nseq, nc),
        in_specs=[pl.BlockSpec((q, SSD_DIM), row), pl.BlockSpec((q, SSD_BC), row), pl.BlockSpec((q, SSD_BC), row),
                  pl.BlockSpec((q, U_DT_W), lambda b, c: (ublk0 + b * nc + pos(c), U_DT // U_DT_W)),
                  _const_spec((1, U_DT_W)), _const_spec((1, U_DT_W)), _const_spec(sel.shape), sspec],
        out_specs=out_specs, out_shape=out_shape,
        scratch_shapes=[pltpu.VMEM(sshape, F32)],
        compiler_params=_params("parallel", "arbitrary"),
    )(xs, bm, cm, u, dt_bias, a_log, sel, s0)
    return res if need_y else (None, res[0])


def _ssd_fin_kernel(yf_ref, yb_ref, xs_ref, z_ref, d_ref, g_ref, o_ref):
    z = z_ref[...]
    y = (yf_ref[...] + yb_ref[...] + d_ref[...] * xs_ref[...]) * (z * _sigmoid(z))
    o_ref[...] = _rms(y, g_ref[...]).astype(o_ref.dtype)


def _ssd_finish(yf, yb, xs, u, row0, d_skip, norm_g):
    nrows = yf.shape[0]
    tq = _tile(nrows, 512)
    blk0 = row0 // tq
    tok = pl.BlockSpec((tq, SSD_DIM), lambda i: (i, 0))
    return pl.pallas_call(
        _ssd_fin_kernel,
        grid=(nrows // tq,),
        in_specs=[tok, tok, tok, pl.BlockSpec((tq, U_Z_W), lambda i: (blk0 + i, U_Z // U_Z_W)),
                  _const_spec((1, SSD_DIM)), _const_spec((1, SSD_DIM))],
        out_specs=tok,
        out_shape=jax.ShapeDtypeStruct((nrows, SSD_DIM), BF16),
        compiler_params=_params("parallel"),
    )(yf, yb, xs, u, d_skip, norm_g)


def _rwkv_prep_kernel(cur_ref, prev_ref, next_ref, mu_ref, w0_ref, w2_ref, a0_ref, a2_ref, g2_ref,
                      kk_ref, ka_ref, rk_ref, bd_ref,
                      p_o, g_o, bonus_o, scr, *, tq, tiles_per_seq):
    r_o, lwf_o, lwb_o, k_o, v_o, kk_o, ka_o = (
        p_o.at[:, c * RWKV_DIM:(c + 1) * RWKV_DIM] for c in (RP_R, RP_LWF, RP_LWB, RP_K, RP_V, RP_KK, RP_KA))
    _fill_halo(scr, cur_ref, prev_ref, next_ref, tq, tiles_per_seq)
    u = cur_ref[...]
    shift = 0.5 * (scr[pl.ds(SUBLANE - 1, tq), :] + scr[pl.ds(SUBLANE + 1, tq), :]) - u
    x = u + shift * mu_ref[...]
    r = x[:, 0:RWKV_DIM]
    k = x[:, RWKV_DIM:2 * RWKV_DIM]
    v = x[:, 2 * RWKV_DIM:3 * RWKV_DIM]
    bd = bd_ref[...]
    for d, (off, out) in enumerate(((RW_WF, lwf_o), (RW_WB, lwb_o))):
        lo = jnp.tanh(x[:, off:off + LANE])
        wl = -_softplus(-(w0_ref[d:d + 1, :] + _dot(lo, w2_ref[d]))) - 0.5
        out[...] = -jnp.exp(wl)
    a = _sigmoid(a0_ref[...] + _dot(x[:, RW_AL:RW_AL + LANE], a2_ref[...]))
    kk = k * kk_ref[...]
    kk = kk * lax.rsqrt(_blocksum(kk * kk, bd) + 1e-12)
    kmod = k * (1.0 + (a - 1.0) * ka_ref[...])
    r_o[...] = r
    k_o[...] = kmod
    v_o[...] = v
    kk_o[...] = kk
    ka_o[...] = kk * a
    g_o[...] = _dot(_sigmoid(x[:, RW_GL:RW_GL + GATE_LORA]), g2_ref[...])
    bonus_o[...] = _blocksum(r * kmod * rk_ref[...], bd) * v


def _rwkv_prep(u, row0, nseq, seqlen, p):
    nrows = nseq * seqlen
    tq = _tile(seqlen, 256)
    cur, prev, nxt = _halo_specs(tq, U_RWKV_W, row0 // tq, 0, u.shape[0] // SUBLANE)
    consts = [p["mu"], p["w0"], p["w2"], p["a0"], p["a2"], p["g2"], p["k_k"], p["k_a"], p["r_k"], p["bd"]]
    tok = pl.BlockSpec((tq, RWKV_DIM), lambda i: (i, 0))
    return pl.pallas_call(
        functools.partial(_rwkv_prep_kernel, tq=tq, tiles_per_seq=seqlen // tq),
        grid=(nrows // tq,),
        in_specs=[cur, prev, nxt] + [_const_spec(c.shape) for c in consts],
        out_specs=[pl.BlockSpec((tq, RP_N * RWKV_DIM), lambda i: (i, 0)), tok, tok],
        out_shape=[jax.ShapeDtypeStruct((nrows, RP_N * RWKV_DIM), F32)]
        + [jax.ShapeDtypeStruct((nrows, RWKV_DIM), F32)] * 2,
        scratch_shapes=[pltpu.VMEM((tq + 2 * SUBLANE, U_RWKV_W), F32)],
        compiler_params=_params("parallel"),
    )(u, u, u, *consts)


def _rwkv_scan_kernel(*refs, C, need_y):
    packed = refs[0:2]
    s0_refs = refs[2:4]
    if need_y:
        y_refs, sfin_refs, st = refs[4:6], refs[6:8], refs[8]
    else:
        sfin_refs, st = refs[4:6], refs[6]

    @pl.when(pl.program_id(1) == 0)
    def _():
        st[0] = s0_refs[0][...]
        st[1] = s0_refs[1][...]

    n2 = 2 * C
    row = lax.broadcasted_iota(jnp.int32, (n2, n2), 0)
    col = lax.broadcasted_iota(jnp.int32, (n2, n2), 1)
    t, s = row & (C - 1), col & (C - 1)
    strict = ((s < t), (s > t))
    incl = ((s <= t), (s >= t))
    eye = jnp.where(row == col, 1.0, 0.0)
    ti = lax.broadcasted_iota(jnp.int32, (C, C), 0)
    tk = lax.broadcasted_iota(jnp.int32, (C, C), 1)
    tri = (jnp.where(tk <= ti, 1.0, 0.0).astype(BF16), jnp.where(tk >= ti, 1.0, 0.0).astype(BF16))
    head0 = lax.broadcasted_iota(jnp.int32, (C, LANE), 1) < RWKV_HEAD_DIM

    def stack(x):
        return jnp.concatenate([jnp.where(head0, x, 0.0), jnp.where(head0, 0.0, x)], axis=0)

    probs = [(d, p) for d in (0, 1) for p in range(RWKV_PAIRS)]
    blocks = ((RP_R, RP_LWF, RP_K, RP_V, RP_KK, RP_KA), (RP_R, RP_LWB, RP_K, RP_V, RP_KK, RP_KA))
    ld = lambda d, p, i: packed[d][:, blocks[d][i] * RWKV_DIM + p * LANE:blocks[d][i] * RWKV_DIM + (p + 1) * LANE]
    lw = [ld(d, p, 1) for d, p in probs]
    cs = [_dot_exact_lhs(tri[d], x) for (d, p), x in zip(probs, lw)]
    tot = [x[C - 1:C, :] if d == 0 else x[0:1, :] for (d, p), x in zip(probs, cs)]
    ems = [jnp.exp(-x) for x in cs]
    edec = [jnp.exp(a - x) for a, x in zip(tot, cs)]
    kt = [stack(ld(d, p, 4) * jnp.exp(x - w)) for (d, p), x, w in zip(probs, cs, lw)]
    rt = [stack(ld(d, p, 0) * jnp.exp(x)) for (d, p), x in zip(probs, cs)]
    at = [stack(ld(d, p, 5) * e) for (d, p), e in zip(probs, ems)]
    kb = [stack(ld(d, p, 2) * e) for (d, p), e in zip(probs, ems)]
    at2 = [stack(ld(d, p, 5) * e) for (d, p), e in zip(probs, edec)]
    kb2 = [stack(ld(d, p, 2) * e) for (d, p), e in zip(probs, edec)]
    vs_t = [stack(ld(d, p, 3)).T for d, p in probs]
    big = [_dot_nt(jnp.concatenate([a, b], axis=0), jnp.concatenate([c, e], axis=0))
           for a, b, c, e in zip(kt, rt, at, kb)]
    npow = [-jnp.where(strict[d], x[:n2, :n2], 0.0) for (d, p), x in zip(probs, big)]
    tinv = [eye + x for x in npow]
    npow = [_dot(x, x) for x in npow]
    for step in range(int(math.log2(C)) - 1):
        if step < int(math.log2(C)) - 2:
            both = [_dot(jnp.concatenate([t, x], axis=0), x) for t, x in zip(tinv, npow)]
            tinv = [t + b[:n2] for t, b in zip(tinv, both)]
            npow = [b[n2:] for b in both]
        else:
            tinv = [t + _dot(t, x) for t, x in zip(tinv, npow)]
    state = [st[d, p] for d, p in probs]
    lhs = [jnp.concatenate(
        [jnp.concatenate([a, jnp.where(strict[d], x[:n2, n2:], 0.0)], axis=1),
         jnp.concatenate([b, jnp.where(incl[d], x[n2:, n2:], 0.0)], axis=1)], axis=0)
        for (d, p), x, a, b in zip(probs, big, kt, rt)]
    xy = [_dot_nt(a, jnp.concatenate([x, v], axis=1)) for a, x, v in zip(lhs, state, vs_t)]
    u_m = [-_dot(x, y[:n2]) for x, y in zip(tinv, xy)]
    if need_y:
        for (d, p), x, u, y in zip(probs, big, u_m, xy):
            yy = y[n2:] + _dot(jnp.where(incl[d], x[n2:, :n2], 0.0), u)
            y_refs[d][:, p * LANE:(p + 1) * LANE] = yy[:C] + yy[C:]
    for (d, p), x, a, u, v, k2, a2 in zip(probs, state, tot, u_m, vs_t, kb2, at2):
        st[d, p] = x * jnp.exp(a) + _dot(jnp.concatenate([u.T, v], axis=1), jnp.concatenate([a2, k2], axis=0))
    sfin_refs[0][...] = st[0]
    sfin_refs[1][...] = st[1]


def _rwkv_scan(packed, nseq, seqlen, s0f, s0b, need_y):
    C = RWKV_CHUNK
    nc = seqlen // C
    fwd, bwd = (lambda b, c: (b * nc + c, 0)), (lambda b, c: (b * nc + nc - 1 - c, 0))
    tok_f, tok_b = pl.BlockSpec((C, RWKV_DIM), fwd), pl.BlockSpec((C, RWKV_DIM), bwd)
    pk_f, pk_b = pl.BlockSpec((C, RP_N * RWKV_DIM), fwd), pl.BlockSpec((C, RP_N * RWKV_DIM), bwd)
    sshape = (RWKV_PAIRS, LANE, LANE)
    sspec = pl.BlockSpec((None,) + sshape, lambda b, c: (b, 0, 0, 0))
    sstruct = jax.ShapeDtypeStruct((nseq,) + sshape, F32)
    out_specs, out_shape = [sspec, sspec], [sstruct, sstruct]
    if need_y:
        ystruct = jax.ShapeDtypeStruct((nseq * seqlen, RWKV_DIM), F32)
        out_specs, out_shape = [tok_f, tok_b] + out_specs, [ystruct, ystruct] + out_shape
    res = pl.pallas_call(
        functools.partial(_rwkv_scan_kernel, C=C, need_y=need_y),
        grid=(nseq, nc),
        in_specs=[pk_f, pk_b, sspec, sspec],
        out_specs=out_specs, out_shape=out_shape,
        scratch_shapes=[pltpu.VMEM((2,) + sshape, F32)],
        compiler_params=_params("parallel", "arbitrary"),
    )(packed, packed, s0f, s0b)
    return tuple(res) if need_y else (None, None) + tuple(res)


def _rwkv_fin_kernel(yf_ref, yb_ref, bonus_ref, g_ref, lg_ref, lb_ref, bd_ref, o_ref):
    y = yf_ref[...] + yb_ref[...]
    bd = bd_ref[...]
    dv = y - _blocksum(y, bd) * (1.0 / RWKV_HEAD_DIM)
    var = _blocksum(dv * dv, bd) * (1.0 / RWKV_HEAD_DIM)
    yn = dv * lax.rsqrt(var + GN_EPS) * lg_ref[...] + lb_ref[...]
    o_ref[...] = ((yn + bonus_ref[...]) * g_ref[...]).astype(o_ref.dtype)


def _rwkv_finish(yf, yb, bonus, g, ln_g, ln_b, bd):
    nrows = yf.shape[0]
    tq = _tile(nrows, 512)
    tok = pl.BlockSpec((tq, RWKV_DIM), lambda i: (i, 0))
    return pl.pallas_call(
        _rwkv_fin_kernel,
        grid=(nrows // tq,),
        in_specs=[tok] * 4 + [_const_spec((1, RWKV_DIM))] * 2 + [_const_spec(bd.shape)],
        out_specs=tok,
        out_shape=jax.ShapeDtypeStruct((nrows, RWKV_DIM), BF16),
        compiler_params=_params("parallel"),
    )(yf, yb, bonus, g, ln_g, ln_b, bd)


def _to_cols(t, b, rows):
    return t.reshape(b, rows, GRID_W, -1).transpose(0, 2, 1, 3).reshape(b * rows * GRID_W, -1)


def _to_rows(t, b, rows):
    return t.reshape(b, GRID_W, rows, -1).transpose(0, 2, 1, 3).reshape(b * rows * GRID_W, -1)


def _pad_cols(w, n):
    return jnp.pad(w, [(0, 0)] * (w.ndim - 1) + [(0, n - w.shape[-1])])


def _pad_rows(w, n):
    return jnp.pad(w, [(0, 0)] * (w.ndim - 2) + [(0, n - w.shape[-2]), (0, 0)])


def kernel(x, c, ctx, c_ctx, mod_w, mod_b, norm_g, w_in, conv_w, conv_b, conv_ln_g, conv_ln_b, conv_out, ssd_conv_w, ssd_conv_b, ssd_A_log, ssd_dt_bias, ssd_D, ssd_norm_g, ssd_out, fourier_out, rwkv_mu, rwkv_w0, rwkv_w2, rwkv_a0, rwkv_a2, rwkv_g2, rwkv_k_k, rwkv_k_a, rwkv_r_k, rwkv_ln_g, rwkv_ln_b, rwkv_out, w_o, mlp_up, mlp_down):
    B, L, D = x.shape
    CL = ctx.shape[1]
    depth = mod_w.shape[0]
    rows = L // GRID_W
    n_lat, n_ctx = B * L, B * CL
    n_all = n_lat + n_ctx
    tm = _tile(math.gcd(L, n_ctx), 1024)
    tm_small = _tile(tm, 512)
    tm_merge = _tile(tm, 256)
    mod_idx = lambda t: (lambda i: jnp.minimum((i * t) // L, B))

    w_u, w_rw, w_gate = _split_w_in(w_in)
    down_w, wo_w = mlp_down.astype(BF16), w_o.astype(BF16)
    branch_w = [conv_out.astype(BF16), ssd_out.astype(BF16), fourier_out.astype(BF16), rwkv_out.astype(BF16)]
    mu = rwkv_mu
    mu_cols = jnp.concatenate(
        [mu[:, :3 * RWKV_DIM]] + [_pad_cols(mu[:, a:b], LANE) for a, b in ((1536, 1600), (1600, 1664), (1664, 1728))]
        + [mu[:, 1728:]], axis=1)
    head_of = jnp.arange(RWKV_DIM, dtype=jnp.int32) // RWKV_HEAD_DIM
    bd = (head_of[:, None] == head_of[None, :]).astype(BF16)
    gidx = jnp.arange(FOURIER_DIM, dtype=jnp.int32)
    cc, sc_ = _dft_mats(FOURIER_GROUP_DIM, FOURIER_GROUP_DIM ** -0.5)
    same_group = (gidx[:, None] // FOURIER_GROUP_DIM) == (gidx[None, :] // FOURIER_GROUP_DIM)
    tile_g = lambda m: jnp.where(same_group, jnp.tile(m, (FOURIER_GROUPS, FOURIER_GROUPS)), 0.0)
    chan_w = jnp.concatenate([tile_g(cc), -tile_g(sc_)], axis=1).astype(BF16)
    conv_w_p = _pad_rows(conv_w, 32)
    ssd_conv_w_p = _pad_rows(ssd_conv_w, SUBLANE)
    dt_bias_p = _pad_cols(ssd_dt_bias.reshape(depth, 1, 2 * SSD_HEADS), U_DT_W)
    a_log_p = _pad_cols(ssd_A_log.reshape(depth, 1, 2 * SSD_HEADS), U_DT_W)
    d_skip = jnp.repeat(ssd_D, SSD_HEAD_DIM, axis=1).reshape(depth, 1, SSD_DIM)

    cond = jnp.concatenate([c, c_ctx[None, :], jnp.zeros((SUBLANE - (B + 1) % SUBLANE, D), F32)], axis=0)
    mod = _mod_all(cond, mod_w, mod_b)

    xs = jnp.concatenate([x.reshape(n_lat, D), ctx.reshape(n_ctx, D)], axis=0)
    ssd_zero = jnp.zeros((B, SSD_GROUPS, SSD_STATE, SSD_HEADS_PER_GROUP * SSD_HEAD_DIM), F32)
    rwkv_zero = jnp.zeros((B, RWKV_PAIRS, LANE, LANE), F32)

    mods = lambda l: tuple(mod[l, :, j * D:(j + 1) * D].reshape(-1, 1, D) for j in range(6))
    h = _norm_mod(xs, norm_g[0, 0].reshape(1, D), *mods(0)[1::-1], n_all, mod_idx(tm), tm)
    for i in range(depth):
        last = i == depth - 1
        m_out = n_lat if last else n_all
        sh1, sc1, g1, sh2, sc2, g2 = mods(i)
        ng = norm_g[i].reshape(4, 1, D)

        u = _matmul(h, w_u, n_all, tm, F32, tn_cap=1536, layer=i)
        gates = _matmul(h, w_gate, m_out, tm, BF16, epilogue="sigmoid", tn_cap=2048, layer=i)

        a_conv = _conformer(u, 0, n_lat, GRID_W, conv_w_p[i], conv_b[i][None], conv_ln_g[i][None], conv_ln_b[i][None])
        z = _matmul(u, chan_w, m_out, tm, BF16, a_col=U_FFT // U_FFT_W)
        a_fft = _fourier(z, 0, B, L)
        if not last:
            a_conv = jnp.concatenate(
                [a_conv, _conformer(u, n_lat, n_ctx, CL, conv_w_p[i], conv_b[i][None], conv_ln_g[i][None],
                                    conv_ln_b[i][None])], axis=0)
            a_fft = jnp.concatenate([a_fft, _fourier(z, n_lat, B, CL)], axis=0)

        ssd_args = (dt_bias_p[i], a_log_p[i])
        xs_c, bm_c, cm_c = _ssd_conv(u, n_lat, B, CL, ssd_conv_w_p[i], ssd_conv_b[i][None])
        yfc, sf = _ssd_scan(xs_c, bm_c, cm_c, u, n_lat, B, CL, *ssd_args, ssd_zero, False, not last)
        ybc, sb = _ssd_scan(xs_c, bm_c, cm_c, u, n_lat, B, CL, *ssd_args, ssd_zero, True, not last)
        xs_l, bm_l, cm_l = _ssd_conv(u, 0, B, L, ssd_conv_w_p[i], ssd_conv_b[i][None])
        yfl, _ = _ssd_scan(xs_l, bm_l, cm_l, u, 0, B, L, *ssd_args, sf, False, True)
        ybl, _ = _ssd_scan(xs_l, bm_l, cm_l, u, 0, B, L, *ssd_args, sb, True, True)
        a_ssd = _ssd_finish(yfl, ybl, xs_l, u, 0, d_skip[i], ssd_norm_g[i][None])
        if not last:
            a_ssd = jnp.concatenate(
                [a_ssd, _ssd_finish(yfc, ybc, xs_c, u, n_lat, d_skip[i], ssd_norm_g[i][None])], axis=0)

        rp = dict(mu=mu_cols[i][None], w0=rwkv_w0[i], w2=_pad_rows(rwkv_w2[i], LANE).astype(BF16),
                  a0=rwkv_a0[i][None], a2=_pad_rows(rwkv_a2[i], LANE).astype(BF16), g2=rwkv_g2[i].astype(BF16),
                  k_k=rwkv_k_k[i][None], k_a=rwkv_k_a[i][None], r_k=rwkv_r_k[i].reshape(1, RWKV_DIM), bd=bd)
        u_rc = _matmul(h, w_rw, n_ctx, tm, F32, a_row=n_lat // tm, layer=i)
        packed, g_, bonus = _rwkv_prep(u_rc, 0, B, CL, rp)
        yfc, ybc, s_f, s_b = _rwkv_scan(packed, B, CL, rwkv_zero, rwkv_zero, not last)
        if not last:
            a_rwkv_c = _rwkv_finish(yfc, ybc, bonus, g_, rwkv_ln_g[i][None], rwkv_ln_b[i][None], bd)
        u_rl = _matmul(_to_cols(h[:n_lat], B, rows), w_rw, n_lat, tm, F32, layer=i)
        packed, g_, bonus = _rwkv_prep(u_rl, 0, B, L, rp)
        yfl, ybl, _, _ = _rwkv_scan(packed, B, L, s_f, s_b, True)
        a_rwkv = _to_rows(_rwkv_finish(yfl, ybl, bonus, g_, rwkv_ln_g[i][None], rwkv_ln_b[i][None], bd), B, rows)
        if not last:
            a_rwkv = jnp.concatenate([a_rwkv, a_rwkv_c], axis=0)

        m = _merge(gates, [a_conv, a_ssd, a_fft, a_rwkv], branch_w, m_out, tm_merge, i)
        xs, h2 = _matmul_residual(m, wo_w, xs, g1, ng[1], m_out, mod_idx(tm_small), tm_small,
                                  nxt=(ng[2], sc2, sh2), layer=i)

        mid = _matmul_f32w(h2, mlp_up, m_out, tm, BF16, epilogue="relu2", layer=i)
        if last:
            xs = _matmul_residual(mid, down_w, xs, g2, ng[3], m_out, mod_idx(tm_small), tm_small, layer=i)
        else:
            xs, h = _matmul_residual(mid, down_w, xs, g2, ng[3], m_out, mod_idx(tm_small), tm_small,
                                     nxt=(norm_g[i + 1, 0].reshape(1, D), *mods(i + 1)[1::-1]), layer=i)

    return xs[:n_lat].reshape(B, L, D)
```

```python
import functools
import math

import jax
import jax.numpy as jnp
import numpy as np
from jax import lax
from jax.experimental import pallas as pl
from jax.experimental.pallas import tpu as pltpu

F32 = jnp.float32
BF16 = jnp.bfloat16

GRID_W = 64
RMS_EPS = 1e-6
LN_EPS = 1e-5
GN_EPS = 64e-5
CONV_DIM = 512
CONV_WIDTH = 31
SSD_HEADS = 12
SSD_HEAD_DIM = 64
SSD_DIM = SSD_HEADS * SSD_HEAD_DIM
SSD_GROUPS = 4
SSD_HEADS_PER_GROUP = SSD_HEADS // SSD_GROUPS
SSD_STATE = 128
SSD_CONV = 5
SSD_BC = SSD_GROUPS * SSD_STATE
SSD_XBC = SSD_DIM + 2 * SSD_BC
SSD_IN = SSD_DIM + SSD_XBC + 2 * SSD_HEADS
SSD_CHUNK = 128
FOURIER_GROUPS = 4
FOURIER_GROUP_DIM = 128
FOURIER_DIM = FOURIER_GROUPS * FOURIER_GROUP_DIM
RWKV_HEADS = 8
RWKV_HEAD_DIM = 64
RWKV_DIM = RWKV_HEADS * RWKV_HEAD_DIM
RWKV_PAIRS = RWKV_HEADS // 2
DECAY_LORA = 64
ICL_LORA = 64
GATE_LORA = 128
RWKV_IN = 3 * RWKV_DIM + 2 * DECAY_LORA + ICL_LORA + GATE_LORA
RWKV_CHUNK = 64
REC_IN = SSD_IN + RWKV_IN
N_BRANCH = 4

LANE = 128
SUBLANE = 8
VMEM_LIMIT = 56 * 1024 * 1024

U_XBC, U_XBC_W = 0, SSD_XBC
U_DT, U_DT_W = 1792, LANE
U_CONV, U_CONV_W = 2048, 2 * CONV_DIM
U_Z, U_Z_W = 3072, SSD_DIM
U_FFT, U_FFT_W = 4096, FOURIER_DIM
U_N = 4608
U_RWKV_W = 2048
RW_WF, RW_WB, RW_AL, RW_GL = 1536, 1664, 1792, 1920
RP_R, RP_LWF, RP_LWB, RP_K, RP_V, RP_KK, RP_KA, RP_N = 0, 1, 2, 3, 4, 5, 6, 7


def _params(*sem):
    return pltpu.CompilerParams(dimension_semantics=sem, vmem_limit_bytes=VMEM_LIMIT)


def _tile(n, cap, quantum=SUBLANE):
    if n <= cap:
        return n
    t = (cap // quantum) * quantum
    while t >= quantum:
        if n % t == 0:
            return t
        t -= quantum
    raise ValueError(f"no tile for {n} under {cap}")


def _const_spec(shape):
    nd = len(shape)
    return pl.BlockSpec(shape, lambda *_: (0,) * nd)


def _dot(a, b):
    return jnp.dot(a.astype(BF16), b.astype(BF16), preferred_element_type=F32)


def _dot_nt(a, b):
    return lax.dot_general(a.astype(BF16), b.astype(BF16), (((1,), (1,)), ((), ())),
                           preferred_element_type=F32)


def _split3(x):
    h1 = x.astype(BF16)
    r1 = x - h1.astype(F32)
    h2 = r1.astype(BF16)
    h3 = (r1 - h2.astype(F32)).astype(BF16)
    return h1, h2, h3


def _dot_exact_lhs(m, x):
    h1, h2, h3 = _split3(x)
    d = lambda h: jnp.dot(m, h, preferred_element_type=F32)
    return d(h1) + d(h2) + d(h3)


def _blocksum(x, bd):
    hi = x.astype(BF16)
    lo = (x - hi.astype(F32)).astype(BF16)
    return (jnp.dot(hi, bd, preferred_element_type=F32) + jnp.dot(lo, bd, preferred_element_type=F32))


def _sigmoid(x):
    return 1.0 / (1.0 + jnp.exp(-x))


def _softplus(x):
    return jnp.maximum(x, 0.0) + jnp.log1p(jnp.exp(-jnp.abs(x)))


def _rms(y, g):
    return y * lax.rsqrt(jnp.mean(y * y, axis=-1, keepdims=True) + RMS_EPS) * g


_O_RWKV, _O_CONV = SSD_IN, REC_IN
_O_FFT, _O_GATE = REC_IN + 2 * CONV_DIM, REC_IN + 2 * CONV_DIM + FOURIER_DIM
_W_IN_PIECES = (
    ("rw", 0, _O_RWKV, 3 * RWKV_DIM), ("rw", RW_WF, _O_RWKV + 1536, DECAY_LORA),
    ("rw", RW_WB, _O_RWKV + 1600, DECAY_LORA), ("rw", RW_AL, _O_RWKV + 1664, ICL_LORA),
    ("rw", RW_GL, _O_RWKV + 1728, GATE_LORA),
    ("u", U_XBC, SSD_DIM, SSD_XBC), ("u", U_DT, SSD_DIM + SSD_XBC, 2 * SSD_HEADS),
    ("u", U_CONV, _O_CONV, 2 * CONV_DIM), ("u", U_Z, 0, SSD_DIM), ("u", U_FFT, _O_FFT, FOURIER_DIM),
)
_RELAYOUT_CHUNK = 1024


def _split_w_in_kernel(w_ref, u_ref, rw_ref, g_ref):
    n_in = w_ref.shape[-1]
    dst = {"u": u_ref, "rw": rw_ref, "g": g_ref}
    u_ref[...] = jnp.zeros_like(u_ref)
    rw_ref[...] = jnp.zeros_like(rw_ref)
    pieces = _W_IN_PIECES + (("g", 0, _O_GATE, g_ref.shape[-1]),)
    for name, dcol, scol, width in pieces:
        for off in range(0, width, _RELAYOUT_CHUNK):
            w = min(_RELAYOUT_CHUNK, width - off)
            lo = scol + off
            lo_al = (lo // LANE) * LANE
            hi_al = min(-(-(lo + w) // LANE) * LANE, n_in)
            x = w_ref[:, lo_al:hi_al]
            dst[name][:, dcol + off:dcol + off + w] = x[:, lo - lo_al:lo - lo_al + w].astype(BF16)


def _split_w_in(w_in):
    depth, d, n_in = w_in.shape
    n_gate = n_in - _O_GATE
    tk = _tile(d, 128)
    outs = (U_N, U_RWKV_W, n_gate)
    return pl.pallas_call(
        _split_w_in_kernel,
        grid=(depth, d // tk),
        in_specs=[pl.BlockSpec((None, tk, n_in), lambda l, i: (l, i, 0))],
        out_specs=[pl.BlockSpec((None, tk, n), lambda l, i: (l, i, 0)) for n in outs],
        out_shape=[jax.ShapeDtypeStruct((depth, d, n), BF16) for n in outs],
        compiler_params=_params("parallel", "parallel"),
    )(w_in)


def _mod_kernel(a_ref, w_ref, b_ref, o_ref):
    a = a_ref[...]
    a = a * _sigmoid(a)
    o_ref[...] = _dot(a, w_ref[...]) + b_ref[...]


def _mod_all(cond, mod_w, mod_b):
    depth, d, n = mod_w.shape
    tn = _tile(n, 1024, LANE)
    return pl.pallas_call(
        _mod_kernel,
        grid=(depth, n // tn),
        in_specs=[_const_spec(cond.shape),
                  pl.BlockSpec((None, d, tn), lambda l, j: (l, 0, j)),
                  pl.BlockSpec((None, 1, tn), lambda l, j: (l, 0, j))],
        out_specs=pl.BlockSpec((None, cond.shape[0], tn), lambda l, j: (l, 0, j)),
        out_shape=jax.ShapeDtypeStruct((depth, cond.shape[0], n), F32),
        compiler_params=_params("parallel", "parallel"),
    )(cond, mod_w, mod_b.reshape(depth, 1, n))


def _norm_mod_kernel(x_ref, g_ref, sc_ref, sh_ref, o_ref):
    o_ref[...] = (_rms(x_ref[...], g_ref[...]) * (1.0 + sc_ref[...]) + sh_ref[...]).astype(o_ref.dtype)


def _norm_mod(x, g, sc, sh, m_rows, mod_idx, tm):
    d = x.shape[1]
    return pl.pallas_call(
        _norm_mod_kernel,
        grid=(m_rows // tm,),
        in_specs=[pl.BlockSpec((tm, d), lambda i: (i, 0)),
                  _const_spec((1, d)),
                  pl.BlockSpec((None, 1, d), lambda i: (mod_idx(i), 0, 0)),
                  pl.BlockSpec((None, 1, d), lambda i: (mod_idx(i), 0, 0))],
        out_specs=pl.BlockSpec((tm, d), lambda i: (i, 0)),
        out_shape=jax.ShapeDtypeStruct((m_rows, d), BF16),
        compiler_params=_params("parallel"),
    )(x, g, sc, sh)


def _mm_kernel(a_ref, w_ref, o_ref, *, epilogue):
    acc = _dot(a_ref[...], w_ref[...])
    if epilogue == "relu2":
        acc = jnp.square(jnp.maximum(acc, 0.0))
    elif epilogue == "sigmoid":
        acc = _sigmoid(acc)
    o_ref[...] = acc.astype(o_ref.dtype)


def _layer_spec(w, layer, block, index):
    if w.ndim == 2:
        return pl.BlockSpec(block, index)
    return pl.BlockSpec((None,) + block, lambda *ids: (layer,) + index(*ids))


def _matmul(a, w, m_rows, tm, out_dtype, epilogue=None, a_col=0, a_row=0, tn_cap=1024, layer=0):
    k, n = w.shape[-2:]
    tn = _tile(n, tn_cap, LANE)
    return pl.pallas_call(
        functools.partial(_mm_kernel, epilogue=epilogue),
        grid=(m_rows // tm, n // tn),
        in_specs=[pl.BlockSpec((tm, k), lambda i, j: (a_row + i, a_col)),
                  _layer_spec(w, layer, (k, tn), lambda i, j: (0, j))],
        out_specs=pl.BlockSpec((tm, tn), lambda i, j: (i, j)),
        out_shape=jax.ShapeDtypeStruct((m_rows, n), out_dtype),
        compiler_params=_params("parallel", "arbitrary"),
    )(a, w)


def _mm_wcast_kernel(a_ref, w_ref, o_ref, wb_ref, *, epilogue):
    @pl.when(pl.program_id(1) == 0)
    def _():
        wb_ref[...] = w_ref[...].astype(wb_ref.dtype)

    _mm_kernel(a_ref, wb_ref, o_ref, epilogue=epilogue)


def _matmul_f32w(a, w, m_rows, tm, out_dtype, epilogue=None, tn_cap=1024, layer=0):
    k, n = w.shape[-2:]
    tn = _tile(n, tn_cap, LANE)
    return pl.pallas_call(
        functools.partial(_mm_wcast_kernel, epilogue=epilogue),
        grid=(n // tn, m_rows // tm),
        in_specs=[pl.BlockSpec((tm, k), lambda j, i: (i, 0)),
                  _layer_spec(w, layer, (k, tn), lambda j, i: (0, j))],
        out_specs=pl.BlockSpec((tm, tn), lambda j, i: (i, j)),
        out_shape=jax.ShapeDtypeStruct((m_rows, n), out_dtype),
        scratch_shapes=[pltpu.VMEM((k, tn), BF16)],
        compiler_params=_params("parallel", "arbitrary"),
    )(a, w)


def _mm_res_kernel(a_ref, w_ref, x_ref, gate_ref, g_ref, *rest, with_next, nk):
    if with_next:
        g2_ref, sc_ref, sh_ref, o_ref, h_ref = rest
    else:
        (o_ref,) = rest
    def finish(y):
        xn = x_ref[...] + gate_ref[...] * _rms(y, g_ref[...])
        o_ref[...] = xn
        if with_next:
            h_ref[...] = (_rms(xn, g2_ref[...]) * (1.0 + sc_ref[...]) + sh_ref[...]).astype(h_ref.dtype)

    if nk == 1:
        finish(_dot(a_ref[...], w_ref[...]))
        return
    k = pl.program_id(1)

    @pl.when(k == 0)
    def _():
        o_ref[...] = jnp.zeros_like(o_ref)

    o_ref[...] += _dot(a_ref[...], w_ref[...])

    @pl.when(k == nk - 1)
    def _():
        finish(o_ref[...])


def _matmul_residual(a, w, x, gate, g, m_rows, mod_idx, tm, nxt=None, tk_cap=2048, layer=0):
    k, d = w.shape[-2:]
    tk = _tile(k, tk_cap, LANE)
    mod_spec = pl.BlockSpec((None, 1, d), lambda i, kk: (mod_idx(i), 0, 0))
    tile_spec = pl.BlockSpec((tm, d), lambda i, kk: (i, 0))
    in_specs = [pl.BlockSpec((tm, tk), lambda i, kk: (i, kk)), _layer_spec(w, layer, (tk, d), lambda i, kk: (kk, 0)),
                tile_spec, mod_spec, _const_spec((1, d))]
    out_specs, out_shape, args = tile_spec, jax.ShapeDtypeStruct((m_rows, d), F32), (a, w, x, gate, g)
    if nxt is not None:
        in_specs += [_const_spec((1, d)), mod_spec, mod_spec]
        out_specs, out_shape = [tile_spec, tile_spec], [out_shape, jax.ShapeDtypeStruct((m_rows, d), BF16)]
        args += tuple(nxt)
    return pl.pallas_call(
        functools.partial(_mm_res_kernel, with_next=nxt is not None, nk=k // tk),
        grid=(m_rows // tm, k // tk),
        in_specs=in_specs, out_specs=out_specs, out_shape=out_shape,
        compiler_params=_params("parallel", "arbitrary"),
    )(*args)


def _merge_kernel(gates_ref, a0, a1, a2, a3, w0, w1, w2, w3, o_ref):
    d = o_ref.shape[-1]
    m = None
    for i, (a, w) in enumerate(((a0, w0), (a1, w1), (a2, w2), (a3, w3))):
        t = gates_ref[:, i * d:(i + 1) * d].astype(F32) * _dot(a[...], w[...])
        m = t if m is None else m + t
    o_ref[...] = m.astype(o_ref.dtype)


def _merge(gates, acts, weights, m_rows, tm, layer):
    d = weights[0].shape[-1]
    return pl.pallas_call(
        _merge_kernel,
        grid=(m_rows // tm,),
        in_specs=[pl.BlockSpec((tm, N_BRANCH * d), lambda i: (i, 0))]
        + [pl.BlockSpec((tm, a.shape[1]), lambda i: (i, 0)) for a in acts]
        + [_layer_spec(w, layer, w.shape[-2:], lambda i: (0, 0)) for w in weights],
        out_specs=pl.BlockSpec((tm, d), lambda i: (i, 0)),
        out_shape=jax.ShapeDtypeStruct((m_rows, d), BF16),
        compiler_params=_params("parallel"),
    )(gates, *acts, *weights)


def _conformer_kernel(u_ref, w_ref, b_ref, lg_ref, lb_ref, o_ref, sh_ref, *, seg, nseg):
    pad = 16
    first = pad - (CONV_WIDTH - 1) // 2
    span = seg + 2 * pad - SUBLANE
    zeros = jnp.zeros((pad, CONV_DIM), F32)
    sh_ref[0, 0:pad, :] = zeros
    sh_ref[0, pad + seg:2 * pad + seg, :] = zeros

    def body(s, carry):
        r0 = pl.multiple_of(s * seg, seg)
        u = u_ref[pl.ds(r0, seg), :]
        sh_ref[0, pad:pad + seg, :] = u[:, :CONV_DIM] * _sigmoid(u[:, CONV_DIM:])
        for r in range(1, SUBLANE):
            sh_ref[r, 0:span, :] = sh_ref[0, pl.ds(r, span), :]
        acc = jnp.zeros((seg, CONV_DIM), F32) + b_ref[...]
        for j in range(CONV_WIDTH):
            start = first + j
            acc = acc + w_ref[j:j + 1, :] * sh_ref[start % SUBLANE, pl.ds(start - start % SUBLANE, seg), :]
        mu = jnp.mean(acc, axis=-1, keepdims=True)
        dv = acc - mu
        var = jnp.mean(dv * dv, axis=-1, keepdims=True)
        v = dv * lax.rsqrt(var + LN_EPS) * lg_ref[...] + lb_ref[...]
        o_ref[pl.ds(r0, seg), :] = (v * _sigmoid(v)).astype(o_ref.dtype)
        return carry

    lax.fori_loop(0, nseg, body, 0)


def _conformer(u, row0, nrows, seg, conv_w, conv_b, ln_g, ln_b):
    tq = max(seg, min(256, nrows))
    blk0 = row0 // tq
    return pl.pallas_call(
        functools.partial(_conformer_kernel, seg=seg, nseg=tq // seg),
        grid=(nrows // tq,),
        in_specs=[pl.BlockSpec((tq, U_CONV_W), lambda i: (blk0 + i, U_CONV // U_CONV_W)),
                  _const_spec(conv_w.shape), _const_spec((1, CONV_DIM)),
                  _const_spec((1, CONV_DIM)), _const_spec((1, CONV_DIM))],
        out_specs=pl.BlockSpec((tq, CONV_DIM), lambda i: (i, 0)),
        out_shape=jax.ShapeDtypeStruct((nrows, CONV_DIM), BF16),
        scratch_shapes=[pltpu.VMEM((SUBLANE, seg + 32, CONV_DIM), F32)],
        compiler_params=_params("parallel"),
    )(u, conv_w, conv_b, ln_g, ln_b)


def _dft_kernel(c_ref, s_ref, z_ref, o_ref):
    o_ref[...] = (jnp.dot(c_ref[...], z_ref[:, :FOURIER_DIM], preferred_element_type=F32)
                  + jnp.dot(s_ref[...], z_ref[:, FOURIER_DIM:], preferred_element_type=F32)).astype(o_ref.dtype)


def _dft_mats(n, scale):
    idx = np.arange(n, dtype=np.int64)
    ang = ((idx[:, None] * idx[None, :]) % n).astype(np.float64) * (2.0 * math.pi / n)
    return (jnp.asarray((np.cos(ang) * scale).astype(np.float32)),
            jnp.asarray((np.sin(ang) * scale).astype(np.float32)))


def _fourier(z, row0, nseq, seqlen):
    nrows = nseq * seqlen
    cl, sl = _dft_mats(seqlen, seqlen ** -0.5)
    cl, sl = cl.astype(BF16), sl.astype(BF16)
    tm = _tile(seqlen, 512)
    nt = seqlen // tm
    sblk0 = row0 // seqlen
    return pl.pallas_call(
        _dft_kernel,
        grid=(nt, nseq),
        in_specs=[pl.BlockSpec((tm, seqlen), lambda i, b: (i, 0)),
                  pl.BlockSpec((tm, seqlen), lambda i, b: (i, 0)),
                  pl.BlockSpec((seqlen, 2 * FOURIER_DIM), lambda i, b: (sblk0 + b, 0))],
        out_specs=pl.BlockSpec((tm, FOURIER_DIM), lambda i, b: (b * nt + i, 0)),
        out_shape=jax.ShapeDtypeStruct((nrows, FOURIER_DIM), BF16),
        compiler_params=_params("parallel", "arbitrary"),
    )(cl, sl, z)


def _halo_specs(tq, width, blk0, col_blk, nblk_total):
    hb = tq // SUBLANE
    cur = pl.BlockSpec((tq, width), lambda i: (blk0 + i, col_blk))
    prev = pl.BlockSpec((SUBLANE, width), lambda i: (jnp.maximum((blk0 + i) * hb - 1, 0), col_blk))
    nxt = pl.BlockSpec((SUBLANE, width),
                       lambda i: (jnp.minimum((blk0 + i + 1) * hb, nblk_total - 1), col_blk))
    return cur, prev, nxt


def _fill_halo(scr, cur_ref, prev_ref, next_ref, tq, tiles_per_seq):
    i = pl.program_id(0)
    first = (i % tiles_per_seq) == 0
    last = (i % tiles_per_seq) == tiles_per_seq - 1
    scr[0:SUBLANE, :] = jnp.where(first, 0.0, prev_ref[...])
    scr[SUBLANE:SUBLANE + tq, :] = cur_ref[...]
    scr[SUBLANE + tq:2 * SUBLANE + tq, :] = jnp.where(last, 0.0, next_ref[...])


def _ssd_conv_kernel(cur_ref, prev_ref, next_ref, w_ref, b_ref, xs_ref, bm_ref, cm_ref, scr, *, tq, tiles_per_seq):
    _fill_halo(scr, cur_ref, prev_ref, next_ref, tq, tiles_per_seq)
    half = (SSD_CONV - 1) // 2
    cw = 256
    for c0 in range(0, SSD_XBC, cw):
        acc = jnp.zeros((tq, cw), F32) + b_ref[:, c0:c0 + cw]
        for j in range(SSD_CONV):
            acc = acc + w_ref[j:j + 1, c0:c0 + cw] * scr[pl.ds(SUBLANE - half + j, tq), c0:c0 + cw]
        y = acc * _sigmoid(acc)
        if c0 < SSD_DIM:
            xs_ref[:, c0:c0 + cw] = y
        elif c0 < SSD_DIM + SSD_BC:
            bm_ref[:, c0 - SSD_DIM:c0 - SSD_DIM + cw] = y
        else:
            cm_ref[:, c0 - SSD_DIM - SSD_BC:c0 - SSD_DIM - SSD_BC + cw] = y


def _ssd_conv(u, row0, nseq, seqlen, conv_w, conv_b):
    nrows = nseq * seqlen
    tq = _tile(seqlen, 256)
    cur, prev, nxt = _halo_specs(tq, U_XBC_W, row0 // tq, U_XBC // U_XBC_W, u.shape[0] // SUBLANE)
    outs = [jax.ShapeDtypeStruct((nrows, w), F32) for w in (SSD_DIM, SSD_BC, SSD_BC)]
    return pl.pallas_call(
        functools.partial(_ssd_conv_kernel, tq=tq, tiles_per_seq=seqlen // tq),
        grid=(nrows // tq,),
        in_specs=[cur, prev, nxt, _const_spec(conv_w.shape), _const_spec((1, SSD_XBC))],
        out_specs=[pl.BlockSpec((tq, w), lambda i: (i, 0)) for w in (SSD_DIM, SSD_BC, SSD_BC)],
        out_shape=outs,
        scratch_shapes=[pltpu.VMEM((tq + 2 * SUBLANE, U_XBC_W), F32)],
        compiler_params=_params("parallel"),
    )(u, u, u, conv_w, conv_b)


def _ssd_scan_kernel(xs_ref, b_ref, c_ref, dt_ref, bias_ref, alog_ref, sel_ref, s0_ref, *rest,
                     q, sub, rev, direction, need_y):
    if need_y:
        y_ref, sfin_ref, st = rest
    else:
        sfin_ref, st = rest

    @pl.when(pl.program_id(1) == 0)
    def _():
        st[...] = s0_ref[...]

    hp, gw = SSD_HEAD_DIM, SSD_HEADS_PER_GROUP * SSD_HEAD_DIM
    ti = lax.broadcasted_iota(jnp.int32, (q, q), 0)
    tk = lax.broadcasted_iota(jnp.int32, (q, q), 1)
    before = (tk >= ti) if rev else (tk <= ti)
    tri = jnp.where(before, 1.0, 0.0).astype(BF16)
    neg_a = -jnp.exp(alog_ref[...])
    sel = sel_ref[...]
    spread = lambda parts: sum(jnp.dot(p, sel, preferred_element_type=F32) for p in parts)
    lane0 = lax.broadcasted_iota(jnp.int32, (q, LANE), 1) < hp

    for step in range(sub):
        rows = slice((sub - 1 - step) * q if rev else step * q, (sub - step) * q if rev else (step + 1) * q)
        dt = _softplus(dt_ref[rows, :] + bias_ref[...])
        cs = _dot_exact_lhs(tri, dt * neg_a)
        dt_b = spread(_split3(dt)[:2])
        cs_b = spread(_split3(cs))
        tot_b = cs_b[0:1, :] if rev else cs_b[q - 1:q, :]
        xd = xs_ref[rows, :] * dt_b
        xdd = xd * jnp.exp(tot_b - cs_b)
        dec = jnp.exp(tot_b)
        if need_y:
            ecs = jnp.exp(cs_b)
            cs_t = cs.T
        for g in range(SSD_GROUPS):
            cg = c_ref[rows, g * SSD_STATE:(g + 1) * SSD_STATE]
            bg = b_ref[rows, g * SSD_STATE:(g + 1) * SSD_STATE]
            state = st[g]
            if need_y:
                cb = _dot_nt(cg, bg)
                y_off = _dot(cg, state) * ecs[:, g * gw:(g + 1) * gw]
                for j in range(SSD_HEADS_PER_GROUP):
                    h = g * SSD_HEADS_PER_GROUP + j
                    col = direction * SSD_HEADS + h
                    pair = cs_b[:, (h // 2) * LANE:(h // 2 + 1) * LANE]
                    swapped = pltpu.roll(pair, hp, axis=1)
                    cs_col = jnp.where(lane0, pair, swapped) if h % 2 == 0 else jnp.where(lane0, swapped, pair)
                    lm = jnp.where(before, jnp.exp(jnp.minimum(cs_col - cs_t[col:col + 1, :], 0.0)), 0.0)
                    y_ref[rows, h * hp:(h + 1) * hp] = (_dot(cb * lm, xd[:, h * hp:(h + 1) * hp])
                                                        + y_off[:, j * hp:(j + 1) * hp])
            st[g] = state * dec[:, g * gw:(g + 1) * gw] + _dot(bg.T, xdd[:, g * gw:(g + 1) * gw])
    sfin_ref[...] = st[...]


def _ssd_scan(xs, bm, cm, u, row0, nseq, seqlen, dt_bias, a_log, s0, rev, need_y):
    q = SSD_CHUNK
    sub = max(s for s in (4, 2, 1) if seqlen % (s * q) == 0)
    bq = sub * q
    nc = seqlen // bq
    ublk0 = row0 // bq
    pos = (lambda c: nc - 1 - c) if rev else (lambda c: c)
    row = lambda b, c: (b * nc + pos(c), 0)
    sshape = (SSD_GROUPS, SSD_STATE, SSD_HEADS_PER_GROUP * SSD_HEAD_DIM)
    sspec = pl.BlockSpec((None,) + sshape, lambda b, c: (b, 0, 0, 0))
    head_col = int(rev) * SSD_HEADS + jnp.arange(SSD_DIM, dtype=jnp.int32) // SSD_HEAD_DIM
    sel = (jnp.arange(U_DT_W, dtype=jnp.int32)[:, None] == head_col[None, :]).astype(BF16)
    out_specs, out_shape = [sspec], [jax.ShapeDtypeStruct((nseq,) + sshape, F32)]
    if need_y:
        out_specs.insert(0, pl.BlockSpec((bq, SSD_DIM), row))
        out_shape.insert(0, jax.ShapeDtypeStruct((nseq * seqlen, SSD_DIM), F32))
    res = pl.pallas_call(
        functools.partial(_ssd_scan_kernel, q=q, sub=sub, rev=rev, direction=int(rev), need_y=need_y),
        grid=(nseq, nc),
        in_specs=[pl.BlockSpec((bq, SSD_DIM), row), pl.BlockSpec((bq, SSD_BC), row),
                  pl.BlockSpec((bq, SSD_BC), row),
                  pl.BlockSpec((bq, U_DT_W), lambda b, c: (ublk0 + b * nc + pos(c), U_DT // U_DT_W)),
                  _const_spec((1, U_DT_W)), _const_spec((1, U_DT_W)), _const_spec(sel.shape), sspec],
        out_specs=out_specs, out_shape=out_shape,
        scratch_shapes=[pltpu.VMEM(sshape, F32)],
        compiler_params=_params("parallel", "arbitrary"),
    )(xs, bm, cm, u, dt_bias, a_log, sel, s0)
    return res if need_y else (None, res[0])


def _ssd_fin_kernel(yf_ref, yb_ref, xs_ref, z_ref, d_ref, g_ref, o_ref):
    z = z_ref[...]
    y = (yf_ref[...] + yb_ref[...] + d_ref[...] * xs_ref[...]) * (z * _sigmoid(z))
    o_ref[...] = _rms(y, g_ref[...]).astype(o_ref.dtype)


def _ssd_finish(yf, yb, xs, u, row0, d_skip, norm_g):
    nrows = yf.shape[0]
    tq = _tile(nrows, 512)
    blk0 = row0 // tq
    tok = pl.BlockSpec((tq, SSD_DIM), lambda i: (i, 0))
    return pl.pallas_call(
        _ssd_fin_kernel,
        grid=(nrows // tq,),
        in_specs=[tok, tok, tok, pl.BlockSpec((tq, U_Z_W), lambda i: (blk0 + i, U_Z // U_Z_W)),
                  _const_spec((1, SSD_DIM)), _const_spec((1, SSD_DIM))],
        out_specs=tok,
        out_shape=jax.ShapeDtypeStruct((nrows, SSD_DIM), BF16),
        compiler_params=_params("parallel"),
    )(yf, yb, xs, u, d_skip, norm_g)


def _rwkv_prep_kernel(cur_ref, prev_ref, next_ref, mu_ref, w0_ref, w2_ref, a0_ref, a2_ref, g2_ref,
                      kk_ref, ka_ref, rk_ref, bd_ref,
                      p_o, g_o, bonus_o, scr, *, tq, tiles_per_seq):
    r_o, lwf_o, lwb_o, k_o, v_o, kk_o, ka_o = (
        p_o.at[:, c * RWKV_DIM:(c + 1) * RWKV_DIM] for c in (RP_R, RP_LWF, RP_LWB, RP_K, RP_V, RP_KK, RP_KA))
    _fill_halo(scr, cur_ref, prev_ref, next_ref, tq, tiles_per_seq)
    u = cur_ref[...]
    shift = 0.5 * (scr[pl.ds(SUBLANE - 1, tq), :] + scr[pl.ds(SUBLANE + 1, tq), :]) - u
    x = u + shift * mu_ref[...]
    r = x[:, 0:RWKV_DIM]
    k = x[:, RWKV_DIM:2 * RWKV_DIM]
    v = x[:, 2 * RWKV_DIM:3 * RWKV_DIM]
    bd = bd_ref[...]
    for d, (off, out) in enumerate(((RW_WF, lwf_o), (RW_WB, lwb_o))):
        lo = jnp.tanh(x[:, off:off + LANE])
        wl = -_softplus(-(w0_ref[d:d + 1, :] + _dot(lo, w2_ref[d]))) - 0.5
        out[...] = -jnp.exp(wl)
    a = _sigmoid(a0_ref[...] + _dot(x[:, RW_AL:RW_AL + LANE], a2_ref[...]))
    kk = k * kk_ref[...]
    kk = kk * lax.rsqrt(_blocksum(kk * kk, bd) + 1e-12)
    kmod = k * (1.0 + (a - 1.0) * ka_ref[...])
    r_o[...] = r
    k_o[...] = kmod
    v_o[...] = v
    kk_o[...] = kk
    ka_o[...] = kk * a
    g_o[...] = _dot(_sigmoid(x[:, RW_GL:RW_GL + GATE_LORA]), g2_ref[...])
    bonus_o[...] = _blocksum(r * kmod * rk_ref[...], bd) * v


def _rwkv_prep(u, row0, nseq, seqlen, p):
    nrows = nseq * seqlen
    tq = _tile(seqlen, 256)
    cur, prev, nxt = _halo_specs(tq, U_RWKV_W, row0 // tq, 0, u.shape[0] // SUBLANE)
    consts = [p["mu"], p["w0"], p["w2"], p["a0"], p["a2"], p["g2"], p["k_k"], p["k_a"], p["r_k"], p["bd"]]
    tok = pl.BlockSpec((tq, RWKV_DIM), lambda i: (i, 0))
    return pl.pallas_call(
        functools.partial(_rwkv_prep_kernel, tq=tq, tiles_per_seq=seqlen // tq),
        grid=(nrows // tq,),
        in_specs=[cur, prev, nxt] + [_const_spec(c.shape) for c in consts],
        out_specs=[pl.BlockSpec((tq, RP_N * RWKV_DIM), lambda i: (i, 0)), tok, tok],
        out_shape=[jax.ShapeDtypeStruct((nrows, RP_N * RWKV_DIM), F32)]
        + [jax.ShapeDtypeStruct((nrows, RWKV_DIM), F32)] * 2,
        scratch_shapes=[pltpu.VMEM((tq + 2 * SUBLANE, U_RWKV_W), F32)],
        compiler_params=_params("parallel"),
    )(u, u, u, *consts)


def _rwkv_scan_kernel(*refs, C, sub, need_y):
    packed = refs[0:2]
    s0_refs = refs[2:4]
    if need_y:
        y_refs, sfin_refs, st = refs[4:6], refs[6:8], refs[8]
    else:
        sfin_refs, st = refs[4:6], refs[6]

    @pl.when(pl.program_id(1) == 0)
    def _():
        st[0] = s0_refs[0][...]
        st[1] = s0_refs[1][...]

    n2 = 2 * C
    row = lax.broadcasted_iota(jnp.int32, (n2, n2), 0)
    col = lax.broadcasted_iota(jnp.int32, (n2, n2), 1)
    t, s = row & (C - 1), col & (C - 1)
    strict = ((s < t), (s > t))
    incl = ((s <= t), (s >= t))
    eye = jnp.where(row == col, 1.0, 0.0)
    ti = lax.broadcasted_iota(jnp.int32, (C, C), 0)
    tk = lax.broadcasted_iota(jnp.int32, (C, C), 1)
    tri = (jnp.where(tk <= ti, 1.0, 0.0).astype(BF16), jnp.where(tk >= ti, 1.0, 0.0).astype(BF16))
    head0 = lax.broadcasted_iota(jnp.int32, (C, LANE), 1) < RWKV_HEAD_DIM

    def stack(x):
        return jnp.concatenate([jnp.where(head0, x, 0.0), jnp.where(head0, 0.0, x)], axis=0)

    probs = [(d, p) for d in (0, 1) for p in range(RWKV_PAIRS)]
    blocks = ((RP_R, RP_LWF, RP_K, RP_V, RP_KK, RP_KA), (RP_R, RP_LWB, RP_K, RP_V, RP_KK, RP_KA))
    for step in range(sub):
        _rwkv_chunk(packed, y_refs if need_y else None, st, ((step * C), (sub - 1 - step) * C), C, probs, blocks,
                    strict, incl, eye, tri, stack)
    sfin_refs[0][...] = st[0]
    sfin_refs[1][...] = st[1]


def _rwkv_chunk(packed, y_refs, st, offs, C, probs, blocks, strict, incl, eye, tri, stack):
    n2 = 2 * C
    need_y = y_refs is not None
    ld = lambda d, p, i: packed[d][offs[d]:offs[d] + C,
                                   blocks[d][i] * RWKV_DIM + p * LANE:blocks[d][i] * RWKV_DIM + (p + 1) * LANE]
    lw = [ld(d, p, 1) for d, p in probs]
    cs = [_dot_exact_lhs(tri[d], x) for (d, p), x in zip(probs, lw)]
    tot = [x[C - 1:C, :] if d == 0 else x[0:1, :] for (d, p), x in zip(probs, cs)]
    ems = [jnp.exp(-x) for x in cs]
    edec = [jnp.exp(a - x) for a, x in zip(tot, cs)]
    kt = [stack(ld(d, p, 4) * jnp.exp(x - w)) for (d, p), x, w in zip(probs, cs, lw)]
    rt = [stack(ld(d, p, 0) * jnp.exp(x)) for (d, p), x in zip(probs, cs)]
    at = [stack(ld(d, p, 5) * e) for (d, p), e in zip(probs, ems)]
    kb = [stack(ld(d, p, 2) * e) for (d, p), e in zip(probs, ems)]
    at2 = [stack(ld(d, p, 5) * e) for (d, p), e in zip(probs, edec)]
    kb2 = [stack(ld(d, p, 2) * e) for (d, p), e in zip(probs, edec)]
    vs_t = [stack(ld(d, p, 3)).T for d, p in probs]
    big = [_dot_nt(jnp.concatenate([a, b], axis=0), jnp.concatenate([c, e], axis=0))
           for a, b, c, e in zip(kt, rt, at, kb)]
    npow = [-jnp.where(strict[d], x[:n2, :n2], 0.0) for (d, p), x in zip(probs, big)]
    tinv = [eye + x for x in npow]
    npow = [_dot(x, x) for x in npow]
    for step in range(int(math.log2(C)) - 1):
        if step < int(math.log2(C)) - 2:
            both = [_dot(jnp.concatenate([t, x], axis=0), x) for t, x in zip(tinv, npow)]
            tinv = [t + b[:n2] for t, b in zip(tinv, both)]
            npow = [b[n2:] for b in both]
        else:
            tinv = [t + _dot(t, x) for t, x in zip(tinv, npow)]
    state = [st[d, p] for d, p in probs]
    lhs = [jnp.concatenate(
        [jnp.concatenate([a, jnp.where(strict[d], x[:n2, n2:], 0.0)], axis=1),
         jnp.concatenate([b, jnp.where(incl[d], x[n2:, n2:], 0.0)], axis=1)], axis=0)
        for (d, p), x, a, b in zip(probs, big, kt, rt)]
    xy = [_dot_nt(a, jnp.concatenate([x, v], axis=1)) for a, x, v in zip(lhs, state, vs_t)]
    u_m = [-_dot(x, y[:n2]) for x, y in zip(tinv, xy)]
    if need_y:
        for (d, p), x, u, y in zip(probs, big, u_m, xy):
            yy = y[n2:] + _dot(jnp.where(incl[d], x[n2:, :n2], 0.0), u)
            y_refs[d][offs[d]:offs[d] + C, p * LANE:(p + 1) * LANE] = yy[:C] + yy[C:]
    for (d, p), x, a, u, v, k2, a2 in zip(probs, state, tot, u_m, vs_t, kb2, at2):
        st[d, p] = x * jnp.exp(a) + _dot(jnp.concatenate([u.T, v], axis=1), jnp.concatenate([a2, k2], axis=0))


def _rwkv_scan(packed, nseq, seqlen, s0f, s0b, need_y):
    C = RWKV_CHUNK
    sub = max(s for s in (4, 2, 1) if seqlen % (s * C) == 0)
    nc = seqlen // (sub * C)
    fwd, bwd = (lambda b, c: (b * nc + c, 0)), (lambda b, c: (b * nc + nc - 1 - c, 0))
    tok_f, tok_b = pl.BlockSpec((sub * C, RWKV_DIM), fwd), pl.BlockSpec((sub * C, RWKV_DIM), bwd)
    pk_f = pl.BlockSpec((sub * C, RP_N * RWKV_DIM), fwd)
    pk_b = pl.BlockSpec((sub * C, RP_N * RWKV_DIM), bwd)
    sshape = (RWKV_PAIRS, LANE, LANE)
    sspec = pl.BlockSpec((None,) + sshape, lambda b, c: (b, 0, 0, 0))
    sstruct = jax.ShapeDtypeStruct((nseq,) + sshape, F32)
    out_specs, out_shape = [sspec, sspec], [sstruct, sstruct]
    if need_y:
        ystruct = jax.ShapeDtypeStruct((nseq * seqlen, RWKV_DIM), F32)
        out_specs, out_shape = [tok_f, tok_b] + out_specs, [ystruct, ystruct] + out_shape
    res = pl.pallas_call(
        functools.partial(_rwkv_scan_kernel, C=C, sub=sub, need_y=need_y),
        grid=(nseq, nc),
        in_specs=[pk_f, pk_b, sspec, sspec],
        out_specs=out_specs, out_shape=out_shape,
        scratch_shapes=[pltpu.VMEM((2,) + sshape, F32)],
        compiler_params=_params("parallel", "arbitrary"),
    )(packed, packed, s0f, s0b)
    return tuple(res) if need_y else (None, None) + tuple(res)


def _rwkv_fin_kernel(yf_ref, yb_ref, bonus_ref, g_ref, lg_ref, lb_ref, bd_ref, o_ref):
    y = yf_ref[...] + yb_ref[...]
    bd = bd_ref[...]
    dv = y - _blocksum(y, bd) * (1.0 / RWKV_HEAD_DIM)
    var = _blocksum(dv * dv, bd) * (1.0 / RWKV_HEAD_DIM)
    yn = dv * lax.rsqrt(var + GN_EPS) * lg_ref[...] + lb_ref[...]
    o_ref[...] = ((yn + bonus_ref[...]) * g_ref[...]).astype(o_ref.dtype)


def _rwkv_finish(yf, yb, bonus, g, ln_g, ln_b, bd):
    nrows = yf.shape[0]
    tq = _tile(nrows, 512)
    tok = pl.BlockSpec((tq, RWKV_DIM), lambda i: (i, 0))
    return pl.pallas_call(
        _rwkv_fin_kernel,
        grid=(nrows // tq,),
        in_specs=[tok] * 4 + [_const_spec((1, RWKV_DIM))] * 2 + [_const_spec(bd.shape)],
        out_specs=tok,
        out_shape=jax.ShapeDtypeStruct((nrows, RWKV_DIM), BF16),
        compiler_params=_params("parallel"),
    )(yf, yb, bonus, g, ln_g, ln_b, bd)


def _to_cols(t, b, rows):
    return t.reshape(b, rows, GRID_W, -1).transpose(0, 2, 1, 3).reshape(b * rows * GRID_W, -1)


def _to_rows(t, b, rows):
    return t.reshape(b, GRID_W, rows, -1).transpose(0, 2, 1, 3).reshape(b * rows * GRID_W, -1)


def _pad_cols(w, n):
    return jnp.pad(w, [(0, 0)] * (w.ndim - 1) + [(0, n - w.shape[-1])])


def _pad_rows(w, n):
    return jnp.pad(w, [(0, 0)] * (w.ndim - 2) + [(0, n - w.shape[-2]), (0, 0)])


def kernel(x, c, ctx, c_ctx, mod_w, mod_b, norm_g, w_in, conv_w, conv_b, conv_ln_g, conv_ln_b, conv_out, ssd_conv_w, ssd_conv_b, ssd_A_log, ssd_dt_bias, ssd_D, ssd_norm_g, ssd_out, fourier_out, rwkv_mu, rwkv_w0, rwkv_w2, rwkv_a0, rwkv_a2, rwkv_g2, rwkv_k_k, rwkv_k_a, rwkv_r_k, rwkv_ln_g, rwkv_ln_b, rwkv_out, w_o, mlp_up, mlp_down):
    B, L, D = x.shape
    CL = ctx.shape[1]
    depth = mod_w.shape[0]
    rows = L // GRID_W
    n_lat, n_ctx = B * L, B * CL
    n_all = n_lat + n_ctx
    tm = _tile(math.gcd(L, n_ctx), 1024)
    tm_small = _tile(tm, 512)
    tm_merge = _tile(tm, 256)
    mod_idx = lambda t: (lambda i: jnp.minimum((i * t) // L, B))

    w_u, w_rw, w_gate = _split_w_in(w_in)
    down_w, wo_w = mlp_down.astype(BF16), w_o.astype(BF16)
    branch_w = [conv_out.astype(BF16), ssd_out.astype(BF16), fourier_out.astype(BF16), rwkv_out.astype(BF16)]
    mu = rwkv_mu
    mu_cols = jnp.concatenate(
        [mu[:, :3 * RWKV_DIM]] + [_pad_cols(mu[:, a:b], LANE) for a, b in ((1536, 1600), (1600, 1664), (1664, 1728))]
        + [mu[:, 1728:]], axis=1)
    head_of = jnp.arange(RWKV_DIM, dtype=jnp.int32) // RWKV_HEAD_DIM
    bd = (head_of[:, None] == head_of[None, :]).astype(BF16)
    gidx = jnp.arange(FOURIER_DIM, dtype=jnp.int32)
    cc, sc_ = _dft_mats(FOURIER_GROUP_DIM, FOURIER_GROUP_DIM ** -0.5)
    same_group = (gidx[:, None] // FOURIER_GROUP_DIM) == (gidx[None, :] // FOURIER_GROUP_DIM)
    tile_g = lambda m: jnp.where(same_group, jnp.tile(m, (FOURIER_GROUPS, FOURIER_GROUPS)), 0.0)
    chan_w = jnp.concatenate([tile_g(cc), -tile_g(sc_)], axis=1).astype(BF16)
    conv_w_p = _pad_rows(conv_w, 32)
    ssd_conv_w_p = _pad_rows(ssd_conv_w, SUBLANE)
    dt_bias_p = _pad_cols(ssd_dt_bias.reshape(depth, 1, 2 * SSD_HEADS), U_DT_W)
    a_log_p = _pad_cols(ssd_A_log.reshape(depth, 1, 2 * SSD_HEADS), U_DT_W)
    d_skip = jnp.repeat(ssd_D, SSD_HEAD_DIM, axis=1).reshape(depth, 1, SSD_DIM)

    cond = jnp.concatenate([c, c_ctx[None, :], jnp.zeros((SUBLANE - (B + 1) % SUBLANE, D), F32)], axis=0)
    mod = _mod_all(cond, mod_w, mod_b)

    xs = jnp.concatenate([x.reshape(n_lat, D), ctx.reshape(n_ctx, D)], axis=0)
    ssd_zero = jnp.zeros((B, SSD_GROUPS, SSD_STATE, SSD_HEADS_PER_GROUP * SSD_HEAD_DIM), F32)
    rwkv_zero = jnp.zeros((B, RWKV_PAIRS, LANE, LANE), F32)

    mods = lambda l: tuple(mod[l, :, j * D:(j + 1) * D].reshape(-1, 1, D) for j in range(6))
    h = _norm_mod(xs, norm_g[0, 0].reshape(1, D), *mods(0)[1::-1], n_all, mod_idx(tm), tm)
    for i in range(depth):
        last = i == depth - 1
        m_out = n_lat if last else n_all
        sh1, sc1, g1, sh2, sc2, g2 = mods(i)
        ng = norm_g[i].reshape(4, 1, D)

        u = _matmul(h, w_u, n_all, tm, F32, tn_cap=1536, layer=i)
        gates = _matmul(h, w_gate, m_out, tm, BF16, epilogue="sigmoid", tn_cap=2048, layer=i)

        a_conv = _conformer(u, 0, n_lat, GRID_W, conv_w_p[i], conv_b[i][None], conv_ln_g[i][None], conv_ln_b[i][None])
        z = _matmul(u, chan_w, m_out, tm, BF16, a_col=U_FFT // U_FFT_W)
        a_fft = _fourier(z, 0, B, L)
        if not last:
            a_conv = jnp.concatenate(
                [a_conv, _conformer(u, n_lat, n_ctx, CL, conv_w_p[i], conv_b[i][None], conv_ln_g[i][None],
                                    conv_ln_b[i][None])], axis=0)
            a_fft = jnp.concatenate([a_fft, _fourier(z, n_lat, B, CL)], axis=0)

        ssd_args = (dt_bias_p[i], a_log_p[i])
        xs_c, bm_c, cm_c = _ssd_conv(u, n_lat, B, CL, ssd_conv_w_p[i], ssd_conv_b[i][None])
        yfc, sf = _ssd_scan(xs_c, bm_c, cm_c, u, n_lat, B, CL, *ssd_args, ssd_zero, False, not last)
        ybc, sb = _ssd_scan(xs_c, bm_c, cm_c, u, n_lat, B, CL, *ssd_args, ssd_zero, True, not last)
        xs_l, bm_l, cm_l = _ssd_conv(u, 0, B, L, ssd_conv_w_p[i], ssd_conv_b[i][None])
        yfl, _ = _ssd_scan(xs_l, bm_l, cm_l, u, 0, B, L, *ssd_args, sf, False, True)
        ybl, _ = _ssd_scan(xs_l, bm_l, cm_l, u, 0, B, L, *ssd_args, sb, True, True)
        a_ssd = _ssd_finish(yfl, ybl, xs_l, u, 0, d_skip[i], ssd_norm_g[i][None])
        if not last:
            a_ssd = jnp.concatenate(
                [a_ssd, _ssd_finish(yfc, ybc, xs_c, u, n_lat, d_skip[i], ssd_norm_g[i][None])], axis=0)

        rp = dict(mu=mu_cols[i][None], w0=rwkv_w0[i], w2=_pad_rows(rwkv_w2[i], LANE).astype(BF16),
                  a0=rwkv_a0[i][None], a2=_pad_rows(rwkv_a2[i], LANE).astype(BF16), g2=rwkv_g2[i].astype(BF16),
                  k_k=rwkv_k_k[i][None], k_a=rwkv_k_a[i][None], r_k=rwkv_r_k[i].reshape(1, RWKV_DIM), bd=bd)
        u_rc = _matmul(h, w_rw, n_ctx, tm, F32, a_row=n_lat // tm, layer=i)
        packed, g_, bonus = _rwkv_prep(u_rc, 0, B, CL, rp)
        yfc, ybc, s_f, s_b = _rwkv_scan(packed, B, CL, rwkv_zero, rwkv_zero, not last)
        if not last:
            a_rwkv_c = _rwkv_finish(yfc, ybc, bonus, g_, rwkv_ln_g[i][None], rwkv_ln_b[i][None], bd)
        u_rl = _matmul(_to_cols(h[:n_lat], B, rows), w_rw, n_lat, tm, F32, layer=i)
        packed, g_, bonus = _rwkv_prep(u_rl, 0, B, L, rp)
        yfl, ybl, _, _ = _rwkv_scan(packed, B, L, s_f, s_b, True)
        a_rwkv = _to_rows(_rwkv_finish(yfl, ybl, bonus, g_, rwkv_ln_g[i][None], rwkv_ln_b[i][None], bd), B, rows)
        if not last:
            a_rwkv = jnp.concatenate([a_rwkv, a_rwkv_c], axis=0)

        m = _merge(gates, [a_conv, a_ssd, a_fft, a_rwkv], branch_w, m_out, tm_merge, i)
        xs, h2 = _matmul_residual(m, wo_w, xs, g1, ng[1], m_out, mod_idx(tm_small), tm_small,
                                  nxt=(ng[2], sc2, sh2), layer=i)

        mid = _matmul_f32w(h2, mlp_up, m_out, tm, BF16, epilogue="relu2", layer=i)
        if last:
            xs = _matmul_residual(mid, down_w, xs, g2, ng[3], m_out, mod_idx(tm_small), tm_small, layer=i)
        else:
            xs, h = _matmul_residual(mid, down_w, xs, g2, ng[3], m_out, mod_idx(tm_small), tm_small,
                                     nxt=(norm_g[i + 1, 0].reshape(1, D), *mods(i + 1)[1::-1]), layer=i)

    return xs[:n_lat].reshape(B, L, D)
```

```python
import functools
import math

import jax
import jax.numpy as jnp
import numpy as np
from jax import lax
from jax.experimental import pallas as pl
from jax.experimental.pallas import tpu as pltpu

F32 = jnp.float32
BF16 = jnp.bfloat16

GRID_W = 64
RMS_EPS = 1e-6
LN_EPS = 1e-5
GN_EPS = 64e-5
CONV_DIM = 512
CONV_WIDTH = 31
SSD_HEADS = 12
SSD_HEAD_DIM = 64
SSD_DIM = SSD_HEADS * SSD_HEAD_DIM
SSD_GROUPS = 4
SSD_HEADS_PER_GROUP = SSD_HEADS // SSD_GROUPS
SSD_STATE = 128
SSD_CONV = 5
SSD_BC = SSD_GROUPS * SSD_STATE
SSD_XBC = SSD_DIM + 2 * SSD_BC
SSD_IN = SSD_DIM + SSD_XBC + 2 * SSD_HEADS
SSD_CHUNK = 128
FOURIER_GROUPS = 4
FOURIER_GROUP_DIM = 128
FOURIER_DIM = FOURIER_GROUPS * FOURIER_GROUP_DIM
RWKV_HEADS = 8
RWKV_HEAD_DIM = 64
RWKV_DIM = RWKV_HEADS * RWKV_HEAD_DIM
RWKV_PAIRS = RWKV_HEADS // 2
DECAY_LORA = 64
ICL_LORA = 64
GATE_LORA = 128
RWKV_IN = 3 * RWKV_DIM + 2 * DECAY_LORA + ICL_LORA + GATE_LORA
RWKV_CHUNK = 64
REC_IN = SSD_IN + RWKV_IN
N_BRANCH = 4

LANE = 128
SUBLANE = 8
VMEM_LIMIT = 56 * 1024 * 1024

U_XBC, U_XBC_W = 0, SSD_XBC
U_DT, U_DT_W = 1792, LANE
U_CONV, U_CONV_W = 2048, 2 * CONV_DIM
U_Z, U_Z_W = 3072, SSD_DIM
U_FFT, U_FFT_W = 4096, FOURIER_DIM
U_N = 4608
U_RWKV_W = 2048
RW_WF, RW_WB, RW_AL, RW_GL = 1536, 1664, 1792, 1920
RP_R, RP_LWF, RP_LWB, RP_K, RP_V, RP_KK, RP_KA, RP_N = 0, 1, 2, 3, 4, 5, 6, 7


def _params(*sem):
    return pltpu.CompilerParams(dimension_semantics=sem, vmem_limit_bytes=VMEM_LIMIT)


def _tile(n, cap, quantum=SUBLANE):
    if n <= cap:
        return n
    t = (cap // quantum) * quantum
    while t >= quantum:
        if n % t == 0:
            return t
        t -= quantum
    raise ValueError(f"no tile for {n} under {cap}")


def _const_spec(shape):
    nd = len(shape)
    return pl.BlockSpec(shape, lambda *_: (0,) * nd)


def _dot(a, b):
    return jnp.dot(a.astype(BF16), b.astype(BF16), preferred_element_type=F32)


def _dot_nt(a, b):
    return lax.dot_general(a.astype(BF16), b.astype(BF16), (((1,), (1,)), ((), ())),
                           preferred_element_type=F32)


def _split3(x):
    h1 = x.astype(BF16)
    r1 = x - h1.astype(F32)
    h2 = r1.astype(BF16)
    h3 = (r1 - h2.astype(F32)).astype(BF16)
    return h1, h2, h3


def _dot_exact_lhs(m, x):
    h1, h2, h3 = _split3(x)
    d = lambda h: jnp.dot(m, h, preferred_element_type=F32)
    return d(h1) + d(h2) + d(h3)


def _blocksum(x, bd):
    hi = x.astype(BF16)
    lo = (x - hi.astype(F32)).astype(BF16)
    return (jnp.dot(hi, bd, preferred_element_type=F32) + jnp.dot(lo, bd, preferred_element_type=F32))


def _sigmoid(x):
    return 1.0 / (1.0 + jnp.exp(-x))


def _softplus(x):
    return jnp.maximum(x, 0.0) + jnp.log1p(jnp.exp(-jnp.abs(x)))


def _rms(y, g):
    return y * lax.rsqrt(jnp.mean(y * y, axis=-1, keepdims=True) + RMS_EPS) * g


_O_RWKV, _O_CONV = SSD_IN, REC_IN
_O_FFT, _O_GATE = REC_IN + 2 * CONV_DIM, REC_IN + 2 * CONV_DIM + FOURIER_DIM
_W_IN_PIECES = (
    ("rw", 0, _O_RWKV, 3 * RWKV_DIM), ("rw", RW_WF, _O_RWKV + 1536, DECAY_LORA),
    ("rw", RW_WB, _O_RWKV + 1600, DECAY_LORA), ("rw", RW_AL, _O_RWKV + 1664, ICL_LORA),
    ("rw", RW_GL, _O_RWKV + 1728, GATE_LORA),
    ("u", U_XBC, SSD_DIM, SSD_XBC), ("u", U_DT, SSD_DIM + SSD_XBC, 2 * SSD_HEADS),
    ("u", U_CONV, _O_CONV, 2 * CONV_DIM), ("u", U_Z, 0, SSD_DIM), ("u", U_FFT, _O_FFT, FOURIER_DIM),
)
_RELAYOUT_CHUNK = 1024


def _split_w_in_kernel(w_ref, u_ref, rw_ref, g_ref):
    n_in = w_ref.shape[-1]
    dst = {"u": u_ref, "rw": rw_ref, "g": g_ref}
    u_ref[...] = jnp.zeros_like(u_ref)
    rw_ref[...] = jnp.zeros_like(rw_ref)
    pieces = _W_IN_PIECES + (("g", 0, _O_GATE, g_ref.shape[-1]),)
    for name, dcol, scol, width in pieces:
        for off in range(0, width, _RELAYOUT_CHUNK):
            w = min(_RELAYOUT_CHUNK, width - off)
            lo = scol + off
            lo_al = (lo // LANE) * LANE
            hi_al = min(-(-(lo + w) // LANE) * LANE, n_in)
            x = w_ref[:, lo_al:hi_al]
            dst[name][:, dcol + off:dcol + off + w] = x[:, lo - lo_al:lo - lo_al + w].astype(BF16)


def _split_w_in(w_in):
    depth, d, n_in = w_in.shape
    n_gate = n_in - _O_GATE
    tk = _tile(d, 128)
    outs = (U_N, U_RWKV_W, n_gate)
    return pl.pallas_call(
        _split_w_in_kernel,
        grid=(depth, d // tk),
        in_specs=[pl.BlockSpec((None, tk, n_in), lambda l, i: (l, i, 0))],
        out_specs=[pl.BlockSpec((None, tk, n), lambda l, i: (l, i, 0)) for n in outs],
        out_shape=[jax.ShapeDtypeStruct((depth, d, n), BF16) for n in outs],
        compiler_params=_params("parallel", "parallel"),
    )(w_in)


def _mod_kernel(a_ref, w_ref, b_ref, o_ref):
    a = a_ref[...]
    a = a * _sigmoid(a)
    o_ref[...] = _dot(a, w_ref[...]) + b_ref[...]


def _mod_all(cond, mod_w, mod_b):
    depth, d, n = mod_w.shape
    tn = _tile(n, 1024, LANE)
    return pl.pallas_call(
        _mod_kernel,
        grid=(depth, n // tn),
        in_specs=[_const_spec(cond.shape),
                  pl.BlockSpec((None, d, tn), lambda l, j: (l, 0, j)),
                  pl.BlockSpec((None, 1, tn), lambda l, j: (l, 0, j))],
        out_specs=pl.BlockSpec((None, cond.shape[0], tn), lambda l, j: (l, 0, j)),
        out_shape=jax.ShapeDtypeStruct((depth, cond.shape[0], n), F32),
        compiler_params=_params("parallel", "parallel"),
    )(cond, mod_w, mod_b.reshape(depth, 1, n))


def _norm_mod_kernel(x_ref, g_ref, sc_ref, sh_ref, o_ref):
    o_ref[...] = (_rms(x_ref[...], g_ref[...]) * (1.0 + sc_ref[...]) + sh_ref[...]).astype(o_ref.dtype)


def _norm_mod(x, g, sc, sh, m_rows, mod_idx, tm):
    d = x.shape[1]
    return pl.pallas_call(
        _norm_mod_kernel,
        grid=(m_rows // tm,),
        in_specs=[pl.BlockSpec((tm, d), lambda i: (i, 0)),
                  _const_spec((1, d)),
                  pl.BlockSpec((None, 1, d), lambda i: (mod_idx(i), 0, 0)),
                  pl.BlockSpec((None, 1, d), lambda i: (mod_idx(i), 0, 0))],
        out_specs=pl.BlockSpec((tm, d), lambda i: (i, 0)),
        out_shape=jax.ShapeDtypeStruct((m_rows, d), BF16),
        compiler_params=_params("parallel"),
    )(x, g, sc, sh)


def _mm_kernel(a_ref, w_ref, o_ref, *, epilogue):
    acc = _dot(a_ref[...], w_ref[...])
    if epilogue == "relu2":
        acc = jnp.square(jnp.maximum(acc, 0.0))
    elif epilogue == "sigmoid":
        acc = _sigmoid(acc)
    o_ref[...] = acc.astype(o_ref.dtype)


def _layer_spec(w, layer, block, index):
    if w.ndim == 2:
        return pl.BlockSpec(block, index)
    return pl.BlockSpec((None,) + block, lambda *ids: (layer,) + index(*ids))


def _matmul(a, w, m_rows, tm, out_dtype, epilogue=None, a_col=0, a_row=0, tn_cap=1024, layer=0):
    k, n = w.shape[-2:]
    tn = _tile(n, tn_cap, LANE)
    return pl.pallas_call(
        functools.partial(_mm_kernel, epilogue=epilogue),
        grid=(m_rows // tm, n // tn),
        in_specs=[pl.BlockSpec((tm, k), lambda i, j: (a_row + i, a_col)),
                  _layer_spec(w, layer, (k, tn), lambda i, j: (0, j))],
        out_specs=pl.BlockSpec((tm, tn), lambda i, j: (i, j)),
        out_shape=jax.ShapeDtypeStruct((m_rows, n), out_dtype),
        compiler_params=_params("parallel", "arbitrary"),
    )(a, w)


def _mm_wcast_kernel(a_ref, w_ref, o_ref, wb_ref, *, epilogue):
    @pl.when(pl.program_id(1) == 0)
    def _():
        wb_ref[...] = w_ref[...].astype(wb_ref.dtype)

    _mm_kernel(a_ref, wb_ref, o_ref, epilogue=epilogue)


def _matmul_f32w(a, w, m_rows, tm, out_dtype, epilogue=None, tn_cap=1024, layer=0):
    k, n = w.shape[-2:]
    tn = _tile(n, tn_cap, LANE)
    return pl.pallas_call(
        functools.partial(_mm_wcast_kernel, epilogue=epilogue),
        grid=(n // tn, m_rows // tm),
        in_specs=[pl.BlockSpec((tm, k), lambda j, i: (i, 0)),
                  _layer_spec(w, layer, (k, tn), lambda j, i: (0, j))],
        out_specs=pl.BlockSpec((tm, tn), lambda j, i: (i, j)),
        out_shape=jax.ShapeDtypeStruct((m_rows, n), out_dtype),
        scratch_shapes=[pltpu.VMEM((k, tn), BF16)],
        compiler_params=_params("parallel", "arbitrary"),
    )(a, w)


def _mm_res_kernel(a_ref, w_ref, x_ref, gate_ref, g_ref, *rest, with_next, nk):
    if with_next:
        g2_ref, sc_ref, sh_ref, o_ref, h_ref = rest
    else:
        (o_ref,) = rest
    def finish(y):
        xn = x_ref[...] + gate_ref[...] * _rms(y, g_ref[...])
        o_ref[...] = xn
        if with_next:
            h_ref[...] = (_rms(xn, g2_ref[...]) * (1.0 + sc_ref[...]) + sh_ref[...]).astype(h_ref.dtype)

    if nk == 1:
        finish(_dot(a_ref[...], w_ref[...]))
        return
    k = pl.program_id(1)

    @pl.when(k == 0)
    def _():
        o_ref[...] = jnp.zeros_like(o_ref)

    o_ref[...] += _dot(a_ref[...], w_ref[...])

    @pl.when(k == nk - 1)
    def _():
        finish(o_ref[...])


def _matmul_residual(a, w, x, gate, g, m_rows, mod_idx, tm, nxt=None, tk_cap=2048, layer=0):
    k, d = w.shape[-2:]
    tk = _tile(k, tk_cap, LANE)
    mod_spec = pl.BlockSpec((None, 1, d), lambda i, kk: (mod_idx(i), 0, 0))
    tile_spec = pl.BlockSpec((tm, d), lambda i, kk: (i, 0))
    in_specs = [pl.BlockSpec((tm, tk), lambda i, kk: (i, kk)), _layer_spec(w, layer, (tk, d), lambda i, kk: (kk, 0)),
                tile_spec, mod_spec, _const_spec((1, d))]
    out_specs, out_shape, args = tile_spec, jax.ShapeDtypeStruct((m_rows, d), F32), (a, w, x, gate, g)
    if nxt is not None:
        in_specs += [_const_spec((1, d)), mod_spec, mod_spec]
        out_specs, out_shape = [tile_spec, tile_spec], [out_shape, jax.ShapeDtypeStruct((m_rows, d), BF16)]
        args += tuple(nxt)
    return pl.pallas_call(
        functools.partial(_mm_res_kernel, with_next=nxt is not None, nk=k // tk),
        grid=(m_rows // tm, k // tk),
        in_specs=in_specs, out_specs=out_specs, out_shape=out_shape,
        compiler_params=_params("parallel", "arbitrary"),
    )(*args)


def _merge_kernel(gates_ref, a0, a1, a2, a3, w0, w1, w2, w3, o_ref):
    d = o_ref.shape[-1]
    m = None
    for i, (a, w) in enumerate(((a0, w0), (a1, w1), (a2, w2), (a3, w3))):
        t = gates_ref[:, i * d:(i + 1) * d].astype(F32) * _dot(a[...], w[...])
        m = t if m is None else m + t
    o_ref[...] = m.astype(o_ref.dtype)


def _merge(gates, acts, weights, m_rows, tm, layer):
    d = weights[0].shape[-1]
    return pl.pallas_call(
        _merge_kernel,
        grid=(m_rows // tm,),
        in_specs=[pl.BlockSpec((tm, N_BRANCH * d), lambda i: (i, 0))]
        + [pl.BlockSpec((tm, a.shape[1]), lambda i: (i, 0)) for a in acts]
        + [_layer_spec(w, layer, w.shape[-2:], lambda i: (0, 0)) for w in weights],
        out_specs=pl.BlockSpec((tm, d), lambda i: (i, 0)),
        out_shape=jax.ShapeDtypeStruct((m_rows, d), BF16),
        compiler_params=_params("parallel"),
    )(gates, *acts, *weights)


def _conformer_kernel(u_ref, w_ref, b_ref, lg_ref, lb_ref, o_ref, sh_ref, *, seg, nseg):
    pad = 16
    first = pad - (CONV_WIDTH - 1) // 2
    span = seg + 2 * pad - SUBLANE
    zeros = jnp.zeros((pad, CONV_DIM), F32)
    sh_ref[0, 0:pad, :] = zeros
    sh_ref[0, pad + seg:2 * pad + seg, :] = zeros

    def body(s, carry):
        r0 = pl.multiple_of(s * seg, seg)
        u = u_ref[pl.ds(r0, seg), :]
        sh_ref[0, pad:pad + seg, :] = u[:, :CONV_DIM] * _sigmoid(u[:, CONV_DIM:])
        for r in range(1, SUBLANE):
            sh_ref[r, 0:span, :] = sh_ref[0, pl.ds(r, span), :]
        acc = jnp.zeros((seg, CONV_DIM), F32) + b_ref[...]
        for j in range(CONV_WIDTH):
            start = first + j
            acc = acc + w_ref[j:j + 1, :] * sh_ref[start % SUBLANE, pl.ds(start - start % SUBLANE, seg), :]
        mu = jnp.mean(acc, axis=-1, keepdims=True)
        dv = acc - mu
        var = jnp.mean(dv * dv, axis=-1, keepdims=True)
        v = dv * lax.rsqrt(var + LN_EPS) * lg_ref[...] + lb_ref[...]
        o_ref[pl.ds(r0, seg), :] = (v * _sigmoid(v)).astype(o_ref.dtype)
        return carry

    lax.fori_loop(0, nseg, body, 0)


def _conformer(u, row0, nrows, seg, conv_w, conv_b, ln_g, ln_b):
    tq = max(seg, min(256, nrows))
    blk0 = row0 // tq
    return pl.pallas_call(
        functools.partial(_conformer_kernel, seg=seg, nseg=tq // seg),
        grid=(nrows // tq,),
        in_specs=[pl.BlockSpec((tq, U_CONV_W), lambda i: (blk0 + i, U_CONV // U_CONV_W)),
                  _const_spec(conv_w.shape), _const_spec((1, CONV_DIM)),
                  _const_spec((1, CONV_DIM)), _const_spec((1, CONV_DIM))],
        out_specs=pl.BlockSpec((tq, CONV_DIM), lambda i: (i, 0)),
        out_shape=jax.ShapeDtypeStruct((nrows, CONV_DIM), BF16),
        scratch_shapes=[pltpu.VMEM((SUBLANE, seg + 32, CONV_DIM), F32)],
        compiler_params=_params("parallel"),
    )(u, conv_w, conv_b, ln_g, ln_b)


def _dft_kernel(c_ref, s_ref, z_ref, o_ref):
    o_ref[...] = (jnp.dot(c_ref[...], z_ref[:, :FOURIER_DIM], preferred_element_type=F32)
                  + jnp.dot(s_ref[...], z_ref[:, FOURIER_DIM:], preferred_element_type=F32)).astype(o_ref.dtype)


def _dft_mats(n, scale):
    idx = np.arange(n, dtype=np.int64)
    ang = ((idx[:, None] * idx[None, :]) % n).astype(np.float64) * (2.0 * math.pi / n)
    return (jnp.asarray((np.cos(ang) * scale).astype(np.float32)),
            jnp.asarray((np.sin(ang) * scale).astype(np.float32)))


def _fourier(z, row0, nseq, seqlen):
    nrows = nseq * seqlen
    cl, sl = _dft_mats(seqlen, seqlen ** -0.5)
    cl, sl = cl.astype(BF16), sl.astype(BF16)
    tm = _tile(seqlen, 512)
    nt = seqlen // tm
    sblk0 = row0 // seqlen
    return pl.pallas_call(
        _dft_kernel,
        grid=(nt, nseq),
        in_specs=[pl.BlockSpec((tm, seqlen), lambda i, b: (i, 0)),
                  pl.BlockSpec((tm, seqlen), lambda i, b: (i, 0)),
                  pl.BlockSpec((seqlen, 2 * FOURIER_DIM), lambda i, b: (sblk0 + b, 0))],
        out_specs=pl.BlockSpec((tm, FOURIER_DIM), lambda i, b: (b * nt + i, 0)),
        out_shape=jax.ShapeDtypeStruct((nrows, FOURIER_DIM), BF16),
        compiler_params=_params("parallel", "arbitrary"),
    )(cl, sl, z)


def _halo_specs(tq, width, blk0, col_blk, nblk_total):
    hb = tq // SUBLANE
    cur = pl.BlockSpec((tq, width), lambda i: (blk0 + i, col_blk))
    prev = pl.BlockSpec((SUBLANE, width), lambda i: (jnp.maximum((blk0 + i) * hb - 1, 0), col_blk))
    nxt = pl.BlockSpec((SUBLANE, width),
                       lambda i: (jnp.minimum((blk0 + i + 1) * hb, nblk_total - 1), col_blk))
    return cur, prev, nxt


def _fill_halo(scr, cur_ref, prev_ref, next_ref, tq, tiles_per_seq):
    i = pl.program_id(0)
    first = (i % tiles_per_seq) == 0
    last = (i % tiles_per_seq) == tiles_per_seq - 1
    scr[0:SUBLANE, :] = jnp.where(first, 0.0, prev_ref[...])
    scr[SUBLANE:SUBLANE + tq, :] = cur_ref[...]
    scr[SUBLANE + tq:2 * SUBLANE + tq, :] = jnp.where(last, 0.0, next_ref[...])


def _ssd_conv_kernel(cur_ref, prev_ref, next_ref, w_ref, b_ref, xs_ref, bm_ref, cm_ref, scr, *, tq, tiles_per_seq):
    _fill_halo(scr, cur_ref, prev_ref, next_ref, tq, tiles_per_seq)
    half = (SSD_CONV - 1) // 2
    cw = 256
    for c0 in range(0, SSD_XBC, cw):
        acc = jnp.zeros((tq, cw), F32) + b_ref[:, c0:c0 + cw]
        for j in range(SSD_CONV):
            acc = acc + w_ref[j:j + 1, c0:c0 + cw] * scr[pl.ds(SUBLANE - half + j, tq), c0:c0 + cw]
        y = acc * _sigmoid(acc)
        if c0 < SSD_DIM:
            xs_ref[:, c0:c0 + cw] = y
        elif c0 < SSD_DIM + SSD_BC:
            bm_ref[:, c0 - SSD_DIM:c0 - SSD_DIM + cw] = y
        else:
            cm_ref[:, c0 - SSD_DIM - SSD_BC:c0 - SSD_DIM - SSD_BC + cw] = y


def _ssd_conv(u, row0, nseq, seqlen, conv_w, conv_b):
    nrows = nseq * seqlen
    tq = _tile(seqlen, 256)
    cur, prev, nxt = _halo_specs(tq, U_XBC_W, row0 // tq, U_XBC // U_XBC_W, u.shape[0] // SUBLANE)
    outs = [jax.ShapeDtypeStruct((nrows, w), F32) for w in (SSD_DIM, SSD_BC, SSD_BC)]
    return pl.pallas_call(
        functools.partial(_ssd_conv_kernel, tq=tq, tiles_per_seq=seqlen // tq),
        grid=(nrows // tq,),
        in_specs=[cur, prev, nxt, _const_spec(conv_w.shape), _const_spec((1, SSD_XBC))],
        out_specs=[pl.BlockSpec((tq, w), lambda i: (i, 0)) for w in (SSD_DIM, SSD_BC, SSD_BC)],
        out_shape=outs,
        scratch_shapes=[pltpu.VMEM((tq + 2 * SUBLANE, U_XBC_W), F32)],
        compiler_params=_params("parallel"),
    )(u, u, u, conv_w, conv_b)


def _ssd_scan_kernel(xs_ref, b_ref, c_ref, dt_ref, bias_ref, alog_ref, sel_ref, s0_ref, *rest,
                     q, sub, rev, direction, need_y):
    if need_y:
        y_ref, sfin_ref, st = rest
    else:
        sfin_ref, st = rest

    @pl.when(pl.program_id(1) == 0)
    def _():
        st[...] = s0_ref[...]

    hp, gw = SSD_HEAD_DIM, SSD_HEADS_PER_GROUP * SSD_HEAD_DIM
    ti = lax.broadcasted_iota(jnp.int32, (q, q), 0)
    tk = lax.broadcasted_iota(jnp.int32, (q, q), 1)
    before = (tk >= ti) if rev else (tk <= ti)
    tri = jnp.where(before, 1.0, 0.0).astype(BF16)
    neg_a = -jnp.exp(alog_ref[...])
    sel = sel_ref[...]
    spread = lambda parts: sum(jnp.dot(p, sel, preferred_element_type=F32) for p in parts)
    lane0 = lax.broadcasted_iota(jnp.int32, (q, LANE), 1) < hp

    for step in range(sub):
        rows = slice((sub - 1 - step) * q if rev else step * q, (sub - step) * q if rev else (step + 1) * q)
        dt = _softplus(dt_ref[rows, :] + bias_ref[...])
        cs = _dot_exact_lhs(tri, dt * neg_a)
        dt_b = spread(_split3(dt)[:2])
        cs_b = spread(_split3(cs))
        tot_b = cs_b[0:1, :] if rev else cs_b[q - 1:q, :]
        xd = xs_ref[rows, :] * dt_b
        xdd = xd * jnp.exp(tot_b - cs_b)
        dec = jnp.exp(tot_b)
        if need_y:
            ecs = jnp.exp(cs_b)
            cs_t = cs.T
        for g in range(SSD_GROUPS):
            cg = c_ref[rows, g * SSD_STATE:(g + 1) * SSD_STATE]
            bg = b_ref[rows, g * SSD_STATE:(g + 1) * SSD_STATE]
            state = st[g]
            if need_y:
                cb = _dot_nt(cg, bg)
                y_off = _dot(cg, state) * ecs[:, g * gw:(g + 1) * gw]
                for j in range(SSD_HEADS_PER_GROUP):
                    h = g * SSD_HEADS_PER_GROUP + j
                    col = direction * SSD_HEADS + h
                    pair = cs_b[:, (h // 2) * LANE:(h // 2 + 1) * LANE]
                    swapped = pltpu.roll(pair, hp, axis=1)
                    cs_col = jnp.where(lane0, pair, swapped) if h % 2 == 0 else jnp.where(lane0, swapped, pair)
                    lm = jnp.where(before, jnp.exp(jnp.minimum(cs_col - cs_t[col:col + 1, :], 0.0)), 0.0)
                    y_ref[rows, h * hp:(h + 1) * hp] = (_dot(cb * lm, xd[:, h * hp:(h + 1) * hp])
                                                        + y_off[:, j * hp:(j + 1) * hp])
            st[g] = state * dec[:, g * gw:(g + 1) * gw] + _dot(bg.T, xdd[:, g * gw:(g + 1) * gw])
    sfin_ref[...] = st[...]


def _ssd_scan(xs, bm, cm, u, row0, nseq, seqlen, dt_bias, a_log, s0, rev, need_y):
    q = SSD_CHUNK
    sub = max(s for s in (4, 2, 1) if seqlen % (s * q) == 0)
    bq = sub * q
    nc = seqlen // bq
    ublk0 = row0 // bq
    pos = (lambda c: nc - 1 - c) if rev else (lambda c: c)
    row = lambda b, c: (b * nc + pos(c), 0)
    sshape = (SSD_GROUPS, SSD_STATE, SSD_HEADS_PER_GROUP * SSD_HEAD_DIM)
    sspec = pl.BlockSpec((None,) + sshape, lambda b, c: (b, 0, 0, 0))
    head_col = int(rev) * SSD_HEADS + jnp.arange(SSD_DIM, dtype=jnp.int32) // SSD_HEAD_DIM
    sel = (jnp.arange(U_DT_W, dtype=jnp.int32)[:, None] == head_col[None, :]).astype(BF16)
    out_specs, out_shape = [sspec], [jax.ShapeDtypeStruct((nseq,) + sshape, F32)]
    if need_y:
        out_specs.insert(0, pl.BlockSpec((bq, SSD_DIM), row))
        out_shape.insert(0, jax.ShapeDtypeStruct((nseq * seqlen, SSD_DIM), F32))
    res = pl.pallas_call(
        functools.partial(_ssd_scan_kernel, q=q, sub=sub, rev=rev, direction=int(rev), need_y=need_y),
        grid=(nseq, nc),
        in_specs=[pl.BlockSpec((bq, SSD_DIM), row), pl.BlockSpec((bq, SSD_BC), row),
                  pl.BlockSpec((bq, SSD_BC), row),
                  pl.BlockSpec((bq, U_DT_W), lambda b, c: (ublk0 + b * nc + pos(c), U_DT // U_DT_W)),
                  _const_spec((1, U_DT_W)), _const_spec((1, U_DT_W)), _const_spec(sel.shape), sspec],
        out_specs=out_specs, out_shape=out_shape,
        scratch_shapes=[pltpu.VMEM(sshape, F32)],
        compiler_params=_params("parallel", "arbitrary"),
    )(xs, bm, cm, u, dt_bias, a_log, sel, s0)
    return res if need_y else (None, res[0])


def _ssd_fin_kernel(yf_ref, yb_ref, xs_ref, z_ref, d_ref, g_ref, o_ref):
    z = z_ref[...]
    y = (yf_ref[...] + yb_ref[...] + d_ref[...] * xs_ref[...]) * (z * _sigmoid(z))
    o_ref[...] = _rms(y, g_ref[...]).astype(o_ref.dtype)


def _ssd_finish(yf, yb, xs, u, row0, d_skip, norm_g):
    nrows = yf.shape[0]
    tq = _tile(nrows, 512)
    blk0 = row0 // tq
    tok = pl.BlockSpec((tq, SSD_DIM), lambda i: (i, 0))
    return pl.pallas_call(
        _ssd_fin_kernel,
        grid=(nrows // tq,),
        in_specs=[tok, tok, tok, pl.BlockSpec((tq, U_Z_W), lambda i: (blk0 + i, U_Z // U_Z_W)),
                  _const_spec((1, SSD_DIM)), _const_spec((1, SSD_DIM))],
        out_specs=tok,
        out_shape=jax.ShapeDtypeStruct((nrows, SSD_DIM), BF16),
        compiler_params=_params("parallel"),
    )(yf, yb, xs, u, d_skip, norm_g)


def _rwkv_prep_kernel(cur_ref, prev_ref, next_ref, mu_ref, w0_ref, w2_ref, a0_ref, a2_ref, g2_ref,
                      kk_ref, ka_ref, rk_ref, bd_ref,
                      p_o, g_o, bonus_o, scr, *, tq, tiles_per_seq, colmajor):
    r_o, lwf_o, lwb_o, k_o, v_o, kk_o, ka_o = (
        p_o.at[:, c * RWKV_DIM:(c + 1) * RWKV_DIM] for c in (RP_R, RP_LWF, RP_LWB, RP_K, RP_V, RP_KK, RP_KA))
    if colmajor:
        i = pl.program_id(0)
        first = (i % tiles_per_seq) == 0
        last = (i % tiles_per_seq) == tiles_per_seq - 1
        u = jnp.swapaxes(cur_ref[...], 0, 1).reshape(tq, cur_ref.shape[-1])
        scr[SUBLANE:SUBLANE + tq, :] = u
        scr[SUBLANE - 1:SUBLANE, :] = jnp.where(first, 0.0, prev_ref[SUBLANE - 1, SUBLANE - 1:SUBLANE, :])
        scr[SUBLANE + tq:SUBLANE + tq + 1, :] = jnp.where(last, 0.0, next_ref[0, 0:1, :])
    else:
        _fill_halo(scr, cur_ref, prev_ref, next_ref, tq, tiles_per_seq)
        u = cur_ref[...]
    shift = 0.5 * (scr[pl.ds(SUBLANE - 1, tq), :] + scr[pl.ds(SUBLANE + 1, tq), :]) - u
    x = u + shift * mu_ref[...]
    r = x[:, 0:RWKV_DIM]
    k = x[:, RWKV_DIM:2 * RWKV_DIM]
    v = x[:, 2 * RWKV_DIM:3 * RWKV_DIM]
    bd = bd_ref[...]
    for d, (off, out) in enumerate(((RW_WF, lwf_o), (RW_WB, lwb_o))):
        lo = jnp.tanh(x[:, off:off + LANE])
        wl = -_softplus(-(w0_ref[d:d + 1, :] + _dot(lo, w2_ref[d]))) - 0.5
        out[...] = -jnp.exp(wl)
    a = _sigmoid(a0_ref[...] + _dot(x[:, RW_AL:RW_AL + LANE], a2_ref[...]))
    kk = k * kk_ref[...]
    kk = kk * lax.rsqrt(_blocksum(kk * kk, bd) + 1e-12)
    kmod = k * (1.0 + (a - 1.0) * ka_ref[...])
    r_o[...] = r
    k_o[...] = kmod
    v_o[...] = v
    kk_o[...] = kk
    ka_o[...] = kk * a
    g_o[...] = _dot(_sigmoid(x[:, RW_GL:RW_GL + GATE_LORA]), g2_ref[...])
    bonus_o[...] = _blocksum(r * kmod * rk_ref[...], bd) * v


def _rwkv_prep(u, row0, nseq, seqlen, p, grid_rows=None):
    nrows = nseq * seqlen
    colmajor = grid_rows is not None
    if colmajor:
        assert row0 == 0 and grid_rows % SUBLANE == 0 and seqlen == grid_rows * GRID_W
        ncg, rb = GRID_W // SUBLANE, grid_rows // SUBLANE
        tq = grid_rows * SUBLANE
        u = u.reshape(u.shape[0] // GRID_W, GRID_W, U_RWKV_W)
        halo = (SUBLANE, SUBLANE, U_RWKV_W)
        cur = pl.BlockSpec((grid_rows, SUBLANE, U_RWKV_W), lambda i: (i // ncg, i % ncg, 0))
        prev = pl.BlockSpec(halo, lambda i: ((i // ncg) * rb + rb - 1, jnp.maximum(i % ncg - 1, 0), 0))
        nxt = pl.BlockSpec(halo, lambda i: ((i // ncg) * rb, jnp.minimum(i % ncg + 1, ncg - 1), 0))
    else:
        tq = _tile(seqlen, 256)
        cur, prev, nxt = _halo_specs(tq, U_RWKV_W, row0 // tq, 0, u.shape[0] // SUBLANE)
    consts = [p["mu"], p["w0"], p["w2"], p["a0"], p["a2"], p["g2"], p["k_k"], p["k_a"], p["r_k"], p["bd"]]
    tok = pl.BlockSpec((tq, RWKV_DIM), lambda i: (i, 0))
    return pl.pallas_call(
        functools.partial(_rwkv_prep_kernel, tq=tq, tiles_per_seq=seqlen // tq, colmajor=colmajor),
        grid=(nrows // tq,),
        in_specs=[cur, prev, nxt] + [_const_spec(c.shape) for c in consts],
        out_specs=[pl.BlockSpec((tq, RP_N * RWKV_DIM), lambda i: (i, 0)), tok, tok],
        out_shape=[jax.ShapeDtypeStruct((nrows, RP_N * RWKV_DIM), F32)]
        + [jax.ShapeDtypeStruct((nrows, RWKV_DIM), F32)] * 2,
        scratch_shapes=[pltpu.VMEM((tq + 2 * SUBLANE, U_RWKV_W), F32)],
        compiler_params=_params("parallel"),
    )(u, u, u, *consts)


def _rwkv_scan_kernel(*refs, C, sub, need_y):
    packed = refs[0:2]
    s0_refs = refs[2:4]
    if need_y:
        y_refs, sfin_refs, st = refs[4:6], refs[6:8], refs[8]
    else:
        sfin_refs, st = refs[4:6], refs[6]

    @pl.when(pl.program_id(1) == 0)
    def _():
        st[0] = s0_refs[0][...]
        st[1] = s0_refs[1][...]

    n2 = 2 * C
    row = lax.broadcasted_iota(jnp.int32, (n2, n2), 0)
    col = lax.broadcasted_iota(jnp.int32, (n2, n2), 1)
    t, s = row & (C - 1), col & (C - 1)
    strict = ((s < t), (s > t))
    incl = ((s <= t), (s >= t))
    eye = jnp.where(row == col, 1.0, 0.0)
    ti = lax.broadcasted_iota(jnp.int32, (C, C), 0)
    tk = lax.broadcasted_iota(jnp.int32, (C, C), 1)
    tri = (jnp.where(tk <= ti, 1.0, 0.0).astype(BF16), jnp.where(tk >= ti, 1.0, 0.0).astype(BF16))
    head0 = lax.broadcasted_iota(jnp.int32, (C, LANE), 1) < RWKV_HEAD_DIM

    def stack(x):
        return jnp.concatenate([jnp.where(head0, x, 0.0), jnp.where(head0, 0.0, x)], axis=0)

    probs = [(d, p) for d in (0, 1) for p in range(RWKV_PAIRS)]
    blocks = ((RP_R, RP_LWF, RP_K, RP_V, RP_KK, RP_KA), (RP_R, RP_LWB, RP_K, RP_V, RP_KK, RP_KA))
    for step in range(sub):
        _rwkv_chunk(packed, y_refs if need_y else None, st, ((step * C), (sub - 1 - step) * C), C, probs, blocks,
                    strict, incl, eye, tri, stack)
    sfin_refs[0][...] = st[0]
    sfin_refs[1][...] = st[1]


def _rwkv_chunk(packed, y_refs, st, offs, C, probs, blocks, strict, incl, eye, tri, stack):
    n2 = 2 * C
    need_y = y_refs is not None
    ld = lambda d, p, i: packed[d][offs[d]:offs[d] + C,
                                   blocks[d][i] * RWKV_DIM + p * LANE:blocks[d][i] * RWKV_DIM + (p + 1) * LANE]
    lw = [ld(d, p, 1) for d, p in probs]
    cs = [_dot_exact_lhs(tri[d], x) for (d, p), x in zip(probs, lw)]
    tot = [x[C - 1:C, :] if d == 0 else x[0:1, :] for (d, p), x in zip(probs, cs)]
    ems = [jnp.exp(-x) for x in cs]
    edec = [jnp.exp(a - x) for a, x in zip(tot, cs)]
    kt = [stack(ld(d, p, 4) * jnp.exp(x - w)) for (d, p), x, w in zip(probs, cs, lw)]
    rt = [stack(ld(d, p, 0) * jnp.exp(x)) for (d, p), x in zip(probs, cs)]
    at = [stack(ld(d, p, 5) * e) for (d, p), e in zip(probs, ems)]
    kb = [stack(ld(d, p, 2) * e) for (d, p), e in zip(probs, ems)]
    at2 = [stack(ld(d, p, 5) * e) for (d, p), e in zip(probs, edec)]
    kb2 = [stack(ld(d, p, 2) * e) for (d, p), e in zip(probs, edec)]
    vs_t = [stack(ld(d, p, 3)).T for d, p in probs]
    big = [_dot_nt(jnp.concatenate([a, b], axis=0), jnp.concatenate([c, e], axis=0))
           for a, b, c, e in zip(kt, rt, at, kb)]
    npow = [-jnp.where(strict[d], x[:n2, :n2], 0.0) for (d, p), x in zip(probs, big)]
    tinv = [eye + x for x in npow]
    npow = [_dot(x, x) for x in npow]
    for step in range(int(math.log2(C)) - 1):
        if step < int(math.log2(C)) - 2:
            both = [_dot(jnp.concatenate([t, x], axis=0), x) for t, x in zip(tinv, npow)]
            tinv = [t + b[:n2] for t, b in zip(tinv, both)]
            npow = [b[n2:] for b in both]
        else:
            tinv = [t + _dot(t, x) for t, x in zip(tinv, npow)]
    state = [st[d, p] for d, p in probs]
    lhs = [jnp.concatenate(
        [jnp.concatenate([a, jnp.where(strict[d], x[:n2, n2:], 0.0)], axis=1),
         jnp.concatenate([b, jnp.where(incl[d], x[n2:, n2:], 0.0)], axis=1)], axis=0)
        for (d, p), x, a, b in zip(probs, big, kt, rt)]
    xy = [_dot_nt(a, jnp.concatenate([x, v], axis=1)) for a, x, v in zip(lhs, state, vs_t)]
    u_m = [-_dot(x, y[:n2]) for x, y in zip(tinv, xy)]
    if need_y:
        for (d, p), x, u, y in zip(probs, big, u_m, xy):
            yy = y[n2:] + _dot(jnp.where(incl[d], x[n2:, :n2], 0.0), u)
            y_refs[d][offs[d]:offs[d] + C, p * LANE:(p + 1) * LANE] = yy[:C] + yy[C:]
    for (d, p), x, a, u, v, k2, a2 in zip(probs, state, tot, u_m, vs_t, kb2, at2):
        st[d, p] = x * jnp.exp(a) + _dot(jnp.concatenate([u.T, v], axis=1), jnp.concatenate([a2, k2], axis=0))


def _rwkv_scan(packed, nseq, seqlen, s0f, s0b, need_y):
    C = RWKV_CHUNK
    sub = max(s for s in (4, 2, 1) if seqlen % (s * C) == 0)
    nc = seqlen // (sub * C)
    fwd, bwd = (lambda b, c: (b * nc + c, 0)), (lambda b, c: (b * nc + nc - 1 - c, 0))
    tok_f, tok_b = pl.BlockSpec((sub * C, RWKV_DIM), fwd), pl.BlockSpec((sub * C, RWKV_DIM), bwd)
    pk_f = pl.BlockSpec((sub * C, RP_N * RWKV_DIM), fwd)
    pk_b = pl.BlockSpec((sub * C, RP_N * RWKV_DIM), bwd)
    sshape = (RWKV_PAIRS, LANE, LANE)
    sspec = pl.BlockSpec((None,) + sshape, lambda b, c: (b, 0, 0, 0))
    sstruct = jax.ShapeDtypeStruct((nseq,) + sshape, F32)
    out_specs, out_shape = [sspec, sspec], [sstruct, sstruct]
    if need_y:
        ystruct = jax.ShapeDtypeStruct((nseq * seqlen, RWKV_DIM), F32)
        out_specs, out_shape = [tok_f, tok_b] + out_specs, [ystruct, ystruct] + out_shape
    res = pl.pallas_call(
        functools.partial(_rwkv_scan_kernel, C=C, sub=sub, need_y=need_y),
        grid=(nseq, nc),
        in_specs=[pk_f, pk_b, sspec, sspec],
        out_specs=out_specs, out_shape=out_shape,
        scratch_shapes=[pltpu.VMEM((2,) + sshape, F32)],
        compiler_params=_params("parallel", "arbitrary"),
    )(packed, packed, s0f, s0b)
    return tuple(res) if need_y else (None, None) + tuple(res)


def _rwkv_fin_kernel(yf_ref, yb_ref, bonus_ref, g_ref, lg_ref, lb_ref, bd_ref, o_ref):
    y = yf_ref[...] + yb_ref[...]
    bd = bd_ref[...]
    dv = y - _blocksum(y, bd) * (1.0 / RWKV_HEAD_DIM)
    var = _blocksum(dv * dv, bd) * (1.0 / RWKV_HEAD_DIM)
    yn = dv * lax.rsqrt(var + GN_EPS) * lg_ref[...] + lb_ref[...]
    out = (yn + bonus_ref[...]) * g_ref[...]
    if len(o_ref.shape) == 3:
        grid_rows, ncols, c = o_ref.shape
        out = jnp.swapaxes(out.reshape(ncols, grid_rows, c), 0, 1)
    o_ref[...] = out.astype(o_ref.dtype)


def _rwkv_finish(yf, yb, bonus, g, ln_g, ln_b, bd, grid_rows=None):
    nrows = yf.shape[0]
    if grid_rows is None:
        tq = _tile(nrows, 512)
        out_spec = pl.BlockSpec((tq, RWKV_DIM), lambda i: (i, 0))
        out_shape = jax.ShapeDtypeStruct((nrows, RWKV_DIM), BF16)
    else:
        ncols = 16
        tq, ncg = ncols * grid_rows, GRID_W // ncols
        out_spec = pl.BlockSpec((grid_rows, ncols, RWKV_DIM), lambda i: (i // ncg, i % ncg, 0))
        out_shape = jax.ShapeDtypeStruct((nrows // GRID_W, GRID_W, RWKV_DIM), BF16)
    tok = pl.BlockSpec((tq, RWKV_DIM), lambda i: (i, 0))
    res = pl.pallas_call(
        _rwkv_fin_kernel,
        grid=(nrows // tq,),
        in_specs=[tok] * 4 + [_const_spec((1, RWKV_DIM))] * 2 + [_const_spec(bd.shape)],
        out_specs=out_spec,
        out_shape=out_shape,
        compiler_params=_params("parallel"),
    )(yf, yb, bonus, g, ln_g, ln_b, bd)
    return res.reshape(nrows, RWKV_DIM)


def _pad_cols(w, n):
    return jnp.pad(w, [(0, 0)] * (w.ndim - 1) + [(0, n - w.shape[-1])])


def _pad_rows(w, n):
    return jnp.pad(w, [(0, 0)] * (w.ndim - 2) + [(0, n - w.shape[-2]), (0, 0)])


def kernel(x, c, ctx, c_ctx, mod_w, mod_b, norm_g, w_in, conv_w, conv_b, conv_ln_g, conv_ln_b, conv_out, ssd_conv_w, ssd_conv_b, ssd_A_log, ssd_dt_bias, ssd_D, ssd_norm_g, ssd_out, fourier_out, rwkv_mu, rwkv_w0, rwkv_w2, rwkv_a0, rwkv_a2, rwkv_g2, rwkv_k_k, rwkv_k_a, rwkv_r_k, rwkv_ln_g, rwkv_ln_b, rwkv_out, w_o, mlp_up, mlp_down):
    B, L, D = x.shape
    CL = ctx.shape[1]
    depth = mod_w.shape[0]
    rows = L // GRID_W
    n_lat, n_ctx = B * L, B * CL
    n_all = n_lat + n_ctx
    tm = _tile(math.gcd(L, n_ctx), 1024)
    tm_small = _tile(tm, 512)
    tm_merge = _tile(tm, 256)
    mod_idx = lambda t: (lambda i: jnp.minimum((i * t) // L, B))

    w_u, w_rw, w_gate = _split_w_in(w_in)
    down_w, wo_w = mlp_down.astype(BF16), w_o.astype(BF16)
    branch_w = [conv_out.astype(BF16), ssd_out.astype(BF16), fourier_out.astype(BF16), rwkv_out.astype(BF16)]
    mu = rwkv_mu
    mu_cols = jnp.concatenate(
        [mu[:, :3 * RWKV_DIM]] + [_pad_cols(mu[:, a:b], LANE) for a, b in ((1536, 1600), (1600, 1664), (1664, 1728))]
        + [mu[:, 1728:]], axis=1)
    head_of = jnp.arange(RWKV_DIM, dtype=jnp.int32) // RWKV_HEAD_DIM
    bd = (head_of[:, None] == head_of[None, :]).astype(BF16)
    gidx = jnp.arange(FOURIER_DIM, dtype=jnp.int32)
    cc, sc_ = _dft_mats(FOURIER_GROUP_DIM, FOURIER_GROUP_DIM ** -0.5)
    same_group = (gidx[:, None] // FOURIER_GROUP_DIM) == (gidx[None, :] // FOURIER_GROUP_DIM)
    tile_g = lambda m: jnp.where(same_group, jnp.tile(m, (FOURIER_GROUPS, FOURIER_GROUPS)), 0.0)
    chan_w = jnp.concatenate([tile_g(cc), -tile_g(sc_)], axis=1).astype(BF16)
    conv_w_p = _pad_rows(conv_w, 32)
    ssd_conv_w_p = _pad_rows(ssd_conv_w, SUBLANE)
    dt_bias_p = _pad_cols(ssd_dt_bias.reshape(depth, 1, 2 * SSD_HEADS), U_DT_W)
    a_log_p = _pad_cols(ssd_A_log.reshape(depth, 1, 2 * SSD_HEADS), U_DT_W)
    d_skip = jnp.repeat(ssd_D, SSD_HEAD_DIM, axis=1).reshape(depth, 1, SSD_DIM)

    cond = jnp.concatenate([c, c_ctx[None, :], jnp.zeros((SUBLANE - (B + 1) % SUBLANE, D), F32)], axis=0)
    mod = _mod_all(cond, mod_w, mod_b)

    xs = jnp.concatenate([x.reshape(n_lat, D), ctx.reshape(n_ctx, D)], axis=0)
    ssd_zero = jnp.zeros((B, SSD_GROUPS, SSD_STATE, SSD_HEADS_PER_GROUP * SSD_HEAD_DIM), F32)
    rwkv_zero = jnp.zeros((B, RWKV_PAIRS, LANE, LANE), F32)

    mods = lambda l: tuple(mod[l, :, j * D:(j + 1) * D].reshape(-1, 1, D) for j in range(6))
    h = _norm_mod(xs, norm_g[0, 0].reshape(1, D), *mods(0)[1::-1], n_all, mod_idx(tm), tm)
    for i in range(depth):
        last = i == depth - 1
        m_out = n_lat if last else n_all
        sh1, sc1, g1, sh2, sc2, g2 = mods(i)
        ng = norm_g[i].reshape(4, 1, D)

        u = _matmul(h, w_u, n_all, tm, F32, tn_cap=1536, layer=i)
        gates = _matmul(h, w_gate, m_out, tm, BF16, epilogue="sigmoid", tn_cap=2048, layer=i)

        a_conv = _conformer(u, 0, n_lat, GRID_W, conv_w_p[i], conv_b[i][None], conv_ln_g[i][None], conv_ln_b[i][None])
        z = _matmul(u, chan_w, m_out, tm, BF16, a_col=U_FFT // U_FFT_W)
        a_fft = _fourier(z, 0, B, L)
        if not last:
            a_conv = jnp.concatenate(
                [a_conv, _conformer(u, n_lat, n_ctx, CL, conv_w_p[i], conv_b[i][None], conv_ln_g[i][None],
                                    conv_ln_b[i][None])], axis=0)
            a_fft = jnp.concatenate([a_fft, _fourier(z, n_lat, B, CL)], axis=0)

        ssd_args = (dt_bias_p[i], a_log_p[i])
        xs_c, bm_c, cm_c = _ssd_conv(u, n_lat, B, CL, ssd_conv_w_p[i], ssd_conv_b[i][None])
        yfc, sf = _ssd_scan(xs_c, bm_c, cm_c, u, n_lat, B, CL, *ssd_args, ssd_zero, False, not last)
        ybc, sb = _ssd_scan(xs_c, bm_c, cm_c, u, n_lat, B, CL, *ssd_args, ssd_zero, True, not last)
        xs_l, bm_l, cm_l = _ssd_conv(u, 0, B, L, ssd_conv_w_p[i], ssd_conv_b[i][None])
        yfl, _ = _ssd_scan(xs_l, bm_l, cm_l, u, 0, B, L, *ssd_args, sf, False, True)
        ybl, _ = _ssd_scan(xs_l, bm_l, cm_l, u, 0, B, L, *ssd_args, sb, True, True)
        a_ssd = _ssd_finish(yfl, ybl, xs_l, u, 0, d_skip[i], ssd_norm_g[i][None])
        if not last:
            a_ssd = jnp.concatenate(
                [a_ssd, _ssd_finish(yfc, ybc, xs_c, u, n_lat, d_skip[i], ssd_norm_g[i][None])], axis=0)

        rp = dict(mu=mu_cols[i][None], w0=rwkv_w0[i], w2=_pad_rows(rwkv_w2[i], LANE).astype(BF16),
                  a0=rwkv_a0[i][None], a2=_pad_rows(rwkv_a2[i], LANE).astype(BF16), g2=rwkv_g2[i].astype(BF16),
                  k_k=rwkv_k_k[i][None], k_a=rwkv_k_a[i][None], r_k=rwkv_r_k[i].reshape(1, RWKV_DIM), bd=bd)
        u_r = _matmul(h, w_rw, n_all, tm, F32, layer=i)
        packed, g_, bonus = _rwkv_prep(u_r, n_lat, B, CL, rp)
        yfc, ybc, s_f, s_b = _rwkv_scan(packed, B, CL, rwkv_zero, rwkv_zero, not last)
        if not last:
            a_rwkv_c = _rwkv_finish(yfc, ybc, bonus, g_, rwkv_ln_g[i][None], rwkv_ln_b[i][None], bd)
        packed, g_, bonus = _rwkv_prep(u_r, 0, B, L, rp, grid_rows=rows)
        yfl, ybl, _, _ = _rwkv_scan(packed, B, L, s_f, s_b, True)
        a_rwkv = _rwkv_finish(yfl, ybl, bonus, g_, rwkv_ln_g[i][None], rwkv_ln_b[i][None], bd, grid_rows=rows)
        if not last:
            a_rwkv = jnp.concatenate([a_rwkv, a_rwkv_c], axis=0)

        m = _merge(gates, [a_conv, a_ssd, a_fft, a_rwkv], branch_w, m_out, tm_merge, i)
        xs, h2 = _matmul_residual(m, wo_w, xs, g1, ng[1], m_out, mod_idx(tm_small), tm_small,
                                  nxt=(ng[2], sc2, sh2), layer=i)

        mid = _matmul_f32w(h2, mlp_up, m_out, tm, BF16, epilogue="relu2", layer=i)
        if last:
            xs = _matmul_residual(mid, down_w, xs, g2, ng[3], m_out, mod_idx(tm_small), tm_small, layer=i)
        else:
            xs, h = _matmul_residual(mid, down_w, xs, g2, ng[3], m_out, mod_idx(tm_small), tm_small,
                                     nxt=(norm_g[i + 1, 0].reshape(1, D), *mods(i + 1)[1::-1]), layer=i)

    return xs[:n_lat].reshape(B, L, D)
```

```python
import functools
import math

import jax
import jax.numpy as jnp
import numpy as np
from jax import lax
from jax.experimental import pallas as pl
from jax.experimental.pallas import tpu as pltpu

F32 = jnp.float32
BF16 = jnp.bfloat16

GRID_W = 64
RMS_EPS = 1e-6
LN_EPS = 1e-5
GN_EPS = 64e-5
CONV_DIM = 512
CONV_WIDTH = 31
SSD_HEADS = 12
SSD_HEAD_DIM = 64
SSD_DIM = SSD_HEADS * SSD_HEAD_DIM
SSD_GROUPS = 4
SSD_HEADS_PER_GROUP = SSD_HEADS // SSD_GROUPS
SSD_STATE = 128
SSD_CONV = 5
SSD_BC = SSD_GROUPS * SSD_STATE
SSD_XBC = SSD_DIM + 2 * SSD_BC
SSD_IN = SSD_DIM + SSD_XBC + 2 * SSD_HEADS
SSD_CHUNK = 128
FOURIER_GROUPS = 4
FOURIER_GROUP_DIM = 128
FOURIER_DIM = FOURIER_GROUPS * FOURIER_GROUP_DIM
RWKV_HEADS = 8
RWKV_HEAD_DIM = 64
RWKV_DIM = RWKV_HEADS * RWKV_HEAD_DIM
RWKV_PAIRS = RWKV_HEADS // 2
DECAY_LORA = 64
ICL_LORA = 64
GATE_LORA = 128
RWKV_IN = 3 * RWKV_DIM + 2 * DECAY_LORA + ICL_LORA + GATE_LORA
RWKV_CHUNK = 64
REC_IN = SSD_IN + RWKV_IN
N_BRANCH = 4

LANE = 128
SUBLANE = 8
VMEM_LIMIT = 56 * 1024 * 1024

U_XBC, U_XBC_W = 0, SSD_XBC
U_DT, U_DT_W = 1792, LANE
U_CONV, U_CONV_W = 2048, 2 * CONV_DIM
U_Z, U_Z_W = 3072, SSD_DIM
U_FFT, U_FFT_W = 4096, FOURIER_DIM
U_N = 4608
U_RWKV_W = 2048
RW_WF, RW_WB, RW_AL, RW_GL = 1536, 1664, 1792, 1920
RP_R, RP_LWF, RP_LWB, RP_K, RP_V, RP_KK, RP_KA, RP_N = 0, 1, 2, 3, 4, 5, 6, 7


def _params(*sem):
    return pltpu.CompilerParams(dimension_semantics=sem, vmem_limit_bytes=VMEM_LIMIT)


def _tile(n, cap, quantum=SUBLANE):
    if n <= cap:
        return n
    t = (cap // quantum) * quantum
    while t >= quantum:
        if n % t == 0:
            return t
        t -= quantum
    raise ValueError(f"no tile for {n} under {cap}")


def _const_spec(shape):
    nd = len(shape)
    return pl.BlockSpec(shape, lambda *_: (0,) * nd)


def _dot(a, b):
    return jnp.dot(a.astype(BF16), b.astype(BF16), preferred_element_type=F32)


def _dot_nt(a, b):
    return lax.dot_general(a.astype(BF16), b.astype(BF16), (((1,), (1,)), ((), ())),
                           preferred_element_type=F32)


def _split3(x):
    h1 = x.astype(BF16)
    r1 = x - h1.astype(F32)
    h2 = r1.astype(BF16)
    h3 = (r1 - h2.astype(F32)).astype(BF16)
    return h1, h2, h3


def _dot_exact_lhs(m, x):
    h1, h2, h3 = _split3(x)
    d = lambda h: jnp.dot(m, h, preferred_element_type=F32)
    return d(h1) + d(h2) + d(h3)


def _blocksum(x, bd):
    hi = x.astype(BF16)
    lo = (x - hi.astype(F32)).astype(BF16)
    return (jnp.dot(hi, bd, preferred_element_type=F32) + jnp.dot(lo, bd, preferred_element_type=F32))


def _sigmoid(x):
    return 1.0 / (1.0 + jnp.exp(-x))


def _softplus(x):
    return jnp.maximum(x, 0.0) + jnp.log1p(jnp.exp(-jnp.abs(x)))


def _rms(y, g):
    return y * lax.rsqrt(jnp.mean(y * y, axis=-1, keepdims=True) + RMS_EPS) * g


_O_RWKV, _O_CONV = SSD_IN, REC_IN
_O_FFT, _O_GATE = REC_IN + 2 * CONV_DIM, REC_IN + 2 * CONV_DIM + FOURIER_DIM
_W_IN_PIECES = (
    ("rw", 0, _O_RWKV, 3 * RWKV_DIM), ("rw", RW_WF, _O_RWKV + 1536, DECAY_LORA),
    ("rw", RW_WB, _O_RWKV + 1600, DECAY_LORA), ("rw", RW_AL, _O_RWKV + 1664, ICL_LORA),
    ("rw", RW_GL, _O_RWKV + 1728, GATE_LORA),
    ("u", U_XBC, SSD_DIM, SSD_XBC), ("u", U_DT, SSD_DIM + SSD_XBC, 2 * SSD_HEADS),
    ("u", U_CONV, _O_CONV, 2 * CONV_DIM), ("u", U_Z, 0, SSD_DIM), ("u", U_FFT, _O_FFT, FOURIER_DIM),
)
_RELAYOUT_CHUNK = 1024


def _split_w_in_kernel(w_ref, u_ref, rw_ref, g_ref):
    n_in = w_ref.shape[-1]
    dst = {"u": u_ref, "rw": rw_ref, "g": g_ref}
    u_ref[...] = jnp.zeros_like(u_ref)
    rw_ref[...] = jnp.zeros_like(rw_ref)
    pieces = _W_IN_PIECES + (("g", 0, _O_GATE, g_ref.shape[-1]),)
    for name, dcol, scol, width in pieces:
        for off in range(0, width, _RELAYOUT_CHUNK):
            w = min(_RELAYOUT_CHUNK, width - off)
            lo = scol + off
            lo_al = (lo // LANE) * LANE
            hi_al = min(-(-(lo + w) // LANE) * LANE, n_in)
            x = w_ref[:, lo_al:hi_al]
            dst[name][:, dcol + off:dcol + off + w] = x[:, lo - lo_al:lo - lo_al + w].astype(BF16)


def _split_w_in(w_in):
    depth, d, n_in = w_in.shape
    n_gate = n_in - _O_GATE
    tk = _tile(d, 128)
    outs = (U_N, U_RWKV_W, n_gate)
    return pl.pallas_call(
        _split_w_in_kernel,
        grid=(depth, d // tk),
        in_specs=[pl.BlockSpec((None, tk, n_in), lambda l, i: (l, i, 0))],
        out_specs=[pl.BlockSpec((None, tk, n), lambda l, i: (l, i, 0)) for n in outs],
        out_shape=[jax.ShapeDtypeStruct((depth, d, n), BF16) for n in outs],
        compiler_params=_params("parallel", "parallel"),
    )(w_in)


def _mod_kernel(a_ref, w_ref, b_ref, o_ref):
    a = a_ref[...]
    a = a * _sigmoid(a)
    o_ref[...] = _dot(a, w_ref[...]) + b_ref[...]


def _mod_all(cond, mod_w, mod_b):
    depth, d, n = mod_w.shape
    tn = _tile(n, 1024, LANE)
    return pl.pallas_call(
        _mod_kernel,
        grid=(depth, n // tn),
        in_specs=[_const_spec(cond.shape),
                  pl.BlockSpec((None, d, tn), lambda l, j: (l, 0, j)),
                  pl.BlockSpec((None, 1, tn), lambda l, j: (l, 0, j))],
        out_specs=pl.BlockSpec((None, cond.shape[0], tn), lambda l, j: (l, 0, j)),
        out_shape=jax.ShapeDtypeStruct((depth, cond.shape[0], n), F32),
        compiler_params=_params("parallel", "parallel"),
    )(cond, mod_w, mod_b.reshape(depth, 1, n))


def _norm_mod_kernel(x_ref, g_ref, sc_ref, sh_ref, o_ref):
    o_ref[...] = (_rms(x_ref[...], g_ref[...]) * (1.0 + sc_ref[...]) + sh_ref[...]).astype(o_ref.dtype)


def _norm_mod(x, g, sc, sh, m_rows, mod_idx, tm):
    d = x.shape[1]
    return pl.pallas_call(
        _norm_mod_kernel,
        grid=(m_rows // tm,),
        in_specs=[pl.BlockSpec((tm, d), lambda i: (i, 0)),
                  _const_spec((1, d)),
                  pl.BlockSpec((None, 1, d), lambda i: (mod_idx(i), 0, 0)),
                  pl.BlockSpec((None, 1, d), lambda i: (mod_idx(i), 0, 0))],
        out_specs=pl.BlockSpec((tm, d), lambda i: (i, 0)),
        out_shape=jax.ShapeDtypeStruct((m_rows, d), BF16),
        compiler_params=_params("parallel"),
    )(x, g, sc, sh)


def _mm_kernel(a_ref, w_ref, o_ref, *, epilogue):
    acc = _dot(a_ref[...], w_ref[...])
    if epilogue == "relu2":
        acc = jnp.square(jnp.maximum(acc, 0.0))
    elif epilogue == "sigmoid":
        acc = _sigmoid(acc)
    o_ref[...] = acc.astype(o_ref.dtype)


def _layer_spec(w, layer, block, index):
    if w.ndim == 2:
        return pl.BlockSpec(block, index)
    return pl.BlockSpec((None,) + block, lambda *ids: (layer,) + index(*ids))


def _matmul(a, w, m_rows, tm, out_dtype, epilogue=None, a_col=0, a_row=0, tn_cap=1024, layer=0):
    k, n = w.shape[-2:]
    tn = _tile(n, tn_cap, LANE)
    return pl.pallas_call(
        functools.partial(_mm_kernel, epilogue=epilogue),
        grid=(m_rows // tm, n // tn),
        in_specs=[pl.BlockSpec((tm, k), lambda i, j: (a_row + i, a_col)),
                  _layer_spec(w, layer, (k, tn), lambda i, j: (0, j))],
        out_specs=pl.BlockSpec((tm, tn), lambda i, j: (i, j)),
        out_shape=jax.ShapeDtypeStruct((m_rows, n), out_dtype),
        compiler_params=_params("parallel", "arbitrary"),
    )(a, w)


def _mm_wcast_kernel(a_ref, w_ref, o_ref, wb_ref, *, epilogue):
    @pl.when(pl.program_id(1) == 0)
    def _():
        wb_ref[...] = w_ref[...].astype(wb_ref.dtype)

    _mm_kernel(a_ref, wb_ref, o_ref, epilogue=epilogue)


def _matmul_f32w(a, w, m_rows, tm, out_dtype, epilogue=None, tn_cap=1024, layer=0):
    k, n = w.shape[-2:]
    tn = _tile(n, tn_cap, LANE)
    return pl.pallas_call(
        functools.partial(_mm_wcast_kernel, epilogue=epilogue),
        grid=(n // tn, m_rows // tm),
        in_specs=[pl.BlockSpec((tm, k), lambda j, i: (i, 0)),
                  _layer_spec(w, layer, (k, tn), lambda j, i: (0, j))],
        out_specs=pl.BlockSpec((tm, tn), lambda j, i: (i, j)),
        out_shape=jax.ShapeDtypeStruct((m_rows, n), out_dtype),
        scratch_shapes=[pltpu.VMEM((k, tn), BF16)],
        compiler_params=_params("parallel", "arbitrary"),
    )(a, w)


def _mm_res_kernel(a_ref, w_ref, x_ref, gate_ref, g_ref, *rest, with_next, nk):
    if with_next:
        g2_ref, sc_ref, sh_ref, o_ref, h_ref = rest
    else:
        (o_ref,) = rest
    def finish(y):
        xn = x_ref[...] + gate_ref[...] * _rms(y, g_ref[...])
        o_ref[...] = xn
        if with_next:
            h_ref[...] = (_rms(xn, g2_ref[...]) * (1.0 + sc_ref[...]) + sh_ref[...]).astype(h_ref.dtype)

    if nk == 1:
        finish(_dot(a_ref[...], w_ref[...]))
        return
    k = pl.program_id(1)

    @pl.when(k == 0)
    def _():
        o_ref[...] = jnp.zeros_like(o_ref)

    o_ref[...] += _dot(a_ref[...], w_ref[...])

    @pl.when(k == nk - 1)
    def _():
        finish(o_ref[...])


def _matmul_residual(a, w, x, gate, g, m_rows, mod_idx, tm, nxt=None, tk_cap=2048, layer=0):
    k, d = w.shape[-2:]
    tk = _tile(k, tk_cap, LANE)
    mod_spec = pl.BlockSpec((None, 1, d), lambda i, kk: (mod_idx(i), 0, 0))
    tile_spec = pl.BlockSpec((tm, d), lambda i, kk: (i, 0))
    in_specs = [pl.BlockSpec((tm, tk), lambda i, kk: (i, kk)), _layer_spec(w, layer, (tk, d), lambda i, kk: (kk, 0)),
                tile_spec, mod_spec, _const_spec((1, d))]
    out_specs, out_shape, args = tile_spec, jax.ShapeDtypeStruct((m_rows, d), F32), (a, w, x, gate, g)
    if nxt is not None:
        in_specs += [_const_spec((1, d)), mod_spec, mod_spec]
        out_specs, out_shape = [tile_spec, tile_spec], [out_shape, jax.ShapeDtypeStruct((m_rows, d), BF16)]
        args += tuple(nxt)
    return pl.pallas_call(
        functools.partial(_mm_res_kernel, with_next=nxt is not None, nk=k // tk),
        grid=(m_rows // tm, k // tk),
        in_specs=in_specs, out_specs=out_specs, out_shape=out_shape,
        compiler_params=_params("parallel", "arbitrary"),
    )(*args)


def _merge_kernel(gates_ref, a0, a1, a2, a3, w0, w1, w2, w3, o_ref):
    d = o_ref.shape[-1]
    m = None
    for i, (a, w) in enumerate(((a0, w0), (a1, w1), (a2, w2), (a3, w3))):
        t = gates_ref[:, i * d:(i + 1) * d].astype(F32) * _dot(a[...], w[...])
        m = t if m is None else m + t
    o_ref[...] = m.astype(o_ref.dtype)


def _merge(gates, acts, weights, m_rows, tm, layer):
    d = weights[0].shape[-1]
    return pl.pallas_call(
        _merge_kernel,
        grid=(m_rows // tm,),
        in_specs=[pl.BlockSpec((tm, N_BRANCH * d), lambda i: (i, 0))]
        + [pl.BlockSpec((tm, a.shape[1]), lambda i: (i, 0)) for a in acts]
        + [_layer_spec(w, layer, w.shape[-2:], lambda i: (0, 0)) for w in weights],
        out_specs=pl.BlockSpec((tm, d), lambda i: (i, 0)),
        out_shape=jax.ShapeDtypeStruct((m_rows, d), BF16),
        compiler_params=_params("parallel"),
    )(gates, *acts, *weights)


def _conformer_kernel(u_ref, w_ref, b_ref, lg_ref, lb_ref, o_ref, sh_ref, *, seg, nseg):
    pad = 16
    first = pad - (CONV_WIDTH - 1) // 2
    span = seg + 2 * pad - SUBLANE
    zeros = jnp.zeros((pad, CONV_DIM), F32)
    sh_ref[0, 0:pad, :] = zeros
    sh_ref[0, pad + seg:2 * pad + seg, :] = zeros

    def body(s, carry):
        r0 = pl.multiple_of(s * seg, seg)
        u = u_ref[pl.ds(r0, seg), :]
        sh_ref[0, pad:pad + seg, :] = u[:, :CONV_DIM] * _sigmoid(u[:, CONV_DIM:])
        for r in range(1, SUBLANE):
            sh_ref[r, 0:span, :] = sh_ref[0, pl.ds(r, span), :]
        acc = jnp.zeros((seg, CONV_DIM), F32) + b_ref[...]
        for j in range(CONV_WIDTH):
            start = first + j
            acc = acc + w_ref[j:j + 1, :] * sh_ref[start % SUBLANE, pl.ds(start - start % SUBLANE, seg), :]
        mu = jnp.mean(acc, axis=-1, keepdims=True)
        dv = acc - mu
        var = jnp.mean(dv * dv, axis=-1, keepdims=True)
        v = dv * lax.rsqrt(var + LN_EPS) * lg_ref[...] + lb_ref[...]
        o_ref[pl.ds(r0, seg), :] = (v * _sigmoid(v)).astype(o_ref.dtype)
        return carry

    lax.fori_loop(0, nseg, body, 0)


def _rows_call(kernel, args, in_specs, out_spec, out_struct, into, **kw):
    if into is None:
        return pl.pallas_call(kernel, in_specs=in_specs, out_specs=out_spec, out_shape=out_struct, **kw)(*args)
    n = len(args)

    def aliased(*refs):
        kernel(*refs[:n], *refs[n + 1:])

    return pl.pallas_call(aliased, in_specs=list(in_specs) + [pl.BlockSpec(memory_space=pl.ANY)],
                          out_specs=out_spec, out_shape=out_struct, input_output_aliases={n: 0}, **kw)(*args, into)


def _conformer(u, row0, nrows, seg, conv_w, conv_b, ln_g, ln_b, total_rows, into=None):
    tq = max(seg, min(256, nrows))
    blk0 = row0 // tq
    return _rows_call(
        functools.partial(_conformer_kernel, seg=seg, nseg=tq // seg),
        (u, conv_w, conv_b, ln_g, ln_b),
        [pl.BlockSpec((tq, U_CONV_W), lambda i: (blk0 + i, U_CONV // U_CONV_W)),
         _const_spec(conv_w.shape), _const_spec((1, CONV_DIM)),
         _const_spec((1, CONV_DIM)), _const_spec((1, CONV_DIM))],
        pl.BlockSpec((tq, CONV_DIM), lambda i: (blk0 + i, 0)),
        jax.ShapeDtypeStruct((total_rows, CONV_DIM), BF16), into,
        grid=(nrows // tq,),
        scratch_shapes=[pltpu.VMEM((SUBLANE, seg + 32, CONV_DIM), F32)],
        compiler_params=_params("parallel"),
    )


def _dft_kernel(c_ref, s_ref, z_ref, o_ref):
    o_ref[...] = (jnp.dot(c_ref[...], z_ref[:, :FOURIER_DIM], preferred_element_type=F32)
                  + jnp.dot(s_ref[...], z_ref[:, FOURIER_DIM:], preferred_element_type=F32)).astype(o_ref.dtype)


def _dft_mats(n, scale):
    idx = np.arange(n, dtype=np.int64)
    ang = ((idx[:, None] * idx[None, :]) % n).astype(np.float64) * (2.0 * math.pi / n)
    return (jnp.asarray((np.cos(ang) * scale).astype(np.float32)),
            jnp.asarray((np.sin(ang) * scale).astype(np.float32)))


def _fourier(z, row0, nseq, seqlen, total_rows, into=None):
    cl, sl = _dft_mats(seqlen, seqlen ** -0.5)
    cl, sl = cl.astype(BF16), sl.astype(BF16)
    tm = _tile(seqlen, 512)
    nt = seqlen // tm
    sblk0, oblk0 = row0 // seqlen, row0 // tm
    return _rows_call(
        _dft_kernel, (cl, sl, z),
        [pl.BlockSpec((tm, seqlen), lambda i, b: (i, 0)),
         pl.BlockSpec((tm, seqlen), lambda i, b: (i, 0)),
         pl.BlockSpec((seqlen, 2 * FOURIER_DIM), lambda i, b: (sblk0 + b, 0))],
        pl.BlockSpec((tm, FOURIER_DIM), lambda i, b: (oblk0 + b * nt + i, 0)),
        jax.ShapeDtypeStruct((total_rows, FOURIER_DIM), BF16), into,
        grid=(nt, nseq),
        compiler_params=_params("parallel", "arbitrary"),
    )


def _halo_specs(tq, width, blk0, col_blk, nblk_total):
    hb = tq // SUBLANE
    cur = pl.BlockSpec((tq, width), lambda i: (blk0 + i, col_blk))
    prev = pl.BlockSpec((SUBLANE, width), lambda i: (jnp.maximum((blk0 + i) * hb - 1, 0), col_blk))
    nxt = pl.BlockSpec((SUBLANE, width),
                       lambda i: (jnp.minimum((blk0 + i + 1) * hb, nblk_total - 1), col_blk))
    return cur, prev, nxt


def _fill_halo(scr, cur_ref, prev_ref, next_ref, tq, tiles_per_seq):
    i = pl.program_id(0)
    first = (i % tiles_per_seq) == 0
    last = (i % tiles_per_seq) == tiles_per_seq - 1
    scr[0:SUBLANE, :] = jnp.where(first, 0.0, prev_ref[...])
    scr[SUBLANE:SUBLANE + tq, :] = cur_ref[...]
    scr[SUBLANE + tq:2 * SUBLANE + tq, :] = jnp.where(last, 0.0, next_ref[...])


def _ssd_conv_kernel(cur_ref, prev_ref, next_ref, w_ref, b_ref, xs_ref, bm_ref, cm_ref, scr, *, tq, tiles_per_seq):
    _fill_halo(scr, cur_ref, prev_ref, next_ref, tq, tiles_per_seq)
    half = (SSD_CONV - 1) // 2
    cw = 256
    for c0 in range(0, SSD_XBC, cw):
        acc = jnp.zeros((tq, cw), F32) + b_ref[:, c0:c0 + cw]
        for j in range(SSD_CONV):
            acc = acc + w_ref[j:j + 1, c0:c0 + cw] * scr[pl.ds(SUBLANE - half + j, tq), c0:c0 + cw]
        y = acc * _sigmoid(acc)
        if c0 < SSD_DIM:
            xs_ref[:, c0:c0 + cw] = y
        elif c0 < SSD_DIM + SSD_BC:
            bm_ref[:, c0 - SSD_DIM:c0 - SSD_DIM + cw] = y
        else:
            cm_ref[:, c0 - SSD_DIM - SSD_BC:c0 - SSD_DIM - SSD_BC + cw] = y


def _ssd_conv(u, row0, nseq, seqlen, conv_w, conv_b):
    nrows = nseq * seqlen
    tq = _tile(seqlen, 256)
    cur, prev, nxt = _halo_specs(tq, U_XBC_W, row0 // tq, U_XBC // U_XBC_W, u.shape[0] // SUBLANE)
    outs = [jax.ShapeDtypeStruct((nrows, w), F32) for w in (SSD_DIM, SSD_BC, SSD_BC)]
    return pl.pallas_call(
        functools.partial(_ssd_conv_kernel, tq=tq, tiles_per_seq=seqlen // tq),
        grid=(nrows // tq,),
        in_specs=[cur, prev, nxt, _const_spec(conv_w.shape), _const_spec((1, SSD_XBC))],
        out_specs=[pl.BlockSpec((tq, w), lambda i: (i, 0)) for w in (SSD_DIM, SSD_BC, SSD_BC)],
        out_shape=outs,
        scratch_shapes=[pltpu.VMEM((tq + 2 * SUBLANE, U_XBC_W), F32)],
        compiler_params=_params("parallel"),
    )(u, u, u, conv_w, conv_b)


def _ssd_scan_kernel(xs_ref, b_ref, c_ref, dt_ref, bias_ref, alog_ref, sel_ref, s0_ref, *rest,
                     q, sub, rev, direction, need_y):
    if need_y:
        y_ref, sfin_ref, st = rest
    else:
        sfin_ref, st = rest

    @pl.when(pl.program_id(1) == 0)
    def _():
        st[...] = s0_ref[...]

    hp, gw = SSD_HEAD_DIM, SSD_HEADS_PER_GROUP * SSD_HEAD_DIM
    ti = lax.broadcasted_iota(jnp.int32, (q, q), 0)
    tk = lax.broadcasted_iota(jnp.int32, (q, q), 1)
    before = (tk >= ti) if rev else (tk <= ti)
    tri = jnp.where(before, 1.0, 0.0).astype(BF16)
    neg_a = -jnp.exp(alog_ref[...])
    sel = sel_ref[...]
    spread = lambda parts: sum(jnp.dot(p, sel, preferred_element_type=F32) for p in parts)
    lane0 = lax.broadcasted_iota(jnp.int32, (q, LANE), 1) < hp

    for step in range(sub):
        rows = slice((sub - 1 - step) * q if rev else step * q, (sub - step) * q if rev else (step + 1) * q)
        dt = _softplus(dt_ref[rows, :] + bias_ref[...])
        cs = _dot_exact_lhs(tri, dt * neg_a)
        dt_b = spread(_split3(dt)[:2])
        cs_b = spread(_split3(cs))
        tot_b = cs_b[0:1, :] if rev else cs_b[q - 1:q, :]
        xd = xs_ref[rows, :] * dt_b
        xdd = xd * jnp.exp(tot_b - cs_b)
        dec = jnp.exp(tot_b)
        if need_y:
            ecs = jnp.exp(cs_b)
            cs_t = cs.T
        for g in range(SSD_GROUPS):
            cg = c_ref[rows, g * SSD_STATE:(g + 1) * SSD_STATE]
            bg = b_ref[rows, g * SSD_STATE:(g + 1) * SSD_STATE]
            state = st[g]
            if need_y:
                cb = _dot_nt(cg, bg)
                y_off = _dot(cg, state) * ecs[:, g * gw:(g + 1) * gw]
                for j in range(SSD_HEADS_PER_GROUP):
                    h = g * SSD_HEADS_PER_GROUP + j
                    col = direction * SSD_HEADS + h
                    pair = cs_b[:, (h // 2) * LANE:(h // 2 + 1) * LANE]
                    swapped = pltpu.roll(pair, hp, axis=1)
                    cs_col = jnp.where(lane0, pair, swapped) if h % 2 == 0 else jnp.where(lane0, swapped, pair)
                    lm = jnp.where(before, jnp.exp(jnp.minimum(cs_col - cs_t[col:col + 1, :], 0.0)), 0.0)
                    y_ref[rows, h * hp:(h + 1) * hp] = (_dot(cb * lm, xd[:, h * hp:(h + 1) * hp])
                                                        + y_off[:, j * hp:(j + 1) * hp])
            st[g] = state * dec[:, g * gw:(g + 1) * gw] + _dot(bg.T, xdd[:, g * gw:(g + 1) * gw])
    sfin_ref[...] = st[...]


def _ssd_scan(xs, bm, cm, u, row0, nseq, seqlen, dt_bias, a_log, s0, rev, need_y):
    q = SSD_CHUNK
    sub = max(s for s in (4, 2, 1) if seqlen % (s * q) == 0)
    bq = sub * q
    nc = seqlen // bq
    ublk0 = row0 // bq
    pos = (lambda c: nc - 1 - c) if rev else (lambda c: c)
    row = lambda b, c: (b * nc + pos(c), 0)
    sshape = (SSD_GROUPS, SSD_STATE, SSD_HEADS_PER_GROUP * SSD_HEAD_DIM)
    sspec = pl.BlockSpec((None,) + sshape, lambda b, c: (b, 0, 0, 0))
    head_col = int(rev) * SSD_HEADS + jnp.arange(SSD_DIM, dtype=jnp.int32) // SSD_HEAD_DIM
    sel = (jnp.arange(U_DT_W, dtype=jnp.int32)[:, None] == head_col[None, :]).astype(BF16)
    out_specs, out_shape = [sspec], [jax.ShapeDtypeStruct((nseq,) + sshape, F32)]
    if need_y:
        out_specs.insert(0, pl.BlockSpec((bq, SSD_DIM), row))
        out_shape.insert(0, jax.ShapeDtypeStruct((nseq * seqlen, SSD_DIM), F32))
    res = pl.pallas_call(
        functools.partial(_ssd_scan_kernel, q=q, sub=sub, rev=rev, direction=int(rev), need_y=need_y),
        grid=(nseq, nc),
        in_specs=[pl.BlockSpec((bq, SSD_DIM), row), pl.BlockSpec((bq, SSD_BC), row),
                  pl.BlockSpec((bq, SSD_BC), row),
                  pl.BlockSpec((bq, U_DT_W), lambda b, c: (ublk0 + b * nc + pos(c), U_DT // U_DT_W)),
                  _const_spec((1, U_DT_W)), _const_spec((1, U_DT_W)), _const_spec(sel.shape), sspec],
        out_specs=out_specs, out_shape=out_shape,
        scratch_shapes=[pltpu.VMEM(sshape, F32)],
        compiler_params=_params("parallel", "arbitrary"),
    )(xs, bm, cm, u, dt_bias, a_log, sel, s0)
    return res if need_y else (None, res[0])


def _ssd_fin_kernel(yf_ref, yb_ref, xs_ref, z_ref, d_ref, g_ref, o_ref):
    z = z_ref[...]
    y = (yf_ref[...] + yb_ref[...] + d_ref[...] * xs_ref[...]) * (z * _sigmoid(z))
    o_ref[...] = _rms(y, g_ref[...]).astype(o_ref.dtype)


def _ssd_finish(yf, yb, xs, u, row0, d_skip, norm_g, total_rows, into=None):
    nrows = yf.shape[0]
    tq = _tile(nrows, 512)
    blk0 = row0 // tq
    tok = pl.BlockSpec((tq, SSD_DIM), lambda i: (i, 0))
    return _rows_call(
        _ssd_fin_kernel, (yf, yb, xs, u, d_skip, norm_g),
        [tok, tok, tok, pl.BlockSpec((tq, U_Z_W), lambda i: (blk0 + i, U_Z // U_Z_W)),
         _const_spec((1, SSD_DIM)), _const_spec((1, SSD_DIM))],
        pl.BlockSpec((tq, SSD_DIM), lambda i: (blk0 + i, 0)),
        jax.ShapeDtypeStruct((total_rows, SSD_DIM), BF16), into,
        grid=(nrows // tq,),
        compiler_params=_params("parallel"),
    )


def _rwkv_prep_kernel(cur_ref, prev_ref, next_ref, mu_ref, w0_ref, w2_ref, a0_ref, a2_ref, g2_ref,
                      kk_ref, ka_ref, rk_ref, bd_ref,
                      p_o, g_o, bonus_o, scr, *, tq, tiles_per_seq, colmajor):
    r_o, lwf_o, lwb_o, k_o, v_o, kk_o, ka_o = (
        p_o.at[:, c * RWKV_DIM:(c + 1) * RWKV_DIM] for c in (RP_R, RP_LWF, RP_LWB, RP_K, RP_V, RP_KK, RP_KA))
    if colmajor:
        i = pl.program_id(0)
        first = (i % tiles_per_seq) == 0
        last = (i % tiles_per_seq) == tiles_per_seq - 1
        u = jnp.swapaxes(cur_ref[...], 0, 1).reshape(tq, cur_ref.shape[-1])
        scr[SUBLANE:SUBLANE + tq, :] = u
        scr[SUBLANE - 1:SUBLANE, :] = jnp.where(first, 0.0, prev_ref[SUBLANE - 1, SUBLANE - 1:SUBLANE, :])
        scr[SUBLANE + tq:SUBLANE + tq + 1, :] = jnp.where(last, 0.0, next_ref[0, 0:1, :])
    else:
        _fill_halo(scr, cur_ref, prev_ref, next_ref, tq, tiles_per_seq)
        u = cur_ref[...]
    shift = 0.5 * (scr[pl.ds(SUBLANE - 1, tq), :] + scr[pl.ds(SUBLANE + 1, tq), :]) - u
    x = u + shift * mu_ref[...]
    r = x[:, 0:RWKV_DIM]
    k = x[:, RWKV_DIM:2 * RWKV_DIM]
    v = x[:, 2 * RWKV_DIM:3 * RWKV_DIM]
    bd = bd_ref[...]
    for d, (off, out) in enumerate(((RW_WF, lwf_o), (RW_WB, lwb_o))):
        lo = jnp.tanh(x[:, off:off + LANE])
        wl = -_softplus(-(w0_ref[d:d + 1, :] + _dot(lo, w2_ref[d]))) - 0.5
        out[...] = -jnp.exp(wl)
    a = _sigmoid(a0_ref[...] + _dot(x[:, RW_AL:RW_AL + LANE], a2_ref[...]))
    kk = k * kk_ref[...]
    kk = kk * lax.rsqrt(_blocksum(kk * kk, bd) + 1e-12)
    kmod = k * (1.0 + (a - 1.0) * ka_ref[...])
    r_o[...] = r
    k_o[...] = kmod
    v_o[...] = v
    kk_o[...] = kk
    ka_o[...] = kk * a
    g_o[...] = _dot(_sigmoid(x[:, RW_GL:RW_GL + GATE_LORA]), g2_ref[...])
    bonus_o[...] = _blocksum(r * kmod * rk_ref[...], bd) * v


def _rwkv_prep(u, row0, nseq, seqlen, p, grid_rows=None):
    nrows = nseq * seqlen
    colmajor = grid_rows is not None
    if colmajor:
        assert row0 == 0 and grid_rows % SUBLANE == 0 and seqlen == grid_rows * GRID_W
        ncg, rb = GRID_W // SUBLANE, grid_rows // SUBLANE
        tq = grid_rows * SUBLANE
        u = u.reshape(u.shape[0] // GRID_W, GRID_W, U_RWKV_W)
        halo = (SUBLANE, SUBLANE, U_RWKV_W)
        cur = pl.BlockSpec((grid_rows, SUBLANE, U_RWKV_W), lambda i: (i // ncg, i % ncg, 0))
        prev = pl.BlockSpec(halo, lambda i: ((i // ncg) * rb + rb - 1, jnp.maximum(i % ncg - 1, 0), 0))
        nxt = pl.BlockSpec(halo, lambda i: ((i // ncg) * rb, jnp.minimum(i % ncg + 1, ncg - 1), 0))
    else:
        tq = _tile(seqlen, 256)
        cur, prev, nxt = _halo_specs(tq, U_RWKV_W, row0 // tq, 0, u.shape[0] // SUBLANE)
    consts = [p["mu"], p["w0"], p["w2"], p["a0"], p["a2"], p["g2"], p["k_k"], p["k_a"], p["r_k"], p["bd"]]
    tok = pl.BlockSpec((tq, RWKV_DIM), lambda i: (i, 0))
    return pl.pallas_call(
        functools.partial(_rwkv_prep_kernel, tq=tq, tiles_per_seq=seqlen // tq, colmajor=colmajor),
        grid=(nrows // tq,),
        in_specs=[cur, prev, nxt] + [_const_spec(c.shape) for c in consts],
        out_specs=[pl.BlockSpec((tq, RP_N * RWKV_DIM), lambda i: (i, 0)), tok, tok],
        out_shape=[jax.ShapeDtypeStruct((nrows, RP_N * RWKV_DIM), F32)]
        + [jax.ShapeDtypeStruct((nrows, RWKV_DIM), F32)] * 2,
        scratch_shapes=[pltpu.VMEM((tq + 2 * SUBLANE, U_RWKV_W), F32)],
        compiler_params=_params("parallel"),
    )(u, u, u, *consts)


def _rwkv_scan_kernel(*refs, C, sub, need_y):
    packed = refs[0:2]
    s0_refs = refs[2:4]
    if need_y:
        y_refs, sfin_refs, st = refs[4:6], refs[6:8], refs[8]
    else:
        sfin_refs, st = refs[4:6], refs[6]

    @pl.when(pl.program_id(1) == 0)
    def _():
        st[0] = s0_refs[0][...]
        st[1] = s0_refs[1][...]

    n2 = 2 * C
    row = lax.broadcasted_iota(jnp.int32, (n2, n2), 0)
    col = lax.broadcasted_iota(jnp.int32, (n2, n2), 1)
    t, s = row & (C - 1), col & (C - 1)
    strict = ((s < t), (s > t))
    incl = ((s <= t), (s >= t))
    eye = jnp.where(row == col, 1.0, 0.0)
    ti = lax.broadcasted_iota(jnp.int32, (C, C), 0)
    tk = lax.broadcasted_iota(jnp.int32, (C, C), 1)
    tri = (jnp.where(tk <= ti, 1.0, 0.0).astype(BF16), jnp.where(tk >= ti, 1.0, 0.0).astype(BF16))
    head0 = lax.broadcasted_iota(jnp.int32, (C, LANE), 1) < RWKV_HEAD_DIM

    def stack(x):
        return jnp.concatenate([jnp.where(head0, x, 0.0), jnp.where(head0, 0.0, x)], axis=0)

    probs = [(d, p) for d in (0, 1) for p in range(RWKV_PAIRS)]
    blocks = ((RP_R, RP_LWF, RP_K, RP_V, RP_KK, RP_KA), (RP_R, RP_LWB, RP_K, RP_V, RP_KK, RP_KA))
    for step in range(sub):
        _rwkv_chunk(packed, y_refs if need_y else None, st, ((step * C), (sub - 1 - step) * C), C, probs, blocks,
                    strict, incl, eye, tri, stack)
    sfin_refs[0][...] = st[0]
    sfin_refs[1][...] = st[1]


def _rwkv_chunk(packed, y_refs, st, offs, C, probs, blocks, strict, incl, eye, tri, stack):
    n2 = 2 * C
    need_y = y_refs is not None
    ld = lambda d, p, i: packed[d][offs[d]:offs[d] + C,
                                   blocks[d][i] * RWKV_DIM + p * LANE:blocks[d][i] * RWKV_DIM + (p + 1) * LANE]
    lw = [ld(d, p, 1) for d, p in probs]
    cs = [_dot_exact_lhs(tri[d], x) for (d, p), x in zip(probs, lw)]
    tot = [x[C - 1:C, :] if d == 0 else x[0:1, :] for (d, p), x in zip(probs, cs)]
    ems = [jnp.exp(-x) for x in cs]
    edec = [jnp.exp(a - x) for a, x in zip(tot, cs)]
    kt = [stack(ld(d, p, 4) * jnp.exp(x - w)) for (d, p), x, w in zip(probs, cs, lw)]
    rt = [stack(ld(d, p, 0) * jnp.exp(x)) for (d, p), x in zip(probs, cs)]
    at = [stack(ld(d, p, 5) * e) for (d, p), e in zip(probs, ems)]
    kb = [stack(ld(d, p, 2) * e) for (d, p), e in zip(probs, ems)]
    at2 = [stack(ld(d, p, 5) * e) for (d, p), e in zip(probs, edec)]
    kb2 = [stack(ld(d, p, 2) * e) for (d, p), e in zip(probs, edec)]
    vs_t = [stack(ld(d, p, 3)).T for d, p in probs]
    big = [_dot_nt(jnp.concatenate([a, b], axis=0), jnp.concatenate([c, e], axis=0))
           for a, b, c, e in zip(kt, rt, at, kb)]
    npow = [-jnp.where(strict[d], x[:n2, :n2], 0.0) for (d, p), x in zip(probs, big)]
    tinv = [eye + x for x in npow]
    npow = [_dot(x, x) for x in npow]
    for step in range(int(math.log2(C)) - 1):
        if step < int(math.log2(C)) - 2:
            both = [_dot(jnp.concatenate([t, x], axis=0), x) for t, x in zip(tinv, npow)]
            tinv = [t + b[:n2] for t, b in zip(tinv, both)]
            npow = [b[n2:] for b in both]
        else:
            tinv = [t + _dot(t, x) for t, x in zip(tinv, npow)]
    state = [st[d, p] for d, p in probs]
    lhs = [jnp.concatenate(
        [jnp.concatenate([a, jnp.where(strict[d], x[:n2, n2:], 0.0)], axis=1),
         jnp.concatenate([b, jnp.where(incl[d], x[n2:, n2:], 0.0)], axis=1)], axis=0)
        for (d, p), x, a, b in zip(probs, big, kt, rt)]
    xy = [_dot_nt(a, jnp.concatenate([x, v], axis=1)) for a, x, v in zip(lhs, state, vs_t)]
    u_m = [-_dot(x, y[:n2]) for x, y in zip(tinv, xy)]
    if need_y:
        for (d, p), x, u, y in zip(probs, big, u_m, xy):
            yy = y[n2:] + _dot(jnp.where(incl[d], x[n2:, :n2], 0.0), u)
            y_refs[d][offs[d]:offs[d] + C, p * LANE:(p + 1) * LANE] = yy[:C] + yy[C:]
    for (d, p), x, a, u, v, k2, a2 in zip(probs, state, tot, u_m, vs_t, kb2, at2):
        st[d, p] = x * jnp.exp(a) + _dot(jnp.concatenate([u.T, v], axis=1), jnp.concatenate([a2, k2], axis=0))


def _rwkv_scan(packed, nseq, seqlen, s0f, s0b, need_y):
    C = RWKV_CHUNK
    sub = max(s for s in (4, 2, 1) if seqlen % (s * C) == 0)
    nc = seqlen // (sub * C)
    fwd, bwd = (lambda b, c: (b * nc + c, 0)), (lambda b, c: (b * nc + nc - 1 - c, 0))
    tok_f, tok_b = pl.BlockSpec((sub * C, RWKV_DIM), fwd), pl.BlockSpec((sub * C, RWKV_DIM), bwd)
    pk_f = pl.BlockSpec((sub * C, RP_N * RWKV_DIM), fwd)
    pk_b = pl.BlockSpec((sub * C, RP_N * RWKV_DIM), bwd)
    sshape = (RWKV_PAIRS, LANE, LANE)
    sspec = pl.BlockSpec((None,) + sshape, lambda b, c: (b, 0, 0, 0))
    sstruct = jax.ShapeDtypeStruct((nseq,) + sshape, F32)
    out_specs, out_shape = [sspec, sspec], [sstruct, sstruct]
    if need_y:
        ystruct = jax.ShapeDtypeStruct((nseq * seqlen, RWKV_DIM), F32)
        out_specs, out_shape = [tok_f, tok_b] + out_specs, [ystruct, ystruct] + out_shape
    res = pl.pallas_call(
        functools.partial(_rwkv_scan_kernel, C=C, sub=sub, need_y=need_y),
        grid=(nseq, nc),
        in_specs=[pk_f, pk_b, sspec, sspec],
        out_specs=out_specs, out_shape=out_shape,
        scratch_shapes=[pltpu.VMEM((2,) + sshape, F32)],
        compiler_params=_params("parallel", "arbitrary"),
    )(packed, packed, s0f, s0b)
    return tuple(res) if need_y else (None, None) + tuple(res)


def _rwkv_fin_kernel(yf_ref, yb_ref, bonus_ref, g_ref, lg_ref, lb_ref, bd_ref, o_ref):
    y = yf_ref[...] + yb_ref[...]
    bd = bd_ref[...]
    dv = y - _blocksum(y, bd) * (1.0 / RWKV_HEAD_DIM)
    var = _blocksum(dv * dv, bd) * (1.0 / RWKV_HEAD_DIM)
    yn = dv * lax.rsqrt(var + GN_EPS) * lg_ref[...] + lb_ref[...]
    out = (yn + bonus_ref[...]) * g_ref[...]
    if len(o_ref.shape) == 3:
        grid_rows, ncols, c = o_ref.shape
        out = jnp.swapaxes(out.reshape(ncols, grid_rows, c), 0, 1)
    o_ref[...] = out.astype(o_ref.dtype)


def _rwkv_finish(yf, yb, bonus, g, ln_g, ln_b, bd, row0, total_rows, into=None, grid_rows=None):
    nrows = yf.shape[0]
    if grid_rows is None:
        tq = _tile(nrows, 512)
        blk0 = row0 // tq
        out_spec = pl.BlockSpec((tq, RWKV_DIM), lambda i: (blk0 + i, 0))
        out_shape = jax.ShapeDtypeStruct((total_rows, RWKV_DIM), BF16)
    else:
        assert row0 == 0
        ncols = 16
        tq, ncg = ncols * grid_rows, GRID_W // ncols
        out_spec = pl.BlockSpec((grid_rows, ncols, RWKV_DIM), lambda i: (i // ncg, i % ncg, 0))
        out_shape = jax.ShapeDtypeStruct((total_rows // GRID_W, GRID_W, RWKV_DIM), BF16)
        into = None if into is None else into.reshape(out_shape.shape)
    tok = pl.BlockSpec((tq, RWKV_DIM), lambda i: (i, 0))
    res = _rows_call(
        _rwkv_fin_kernel, (yf, yb, bonus, g, ln_g, ln_b, bd),
        [tok] * 4 + [_const_spec((1, RWKV_DIM))] * 2 + [_const_spec(bd.shape)],
        out_spec, out_shape, into,
        grid=(nrows // tq,),
        compiler_params=_params("parallel"),
    )
    return res.reshape(total_rows, RWKV_DIM)


def _pad_cols(w, n):
    return jnp.pad(w, [(0, 0)] * (w.ndim - 1) + [(0, n - w.shape[-1])])


def _pad_rows(w, n):
    return jnp.pad(w, [(0, 0)] * (w.ndim - 2) + [(0, n - w.shape[-2]), (0, 0)])


def kernel(x, c, ctx, c_ctx, mod_w, mod_b, norm_g, w_in, conv_w, conv_b, conv_ln_g, conv_ln_b, conv_out, ssd_conv_w, ssd_conv_b, ssd_A_log, ssd_dt_bias, ssd_D, ssd_norm_g, ssd_out, fourier_out, rwkv_mu, rwkv_w0, rwkv_w2, rwkv_a0, rwkv_a2, rwkv_g2, rwkv_k_k, rwkv_k_a, rwkv_r_k, rwkv_ln_g, rwkv_ln_b, rwkv_out, w_o, mlp_up, mlp_down):
    B, L, D = x.shape
    CL = ctx.shape[1]
    depth = mod_w.shape[0]
    rows = L // GRID_W
    n_lat, n_ctx = B * L, B * CL
    n_all = n_lat + n_ctx
    tm = _tile(math.gcd(L, n_ctx), 1024)
    tm_small = _tile(tm, 512)
    tm_merge = _tile(tm, 256)
    mod_idx = lambda t: (lambda i: jnp.minimum((i * t) // L, B))

    w_u, w_rw, w_gate = _split_w_in(w_in)
    down_w, wo_w = mlp_down.astype(BF16), w_o.astype(BF16)
    branch_w = [conv_out.astype(BF16), ssd_out.astype(BF16), fourier_out.astype(BF16), rwkv_out.astype(BF16)]
    mu = rwkv_mu
    mu_cols = jnp.concatenate(
        [mu[:, :3 * RWKV_DIM]] + [_pad_cols(mu[:, a:b], LANE) for a, b in ((1536, 1600), (1600, 1664), (1664, 1728))]
        + [mu[:, 1728:]], axis=1)
    head_of = jnp.arange(RWKV_DIM, dtype=jnp.int32) // RWKV_HEAD_DIM
    bd = (head_of[:, None] == head_of[None, :]).astype(BF16)
    gidx = jnp.arange(FOURIER_DIM, dtype=jnp.int32)
    cc, sc_ = _dft_mats(FOURIER_GROUP_DIM, FOURIER_GROUP_DIM ** -0.5)
    same_group = (gidx[:, None] // FOURIER_GROUP_DIM) == (gidx[None, :] // FOURIER_GROUP_DIM)
    tile_g = lambda m: jnp.where(same_group, jnp.tile(m, (FOURIER_GROUPS, FOURIER_GROUPS)), 0.0)
    chan_w = jnp.concatenate([tile_g(cc), -tile_g(sc_)], axis=1).astype(BF16)
    conv_w_p = _pad_rows(conv_w, 32)
    ssd_conv_w_p = _pad_rows(ssd_conv_w, SUBLANE)
    dt_bias_p = _pad_cols(ssd_dt_bias.reshape(depth, 1, 2 * SSD_HEADS), U_DT_W)
    a_log_p = _pad_cols(ssd_A_log.reshape(depth, 1, 2 * SSD_HEADS), U_DT_W)
    d_skip = jnp.repeat(ssd_D, SSD_HEAD_DIM, axis=1).reshape(depth, 1, SSD_DIM)

    cond = jnp.concatenate([c, c_ctx[None, :], jnp.zeros((SUBLANE - (B + 1) % SUBLANE, D), F32)], axis=0)
    mod = _mod_all(cond, mod_w, mod_b)

    xs = jnp.concatenate([x.reshape(n_lat, D), ctx.reshape(n_ctx, D)], axis=0)
    ssd_zero = jnp.zeros((B, SSD_GROUPS, SSD_STATE, SSD_HEADS_PER_GROUP * SSD_HEAD_DIM), F32)
    rwkv_zero = jnp.zeros((B, RWKV_PAIRS, LANE, LANE), F32)

    mods = lambda l: tuple(mod[l, :, j * D:(j + 1) * D].reshape(-1, 1, D) for j in range(6))
    h = _norm_mod(xs, norm_g[0, 0].reshape(1, D), *mods(0)[1::-1], n_all, mod_idx(tm), tm)
    for i in range(depth):
        last = i == depth - 1
        m_out = n_lat if last else n_all
        sh1, sc1, g1, sh2, sc2, g2 = mods(i)
        ng = norm_g[i].reshape(4, 1, D)

        u = _matmul(h, w_u, n_all, tm, F32, tn_cap=2304, layer=i)
        gates = _matmul(h, w_gate, m_out, tm, BF16, epilogue="sigmoid", tn_cap=2048, layer=i)

        cargs = (conv_w_p[i], conv_b[i][None], conv_ln_g[i][None], conv_ln_b[i][None])
        a_conv = _conformer(u, 0, n_lat, GRID_W, *cargs, m_out)
        z = _matmul(u, chan_w, m_out, tm, BF16, a_col=U_FFT // U_FFT_W)
        a_fft = _fourier(z, 0, B, L, m_out)
        if not last:
            a_conv = _conformer(u, n_lat, n_ctx, CL, *cargs, m_out, into=a_conv)
            a_fft = _fourier(z, n_lat, B, CL, m_out, into=a_fft)

        ssd_args = (dt_bias_p[i], a_log_p[i])
        xs_c, bm_c, cm_c = _ssd_conv(u, n_lat, B, CL, ssd_conv_w_p[i], ssd_conv_b[i][None])
        yfc, sf = _ssd_scan(xs_c, bm_c, cm_c, u, n_lat, B, CL, *ssd_args, ssd_zero, False, not last)
        ybc, sb = _ssd_scan(xs_c, bm_c, cm_c, u, n_lat, B, CL, *ssd_args, ssd_zero, True, not last)
        xs_l, bm_l, cm_l = _ssd_conv(u, 0, B, L, ssd_conv_w_p[i], ssd_conv_b[i][None])
        yfl, _ = _ssd_scan(xs_l, bm_l, cm_l, u, 0, B, L, *ssd_args, sf, False, True)
        ybl, _ = _ssd_scan(xs_l, bm_l, cm_l, u, 0, B, L, *ssd_args, sb, True, True)
        a_ssd = _ssd_finish(yfl, ybl, xs_l, u, 0, d_skip[i], ssd_norm_g[i][None], m_out)
        if not last:
            a_ssd = _ssd_finish(yfc, ybc, xs_c, u, n_lat, d_skip[i], ssd_norm_g[i][None], m_out, into=a_ssd)

        rp = dict(mu=mu_cols[i][None], w0=rwkv_w0[i], w2=_pad_rows(rwkv_w2[i], LANE).astype(BF16),
                  a0=rwkv_a0[i][None], a2=_pad_rows(rwkv_a2[i], LANE).astype(BF16), g2=rwkv_g2[i].astype(BF16),
                  k_k=rwkv_k_k[i][None], k_a=rwkv_k_a[i][None], r_k=rwkv_r_k[i].reshape(1, RWKV_DIM), bd=bd)
        fin = (rwkv_ln_g[i][None], rwkv_ln_b[i][None], bd)
        u_r = _matmul(h, w_rw, n_all, tm, F32, tn_cap=2048, layer=i)
        packed_c, g_c, bonus_c = _rwkv_prep(u_r, n_lat, B, CL, rp)
        yfc, ybc, s_f, s_b = _rwkv_scan(packed_c, B, CL, rwkv_zero, rwkv_zero, not last)
        packed, g_, bonus = _rwkv_prep(u_r, 0, B, L, rp, grid_rows=rows)
        yfl, ybl, _, _ = _rwkv_scan(packed, B, L, s_f, s_b, True)
        a_rwkv = _rwkv_finish(yfl, ybl, bonus, g_, *fin, 0, m_out, grid_rows=rows)
        if not last:
            a_rwkv = _rwkv_finish(yfc, ybc, bonus_c, g_c, *fin, n_lat, m_out, into=a_rwkv)

        m = _merge(gates, [a_conv, a_ssd, a_fft, a_rwkv], branch_w, m_out, tm_merge, i)
        xs, h2 = _matmul_residual(m, wo_w, xs, g1, ng[1], m_out, mod_idx(tm_small), tm_small,
                                  nxt=(ng[2], sc2, sh2), layer=i)

        mid = _matmul_f32w(h2, mlp_up, m_out, tm, BF16, epilogue="relu2", layer=i)
        if last:
            xs = _matmul_residual(mid, down_w, xs, g2, ng[3], m_out, mod_idx(tm_small), tm_small, layer=i)
        else:
            xs, h = _matmul_residual(mid, down_w, xs, g2, ng[3], m_out, mod_idx(tm_small), tm_small,
                                     nxt=(norm_g[i + 1, 0].reshape(1, D), *mods(i + 1)[1::-1]), layer=i)

    return xs[:n_lat].reshape(B, L, D)
```

```python
import functools
import math

import jax
import jax.numpy as jnp
import numpy as np
from jax import lax
from jax.experimental import pallas as pl
from jax.experimental.pallas import tpu as pltpu

F32 = jnp.float32
BF16 = jnp.bfloat16

GRID_W = 64
RMS_EPS = 1e-6
LN_EPS = 1e-5
GN_EPS = 64e-5
CONV_DIM = 512
CONV_WIDTH = 31
SSD_HEADS = 12
SSD_HEAD_DIM = 64
SSD_DIM = SSD_HEADS * SSD_HEAD_DIM
SSD_GROUPS = 4
SSD_HEADS_PER_GROUP = SSD_HEADS // SSD_GROUPS
SSD_STATE = 128
SSD_CONV = 5
SSD_BC = SSD_GROUPS * SSD_STATE
SSD_XBC = SSD_DIM + 2 * SSD_BC
SSD_IN = SSD_DIM + SSD_XBC + 2 * SSD_HEADS
SSD_CHUNK = 128
FOURIER_GROUPS = 4
FOURIER_GROUP_DIM = 128
FOURIER_DIM = FOURIER_GROUPS * FOURIER_GROUP_DIM
RWKV_HEADS = 8
RWKV_HEAD_DIM = 64
RWKV_DIM = RWKV_HEADS * RWKV_HEAD_DIM
RWKV_PAIRS = RWKV_HEADS // 2
DECAY_LORA = 64
ICL_LORA = 64
GATE_LORA = 128
RWKV_IN = 3 * RWKV_DIM + 2 * DECAY_LORA + ICL_LORA + GATE_LORA
RWKV_CHUNK = 64
REC_IN = SSD_IN + RWKV_IN
N_BRANCH = 4

LANE = 128
SUBLANE = 8
VMEM_LIMIT = 56 * 1024 * 1024

U_XBC, U_XBC_W = 0, SSD_XBC
U_DT, U_DT_W = 1792, LANE
U_CONV, U_CONV_W = 2048, 2 * CONV_DIM
U_Z, U_Z_W = 3072, SSD_DIM
U_FFT, U_FFT_W = 4096, FOURIER_DIM
U_N = 4608
U_RWKV_W = 2048
RW_WF, RW_WB, RW_AL, RW_GL = 1536, 1664, 1792, 1920
RP_R, RP_LWF, RP_LWB, RP_K, RP_V, RP_KK, RP_KA, RP_N = 0, 1, 2, 3, 4, 5, 6, 7


def _params(*sem):
    return pltpu.CompilerParams(dimension_semantics=sem, vmem_limit_bytes=VMEM_LIMIT)


def _tile(n, cap, quantum=SUBLANE):
    if n <= cap:
        return n
    t = (cap // quantum) * quantum
    while t >= quantum:
        if n % t == 0:
            return t
        t -= quantum
    raise ValueError(f"no tile for {n} under {cap}")


def _const_spec(shape):
    nd = len(shape)
    return pl.BlockSpec(shape, lambda *_: (0,) * nd)


def _dot(a, b):
    return jnp.dot(a.astype(BF16), b.astype(BF16), preferred_element_type=F32)


def _dot_nt(a, b):
    return lax.dot_general(a.astype(BF16), b.astype(BF16), (((1,), (1,)), ((), ())),
                           preferred_element_type=F32)


def _split3(x):
    h1 = x.astype(BF16)
    r1 = x - h1.astype(F32)
    h2 = r1.astype(BF16)
    h3 = (r1 - h2.astype(F32)).astype(BF16)
    return h1, h2, h3


def _dot_exact_lhs(m, x):
    h1, h2, h3 = _split3(x)
    d = lambda h: jnp.dot(m, h, preferred_element_type=F32)
    return d(h1) + d(h2) + d(h3)


def _blocksum(x, bd):
    hi = x.astype(BF16)
    lo = (x - hi.astype(F32)).astype(BF16)
    return (jnp.dot(hi, bd, preferred_element_type=F32) + jnp.dot(lo, bd, preferred_element_type=F32))


def _sigmoid(x):
    return 1.0 / (1.0 + jnp.exp(-x))


def _softplus(x):
    return jnp.maximum(x, 0.0) + jnp.log1p(jnp.exp(-jnp.abs(x)))


def _rms(y, g):
    return y * lax.rsqrt(jnp.mean(y * y, axis=-1, keepdims=True) + RMS_EPS) * g


_O_RWKV, _O_CONV = SSD_IN, REC_IN
_O_FFT, _O_GATE = REC_IN + 2 * CONV_DIM, REC_IN + 2 * CONV_DIM + FOURIER_DIM
_W_IN_PIECES = (
    ("rw", 0, _O_RWKV, 3 * RWKV_DIM), ("rw", RW_WF, _O_RWKV + 1536, DECAY_LORA),
    ("rw", RW_WB, _O_RWKV + 1600, DECAY_LORA), ("rw", RW_AL, _O_RWKV + 1664, ICL_LORA),
    ("rw", RW_GL, _O_RWKV + 1728, GATE_LORA),
    ("u", U_XBC, SSD_DIM, SSD_XBC), ("u", U_DT, SSD_DIM + SSD_XBC, 2 * SSD_HEADS),
    ("u", U_CONV, _O_CONV, 2 * CONV_DIM), ("u", U_Z, 0, SSD_DIM), ("u", U_FFT, _O_FFT, FOURIER_DIM),
)


def _split_w_in(w_in):
    depth, d, _ = w_in.shape
    w_t = jnp.swapaxes(w_in, 1, 2)

    def assemble(name, n_rows):
        parts, pos = [], 0
        for _, dst, src, width in sorted(p for p in _W_IN_PIECES if p[0] == name):
            if dst > pos:
                parts.append(jnp.zeros((depth, dst - pos, d), w_in.dtype))
            parts.append(w_t[:, src:src + width, :])
            pos = dst + width
        if n_rows > pos:
            parts.append(jnp.zeros((depth, n_rows - pos, d), w_in.dtype))
        return jnp.concatenate(parts, axis=1).astype(BF16)

    return assemble("u", U_N), assemble("rw", U_RWKV_W), w_t[:, _O_GATE:, :].astype(BF16)


def _mod_kernel(a_ref, w_ref, b_ref, o_ref):
    a = a_ref[...]
    a = a * _sigmoid(a)
    o_ref[...] = _dot(a, w_ref[...]) + b_ref[...]


def _mod_all(cond, mod_w, mod_b):
    depth, d, n = mod_w.shape
    tn = _tile(n, 1024, LANE)
    return pl.pallas_call(
        _mod_kernel,
        grid=(depth, n // tn),
        in_specs=[_const_spec(cond.shape),
                  pl.BlockSpec((None, d, tn), lambda l, j: (l, 0, j)),
                  pl.BlockSpec((None, 1, tn), lambda l, j: (l, 0, j))],
        out_specs=pl.BlockSpec((None, cond.shape[0], tn), lambda l, j: (l, 0, j)),
        out_shape=jax.ShapeDtypeStruct((depth, cond.shape[0], n), F32),
        compiler_params=_params("parallel", "parallel"),
    )(cond, mod_w, mod_b.reshape(depth, 1, n))


def _norm_mod_kernel(x_ref, g_ref, sc_ref, sh_ref, o_ref):
    o_ref[...] = (_rms(x_ref[...], g_ref[...]) * (1.0 + sc_ref[...]) + sh_ref[...]).astype(o_ref.dtype)


def _norm_mod(x, g, sc, sh, m_rows, mod_idx, tm):
    d = x.shape[1]
    return pl.pallas_call(
        _norm_mod_kernel,
        grid=(m_rows // tm,),
        in_specs=[pl.BlockSpec((tm, d), lambda i: (i, 0)),
                  _const_spec((1, d)),
                  pl.BlockSpec((None, 1, d), lambda i: (mod_idx(i), 0, 0)),
                  pl.BlockSpec((None, 1, d), lambda i: (mod_idx(i), 0, 0))],
        out_specs=pl.BlockSpec((tm, d), lambda i: (i, 0)),
        out_shape=jax.ShapeDtypeStruct((m_rows, d), BF16),
        compiler_params=_params("parallel"),
    )(x, g, sc, sh)


def _mm_kernel(a_ref, w_ref, o_ref, *, epilogue, w_is_nk=False):
    acc = _dot_nt(a_ref[...], w_ref[...]) if w_is_nk else _dot(a_ref[...], w_ref[...])
    if epilogue == "relu2":
        acc = jnp.square(jnp.maximum(acc, 0.0))
    elif epilogue == "sigmoid":
        acc = _sigmoid(acc)
    o_ref[...] = acc.astype(o_ref.dtype)


def _layer_spec(w, layer, block, index):
    if w.ndim == 2:
        return pl.BlockSpec(block, index)
    return pl.BlockSpec((None,) + block, lambda *ids: (layer,) + index(*ids))


def _matmul(a, w, m_rows, tm, out_dtype, epilogue=None, a_col=0, a_row=0, tn_cap=1024, layer=0, w_is_nk=False):
    n, k = w.shape[-2:] if w_is_nk else w.shape[-2:][::-1]
    tn = _tile(n, tn_cap, LANE)
    w_spec = (_layer_spec(w, layer, (tn, k), lambda i, j: (j, 0)) if w_is_nk
              else _layer_spec(w, layer, (k, tn), lambda i, j: (0, j)))
    return pl.pallas_call(
        functools.partial(_mm_kernel, epilogue=epilogue, w_is_nk=w_is_nk),
        grid=(m_rows // tm, n // tn),
        in_specs=[pl.BlockSpec((tm, k), lambda i, j: (a_row + i, a_col)), w_spec],
        out_specs=pl.BlockSpec((tm, tn), lambda i, j: (i, j)),
        out_shape=jax.ShapeDtypeStruct((m_rows, n), out_dtype),
        compiler_params=_params("parallel", "arbitrary"),
    )(a, w)


def _mm_wcast_kernel(a_ref, w_ref, o_ref, wb_ref, *, epilogue):
    @pl.when(pl.program_id(1) == 0)
    def _():
        wb_ref[...] = w_ref[...].astype(wb_ref.dtype)

    _mm_kernel(a_ref, wb_ref, o_ref, epilogue=epilogue)


def _matmul_f32w(a, w, m_rows, tm, out_dtype, epilogue=None, tn_cap=1024, layer=0):
    k, n = w.shape[-2:]
    tn = _tile(n, tn_cap, LANE)
    return pl.pallas_call(
        functools.partial(_mm_wcast_kernel, epilogue=epilogue),
        grid=(n // tn, m_rows // tm),
        in_specs=[pl.BlockSpec((tm, k), lambda j, i: (i, 0)),
                  _layer_spec(w, layer, (k, tn), lambda j, i: (0, j))],
        out_specs=pl.BlockSpec((tm, tn), lambda j, i: (i, j)),
        out_shape=jax.ShapeDtypeStruct((m_rows, n), out_dtype),
        scratch_shapes=[pltpu.VMEM((k, tn), BF16)],
        compiler_params=_params("parallel", "arbitrary"),
    )(a, w)


def _mm_res_kernel(a_ref, w_ref, x_ref, gate_ref, g_ref, *rest, with_next, nk):
    if with_next:
        g2_ref, sc_ref, sh_ref, o_ref, h_ref = rest
    else:
        (o_ref,) = rest
    def finish(y):
        xn = x_ref[...] + gate_ref[...] * _rms(y, g_ref[...])
        o_ref[...] = xn
        if with_next:
            h_ref[...] = (_rms(xn, g2_ref[...]) * (1.0 + sc_ref[...]) + sh_ref[...]).astype(h_ref.dtype)

    if nk == 1:
        finish(_dot(a_ref[...], w_ref[...]))
        return
    k = pl.program_id(1)

    @pl.when(k == 0)
    def _():
        o_ref[...] = jnp.zeros_like(o_ref)

    o_ref[...] += _dot(a_ref[...], w_ref[...])

    @pl.when(k == nk - 1)
    def _():
        finish(o_ref[...])


def _matmul_residual(a, w, x, gate, g, m_rows, mod_idx, tm, nxt=None, tk_cap=2048, layer=0):
    k, d = w.shape[-2:]
    tk = _tile(k, tk_cap, LANE)
    mod_spec = pl.BlockSpec((None, 1, d), lambda i, kk: (mod_idx(i), 0, 0))
    tile_spec = pl.BlockSpec((tm, d), lambda i, kk: (i, 0))
    in_specs = [pl.BlockSpec((tm, tk), lambda i, kk: (i, kk)), _layer_spec(w, layer, (tk, d), lambda i, kk: (kk, 0)),
                tile_spec, mod_spec, _const_spec((1, d))]
    out_specs, out_shape, args = tile_spec, jax.ShapeDtypeStruct((m_rows, d), F32), (a, w, x, gate, g)
    if nxt is not None:
        in_specs += [_const_spec((1, d)), mod_spec, mod_spec]
        out_specs, out_shape = [tile_spec, tile_spec], [out_shape, jax.ShapeDtypeStruct((m_rows, d), BF16)]
        args += tuple(nxt)
    return pl.pallas_call(
        functools.partial(_mm_res_kernel, with_next=nxt is not None, nk=k // tk),
        grid=(m_rows // tm, k // tk),
        in_specs=in_specs, out_specs=out_specs, out_shape=out_shape,
        compiler_params=_params("parallel", "arbitrary"),
    )(*args)


def _merge_kernel(gates_ref, a0, a1, a2, a3, w0, w1, w2, w3, o_ref):
    d = o_ref.shape[-1]
    m = None
    for i, (a, w) in enumerate(((a0, w0), (a1, w1), (a2, w2), (a3, w3))):
        t = gates_ref[:, i * d:(i + 1) * d].astype(F32) * _dot(a[...], w[...])
        m = t if m is None else m + t
    o_ref[...] = m.astype(o_ref.dtype)


def _merge(gates, acts, weights, m_rows, tm, layer):
    d = weights[0].shape[-1]
    return pl.pallas_call(
        _merge_kernel,
        grid=(m_rows // tm,),
        in_specs=[pl.BlockSpec((tm, N_BRANCH * d), lambda i: (i, 0))]
        + [pl.BlockSpec((tm, a.shape[1]), lambda i: (i, 0)) for a in acts]
        + [_layer_spec(w, layer, w.shape[-2:], lambda i: (0, 0)) for w in weights],
        out_specs=pl.BlockSpec((tm, d), lambda i: (i, 0)),
        out_shape=jax.ShapeDtypeStruct((m_rows, d), BF16),
        compiler_params=_params("parallel"),
    )(gates, *acts, *weights)


def _conformer_kernel(u_ref, w_ref, b_ref, lg_ref, lb_ref, o_ref, sh_ref, *, seg, nseg):
    pad = 16
    first = pad - (CONV_WIDTH - 1) // 2
    span = seg + 2 * pad - SUBLANE
    zeros = jnp.zeros((pad, CONV_DIM), F32)
    sh_ref[0, 0:pad, :] = zeros
    sh_ref[0, pad + seg:2 * pad + seg, :] = zeros

    def body(s, carry):
        r0 = pl.multiple_of(s * seg, seg)
        u = u_ref[pl.ds(r0, seg), :]
        sh_ref[0, pad:pad + seg, :] = u[:, :CONV_DIM] * _sigmoid(u[:, CONV_DIM:])
        for r in range(1, SUBLANE):
            sh_ref[r, 0:span, :] = sh_ref[0, pl.ds(r, span), :]
        acc = jnp.zeros((seg, CONV_DIM), F32) + b_ref[...]
        for j in range(CONV_WIDTH):
            start = first + j
            acc = acc + w_ref[j:j + 1, :] * sh_ref[start % SUBLANE, pl.ds(start - start % SUBLANE, seg), :]
        mu = jnp.mean(acc, axis=-1, keepdims=True)
        dv = acc - mu
        var = jnp.mean(dv * dv, axis=-1, keepdims=True)
        v = dv * lax.rsqrt(var + LN_EPS) * lg_ref[...] + lb_ref[...]
        o_ref[pl.ds(r0, seg), :] = (v * _sigmoid(v)).astype(o_ref.dtype)
        return carry

    lax.fori_loop(0, nseg, body, 0)


def _rows_call(kernel, args, in_specs, out_spec, out_struct, into, **kw):
    if into is None:
        return pl.pallas_call(kernel, in_specs=in_specs, out_specs=out_spec, out_shape=out_struct, **kw)(*args)
    n = len(args)

    def aliased(*refs):
        kernel(*refs[:n], *refs[n + 1:])

    return pl.pallas_call(aliased, in_specs=list(in_specs) + [pl.BlockSpec(memory_space=pl.ANY)],
                          out_specs=out_spec, out_shape=out_struct, input_output_aliases={n: 0}, **kw)(*args, into)


def _conformer(u, row0, nrows, seg, conv_w, conv_b, ln_g, ln_b, total_rows, into=None):
    tq = max(seg, min(256, nrows))
    blk0 = row0 // tq
    return _rows_call(
        functools.partial(_conformer_kernel, seg=seg, nseg=tq // seg),
        (u, conv_w, conv_b, ln_g, ln_b),
        [pl.BlockSpec((tq, U_CONV_W), lambda i: (blk0 + i, U_CONV // U_CONV_W)),
         _const_spec(conv_w.shape), _const_spec((1, CONV_DIM)),
         _const_spec((1, CONV_DIM)), _const_spec((1, CONV_DIM))],
        pl.BlockSpec((tq, CONV_DIM), lambda i: (blk0 + i, 0)),
        jax.ShapeDtypeStruct((total_rows, CONV_DIM), BF16), into,
        grid=(nrows // tq,),
        scratch_shapes=[pltpu.VMEM((SUBLANE, seg + 32, CONV_DIM), F32)],
        compiler_params=_params("parallel"),
    )


def _dft_kernel(c_ref, s_ref, z_ref, o_ref):
    o_ref[...] = (jnp.dot(c_ref[...], z_ref[:, :FOURIER_DIM], preferred_element_type=F32)
                  + jnp.dot(s_ref[...], z_ref[:, FOURIER_DIM:], preferred_element_type=F32)).astype(o_ref.dtype)


def _dft_mats(n, scale):
    idx = np.arange(n, dtype=np.int64)
    ang = ((idx[:, None] * idx[None, :]) % n).astype(np.float64) * (2.0 * math.pi / n)
    return (jnp.asarray((np.cos(ang) * scale).astype(np.float32)),
            jnp.asarray((np.sin(ang) * scale).astype(np.float32)))


def _fourier(z, row0, nseq, seqlen, total_rows, into=None):
    cl, sl = _dft_mats(seqlen, seqlen ** -0.5)
    cl, sl = cl.astype(BF16), sl.astype(BF16)
    tm = _tile(seqlen, 512)
    nt = seqlen // tm
    sblk0, oblk0 = row0 // seqlen, row0 // tm
    return _rows_call(
        _dft_kernel, (cl, sl, z),
        [pl.BlockSpec((tm, seqlen), lambda i, b: (i, 0)),
         pl.BlockSpec((tm, seqlen), lambda i, b: (i, 0)),
         pl.BlockSpec((seqlen, 2 * FOURIER_DIM), lambda i, b: (sblk0 + b, 0))],
        pl.BlockSpec((tm, FOURIER_DIM), lambda i, b: (oblk0 + b * nt + i, 0)),
        jax.ShapeDtypeStruct((total_rows, FOURIER_DIM), BF16), into,
        grid=(nt, nseq),
        compiler_params=_params("parallel", "arbitrary"),
    )


def _halo_specs(tq, width, blk0, col_blk, nblk_total):
    hb = tq // SUBLANE
    cur = pl.BlockSpec((tq, width), lambda i: (blk0 + i, col_blk))
    prev = pl.BlockSpec((SUBLANE, width), lambda i: (jnp.maximum((blk0 + i) * hb - 1, 0), col_blk))
    nxt = pl.BlockSpec((SUBLANE, width),
                       lambda i: (jnp.minimum((blk0 + i + 1) * hb, nblk_total - 1), col_blk))
    return cur, prev, nxt


def _fill_halo(scr, cur_ref, prev_ref, next_ref, tq, tiles_per_seq):
    i = pl.program_id(0)
    first = (i % tiles_per_seq) == 0
    last = (i % tiles_per_seq) == tiles_per_seq - 1
    scr[0:SUBLANE, :] = jnp.where(first, 0.0, prev_ref[...])
    scr[SUBLANE:SUBLANE + tq, :] = cur_ref[...]
    scr[SUBLANE + tq:2 * SUBLANE + tq, :] = jnp.where(last, 0.0, next_ref[...])


def _ssd_conv_kernel(cur_ref, prev_ref, next_ref, w_ref, b_ref, xs_ref, bm_ref, cm_ref, scr, *, tq, tiles_per_seq):
    _fill_halo(scr, cur_ref, prev_ref, next_ref, tq, tiles_per_seq)
    half = (SSD_CONV - 1) // 2
    cw = 256
    for c0 in range(0, SSD_XBC, cw):
        acc = jnp.zeros((tq, cw), F32) + b_ref[:, c0:c0 + cw]
        for j in range(SSD_CONV):
            acc = acc + w_ref[j:j + 1, c0:c0 + cw] * scr[pl.ds(SUBLANE - half + j, tq), c0:c0 + cw]
        y = acc * _sigmoid(acc)
        if c0 < SSD_DIM:
            xs_ref[:, c0:c0 + cw] = y
        elif c0 < SSD_DIM + SSD_BC:
            bm_ref[:, c0 - SSD_DIM:c0 - SSD_DIM + cw] = y
        else:
            cm_ref[:, c0 - SSD_DIM - SSD_BC:c0 - SSD_DIM - SSD_BC + cw] = y


def _ssd_conv(u, row0, nseq, seqlen, conv_w, conv_b):
    nrows = nseq * seqlen
    tq = _tile(seqlen, 256)
    cur, prev, nxt = _halo_specs(tq, U_XBC_W, row0 // tq, U_XBC // U_XBC_W, u.shape[0] // SUBLANE)
    outs = [jax.ShapeDtypeStruct((nrows, w), F32) for w in (SSD_DIM, SSD_BC, SSD_BC)]
    return pl.pallas_call(
        functools.partial(_ssd_conv_kernel, tq=tq, tiles_per_seq=seqlen // tq),
        grid=(nrows // tq,),
        in_specs=[cur, prev, nxt, _const_spec(conv_w.shape), _const_spec((1, SSD_XBC))],
        out_specs=[pl.BlockSpec((tq, w), lambda i: (i, 0)) for w in (SSD_DIM, SSD_BC, SSD_BC)],
        out_shape=outs,
        scratch_shapes=[pltpu.VMEM((tq + 2 * SUBLANE, U_XBC_W), F32)],
        compiler_params=_params("parallel"),
    )(u, u, u, conv_w, conv_b)


def _ssd_scan_kernel(xs_ref, b_ref, c_ref, dt_ref, bias_ref, alog_ref, sel_ref, s0_ref, *rest,
                     q, sub, rev, direction, need_y):
    if need_y:
        y_ref, sfin_ref, st = rest
    else:
        sfin_ref, st = rest

    @pl.when(pl.program_id(1) == 0)
    def _():
        st[...] = s0_ref[...]

    hp, gw = SSD_HEAD_DIM, SSD_HEADS_PER_GROUP * SSD_HEAD_DIM
    ti = lax.broadcasted_iota(jnp.int32, (q, q), 0)
    tk = lax.broadcasted_iota(jnp.int32, (q, q), 1)
    before = (tk >= ti) if rev else (tk <= ti)
    tri = jnp.where(before, 1.0, 0.0).astype(BF16)
    neg_a = -jnp.exp(alog_ref[...])
    sel = sel_ref[...]
    spread = lambda parts: sum(jnp.dot(p, sel, preferred_element_type=F32) for p in parts)
    lane0 = lax.broadcasted_iota(jnp.int32, (q, LANE), 1) < hp

    for step in range(sub):
        rows = slice((sub - 1 - step) * q if rev else step * q, (sub - step) * q if rev else (step + 1) * q)
        dt = _softplus(dt_ref[rows, :] + bias_ref[...])
        cs = _dot_exact_lhs(tri, dt * neg_a)
        dt_b = spread(_split3(dt)[:2])
        cs_b = spread(_split3(cs))
        tot_b = cs_b[0:1, :] if rev else cs_b[q - 1:q, :]
        xd = xs_ref[rows, :] * dt_b
        xdd = xd * jnp.exp(tot_b - cs_b)
        dec = jnp.exp(tot_b)
        if need_y:
            ecs = jnp.exp(cs_b)
            cs_t = cs.T
        for g in range(SSD_GROUPS):
            cg = c_ref[rows, g * SSD_STATE:(g + 1) * SSD_STATE]
            bg = b_ref[rows, g * SSD_STATE:(g + 1) * SSD_STATE]
            state = st[g]
            if need_y:
                cb = _dot_nt(cg, bg)
                y_off = _dot(cg, state) * ecs[:, g * gw:(g + 1) * gw]
                for j in range(SSD_HEADS_PER_GROUP):
                    h = g * SSD_HEADS_PER_GROUP + j
                    col = direction * SSD_HEADS + h
                    pair = cs_b[:, (h // 2) * LANE:(h // 2 + 1) * LANE]
                    swapped = pltpu.roll(pair, hp, axis=1)
                    cs_col = jnp.where(lane0, pair, swapped) if h % 2 == 0 else jnp.where(lane0, swapped, pair)
                    lm = jnp.where(before, jnp.exp(jnp.minimum(cs_col - cs_t[col:col + 1, :], 0.0)), 0.0)
                    y_ref[rows, h * hp:(h + 1) * hp] = (_dot(cb * lm, xd[:, h * hp:(h + 1) * hp])
                                                        + y_off[:, j * hp:(j + 1) * hp])
            st[g] = state * dec[:, g * gw:(g + 1) * gw] + _dot(bg.T, xdd[:, g * gw:(g + 1) * gw])
    sfin_ref[...] = st[...]


def _ssd_scan(xs, bm, cm, u, row0, nseq, seqlen, dt_bias, a_log, s0, rev, need_y):
    q = SSD_CHUNK
    sub = max(s for s in (4, 2, 1) if seqlen % (s * q) == 0)
    bq = sub * q
    nc = seqlen // bq
    ublk0 = row0 // bq
    pos = (lambda c: nc - 1 - c) if rev else (lambda c: c)
    row = lambda b, c: (b * nc + pos(c), 0)
    sshape = (SSD_GROUPS, SSD_STATE, SSD_HEADS_PER_GROUP * SSD_HEAD_DIM)
    sspec = pl.BlockSpec((None,) + sshape, lambda b, c: (b, 0, 0, 0))
    head_col = int(rev) * SSD_HEADS + jnp.arange(SSD_DIM, dtype=jnp.int32) // SSD_HEAD_DIM
    sel = (jnp.arange(U_DT_W, dtype=jnp.int32)[:, None] == head_col[None, :]).astype(BF16)
    out_specs, out_shape = [sspec], [jax.ShapeDtypeStruct((nseq,) + sshape, F32)]
    if need_y:
        out_specs.insert(0, pl.BlockSpec((bq, SSD_DIM), row))
        out_shape.insert(0, jax.ShapeDtypeStruct((nseq * seqlen, SSD_DIM), F32))
    res = pl.pallas_call(
        functools.partial(_ssd_scan_kernel, q=q, sub=sub, rev=rev, direction=int(rev), need_y=need_y),
        grid=(nseq, nc),
        in_specs=[pl.BlockSpec((bq, SSD_DIM), row), pl.BlockSpec((bq, SSD_BC), row),
                  pl.BlockSpec((bq, SSD_BC), row),
                  pl.BlockSpec((bq, U_DT_W), lambda b, c: (ublk0 + b * nc + pos(c), U_DT // U_DT_W)),
                  _const_spec((1, U_DT_W)), _const_spec((1, U_DT_W)), _const_spec(sel.shape), sspec],
        out_specs=out_specs, out_shape=out_shape,
        scratch_shapes=[pltpu.VMEM(sshape, F32)],
        compiler_params=_params("parallel", "arbitrary"),
    )(xs, bm, cm, u, dt_bias, a_log, sel, s0)
    return res if need_y else (None, res[0])


def _ssd_fin_kernel(yf_ref, yb_ref, xs_ref, z_ref, d_ref, g_ref, o_ref):
    z = z_ref[...]
    y = (yf_ref[...] + yb_ref[...] + d_ref[...] * xs_ref[...]) * (z * _sigmoid(z))
    o_ref[...] = _rms(y, g_ref[...]).astype(o_ref.dtype)


def _ssd_finish(yf, yb, xs, u, row0, d_skip, norm_g, total_rows, into=None):
    nrows = yf.shape[0]
    tq = _tile(nrows, 512)
    blk0 = row0 // tq
    tok = pl.BlockSpec((tq, SSD_DIM), lambda i: (i, 0))
    return _rows_call(
        _ssd_fin_kernel, (yf, yb, xs, u, d_skip, norm_g),
        [tok, tok, tok, pl.BlockSpec((tq, U_Z_W), lambda i: (blk0 + i, U_Z // U_Z_W)),
         _const_spec((1, SSD_DIM)), _const_spec((1, SSD_DIM))],
        pl.BlockSpec((tq, SSD_DIM), lambda i: (blk0 + i, 0)),
        jax.ShapeDtypeStruct((total_rows, SSD_DIM), BF16), into,
        grid=(nrows // tq,),
        compiler_params=_params("parallel"),
    )


def _rwkv_prep_kernel(cur_ref, prev_ref, next_ref, mu_ref, w0_ref, w2_ref, a0_ref, a2_ref, g2_ref,
                      kk_ref, ka_ref, rk_ref, bd_ref,
                      p_o, g_o, bonus_o, scr, *, tq, tiles_per_seq, colmajor):
    r_o, lwf_o, lwb_o, k_o, v_o, kk_o, ka_o = (
        p_o.at[:, c * RWKV_DIM:(c + 1) * RWKV_DIM] for c in (RP_R, RP_LWF, RP_LWB, RP_K, RP_V, RP_KK, RP_KA))
    if colmajor:
        i = pl.program_id(0)
        first = (i % tiles_per_seq) == 0
        last = (i % tiles_per_seq) == tiles_per_seq - 1
        u = jnp.swapaxes(cur_ref[...], 0, 1).reshape(tq, cur_ref.shape[-1])
        scr[SUBLANE:SUBLANE + tq, :] = u
        scr[SUBLANE - 1:SUBLANE, :] = jnp.where(first, 0.0, prev_ref[SUBLANE - 1, SUBLANE - 1:SUBLANE, :])
        scr[SUBLANE + tq:SUBLANE + tq + 1, :] = jnp.where(last, 0.0, next_ref[0, 0:1, :])
    else:
        _fill_halo(scr, cur_ref, prev_ref, next_ref, tq, tiles_per_seq)
        u = cur_ref[...]
    shift = 0.5 * (scr[pl.ds(SUBLANE - 1, tq), :] + scr[pl.ds(SUBLANE + 1, tq), :]) - u
    x = u + shift * mu_ref[...]
    r = x[:, 0:RWKV_DIM]
    k = x[:, RWKV_DIM:2 * RWKV_DIM]
    v = x[:, 2 * RWKV_DIM:3 * RWKV_DIM]
    bd = bd_ref[...]
    for d, (off, out) in enumerate(((RW_WF, lwf_o), (RW_WB, lwb_o))):
        lo = jnp.tanh(x[:, off:off + LANE])
        wl = -_softplus(-(w0_ref[d:d + 1, :] + _dot(lo, w2_ref[d]))) - 0.5
        out[...] = -jnp.exp(wl)
    a = _sigmoid(a0_ref[...] + _dot(x[:, RW_AL:RW_AL + LANE], a2_ref[...]))
    kk = k * kk_ref[...]
    kk = kk * lax.rsqrt(_blocksum(kk * kk, bd) + 1e-12)
    kmod = k * (1.0 + (a - 1.0) * ka_ref[...])
    r_o[...] = r
    k_o[...] = kmod
    v_o[...] = v
    kk_o[...] = kk
    ka_o[...] = kk * a
    g_o[...] = _dot(_sigmoid(x[:, RW_GL:RW_GL + GATE_LORA]), g2_ref[...])
    bonus_o[...] = _blocksum(r * kmod * rk_ref[...], bd) * v


def _rwkv_prep(u, row0, nseq, seqlen, p, grid_rows=None):
    nrows = nseq * seqlen
    colmajor = grid_rows is not None
    if colmajor:
        assert row0 == 0 and grid_rows % SUBLANE == 0 and seqlen == grid_rows * GRID_W
        ncg, rb = GRID_W // SUBLANE, grid_rows // SUBLANE
        tq = grid_rows * SUBLANE
        u = u.reshape(u.shape[0] // GRID_W, GRID_W, U_RWKV_W)
        halo = (SUBLANE, SUBLANE, U_RWKV_W)
        cur = pl.BlockSpec((grid_rows, SUBLANE, U_RWKV_W), lambda i: (i // ncg, i % ncg, 0))
        prev = pl.BlockSpec(halo, lambda i: ((i // ncg) * rb + rb - 1, jnp.maximum(i % ncg - 1, 0), 0))
        nxt = pl.BlockSpec(halo, lambda i: ((i // ncg) * rb, jnp.minimum(i % ncg + 1, ncg - 1), 0))
    else:
        tq = _tile(seqlen, 256)
        cur, prev, nxt = _halo_specs(tq, U_RWKV_W, row0 // tq, 0, u.shape[0] // SUBLANE)
    consts = [p["mu"], p["w0"], p["w2"], p["a0"], p["a2"], p["g2"], p["k_k"], p["k_a"], p["r_k"], p["bd"]]
    tok = pl.BlockSpec((tq, RWKV_DIM), lambda i: (i, 0))
    return pl.pallas_call(
        functools.partial(_rwkv_prep_kernel, tq=tq, tiles_per_seq=seqlen // tq, colmajor=colmajor),
        grid=(nrows // tq,),
        in_specs=[cur, prev, nxt] + [_const_spec(c.shape) for c in consts],
        out_specs=[pl.BlockSpec((tq, RP_N * RWKV_DIM), lambda i: (i, 0)), tok, tok],
        out_shape=[jax.ShapeDtypeStruct((nrows, RP_N * RWKV_DIM), F32)]
        + [jax.ShapeDtypeStruct((nrows, RWKV_DIM), F32)] * 2,
        scratch_shapes=[pltpu.VMEM((tq + 2 * SUBLANE, U_RWKV_W), F32)],
        compiler_params=_params("parallel"),
    )(u, u, u, *consts)


def _rwkv_scan_kernel(*refs, C, sub, need_y):
    packed = refs[0:2]
    s0_refs = refs[2:4]
    if need_y:
        y_refs, sfin_refs, st = refs[4:6], refs[6:8], refs[8]
    else:
        sfin_refs, st = refs[4:6], refs[6]

    @pl.when(pl.program_id(1) == 0)
    def _():
        st[0] = s0_refs[0][...]
        st[1] = s0_refs[1][...]

    n2 = 2 * C
    row = lax.broadcasted_iota(jnp.int32, (n2, n2), 0)
    col = lax.broadcasted_iota(jnp.int32, (n2, n2), 1)
    t, s = row & (C - 1), col & (C - 1)
    strict = ((s < t), (s > t))
    incl = ((s <= t), (s >= t))
    eye = jnp.where(row == col, 1.0, 0.0)
    ti = lax.broadcasted_iota(jnp.int32, (C, C), 0)
    tk = lax.broadcasted_iota(jnp.int32, (C, C), 1)
    tri = (jnp.where(tk <= ti, 1.0, 0.0).astype(BF16), jnp.where(tk >= ti, 1.0, 0.0).astype(BF16))
    head0 = lax.broadcasted_iota(jnp.int32, (C, LANE), 1) < RWKV_HEAD_DIM

    def stack(x):
        return jnp.concatenate([jnp.where(head0, x, 0.0), jnp.where(head0, 0.0, x)], axis=0)

    probs = [(d, p) for d in (0, 1) for p in range(RWKV_PAIRS)]
    blocks = ((RP_R, RP_LWF, RP_K, RP_V, RP_KK, RP_KA), (RP_R, RP_LWB, RP_K, RP_V, RP_KK, RP_KA))
    for step in range(sub):
        _rwkv_chunk(packed, y_refs if need_y else None, st, ((step * C), (sub - 1 - step) * C), C, probs, blocks,
                    strict, incl, eye, tri, stack)
    sfin_refs[0][...] = st[0]
    sfin_refs[1][...] = st[1]


def _rwkv_chunk(packed, y_refs, st, offs, C, probs, blocks, strict, incl, eye, tri, stack):
    n2 = 2 * C
    need_y = y_refs is not None
    ld = lambda d, p, i: packed[d][offs[d]:offs[d] + C,
                                   blocks[d][i] * RWKV_DIM + p * LANE:blocks[d][i] * RWKV_DIM + (p + 1) * LANE]
    lw = [ld(d, p, 1) for d, p in probs]
    cs = [_dot_exact_lhs(tri[d], x) for (d, p), x in zip(probs, lw)]
    tot = [x[C - 1:C, :] if d == 0 else x[0:1, :] for (d, p), x in zip(probs, cs)]
    ems = [jnp.exp(-x) for x in cs]
    edec = [jnp.exp(a - x) for a, x in zip(tot, cs)]
    kt = [stack(ld(d, p, 4) * jnp.exp(x - w)) for (d, p), x, w in zip(probs, cs, lw)]
    rt = [stack(ld(d, p, 0) * jnp.exp(x)) for (d, p), x in zip(probs, cs)]
    at = [stack(ld(d, p, 5) * e) for (d, p), e in zip(probs, ems)]
    kb = [stack(ld(d, p, 2) * e) for (d, p), e in zip(probs, ems)]
    at2 = [stack(ld(d, p, 5) * e) for (d, p), e in zip(probs, edec)]
    kb2 = [stack(ld(d, p, 2) * e) for (d, p), e in zip(probs, edec)]
    vs_t = [stack(ld(d, p, 3)).T for d, p in probs]
    big = [_dot_nt(jnp.concatenate([a, b], axis=0), jnp.concatenate([c, e], axis=0))
           for a, b, c, e in zip(kt, rt, at, kb)]
    npow = [-jnp.where(strict[d], x[:n2, :n2], 0.0) for (d, p), x in zip(probs, big)]
    tinv = [eye + x for x in npow]
    npow = [_dot(x, x) for x in npow]
    for step in range(int(math.log2(C)) - 1):
        if step < int(math.log2(C)) - 2:
            both = [_dot(jnp.concatenate([t, x], axis=0), x) for t, x in zip(tinv, npow)]
            tinv = [t + b[:n2] for t, b in zip(tinv, both)]
            npow = [b[n2:] for b in both]
        else:
            tinv = [t + _dot(t, x) for t, x in zip(tinv, npow)]
    state = [st[d, p] for d, p in probs]
    lhs = [jnp.concatenate(
        [jnp.concatenate([a, jnp.where(strict[d], x[:n2, n2:], 0.0)], axis=1),
         jnp.concatenate([b, jnp.where(incl[d], x[n2:, n2:], 0.0)], axis=1)], axis=0)
        for (d, p), x, a, b in zip(probs, big, kt, rt)]
    xy = [_dot_nt(a, jnp.concatenate([x, v], axis=1)) for a, x, v in zip(lhs, state, vs_t)]
    u_m = [-_dot(x, y[:n2]) for x, y in zip(tinv, xy)]
    if need_y:
        for (d, p), x, u, y in zip(probs, big, u_m, xy):
            yy = y[n2:] + _dot(jnp.where(incl[d], x[n2:, :n2], 0.0), u)
            y_refs[d][offs[d]:offs[d] + C, p * LANE:(p + 1) * LANE] = yy[:C] + yy[C:]
    for (d, p), x, a, u, v, k2, a2 in zip(probs, state, tot, u_m, vs_t, kb2, at2):
        st[d, p] = x * jnp.exp(a) + _dot(jnp.concatenate([u.T, v], axis=1), jnp.concatenate([a2, k2], axis=0))


def _rwkv_scan(packed, nseq, seqlen, s0f, s0b, need_y):
    C = RWKV_CHUNK
    sub = max(s for s in (4, 2, 1) if seqlen % (s * C) == 0)
    nc = seqlen // (sub * C)
    fwd, bwd = (lambda b, c: (b * nc + c, 0)), (lambda b, c: (b * nc + nc - 1 - c, 0))
    tok_f, tok_b = pl.BlockSpec((sub * C, RWKV_DIM), fwd), pl.BlockSpec((sub * C, RWKV_DIM), bwd)
    pk_f = pl.BlockSpec((sub * C, RP_N * RWKV_DIM), fwd)
    pk_b = pl.BlockSpec((sub * C, RP_N * RWKV_DIM), bwd)
    sshape = (RWKV_PAIRS, LANE, LANE)
    sspec = pl.BlockSpec((None,) + sshape, lambda b, c: (b, 0, 0, 0))
    sstruct = jax.ShapeDtypeStruct((nseq,) + sshape, F32)
    out_specs, out_shape = [sspec, sspec], [sstruct, sstruct]
    if need_y:
        ystruct = jax.ShapeDtypeStruct((nseq * seqlen, RWKV_DIM), F32)
        out_specs, out_shape = [tok_f, tok_b] + out_specs, [ystruct, ystruct] + out_shape
    res = pl.pallas_call(
        functools.partial(_rwkv_scan_kernel, C=C, sub=sub, need_y=need_y),
        grid=(nseq, nc),
        in_specs=[pk_f, pk_b, sspec, sspec],
        out_specs=out_specs, out_shape=out_shape,
        scratch_shapes=[pltpu.VMEM((2,) + sshape, F32)],
        compiler_params=_params("parallel", "arbitrary"),
    )(packed, packed, s0f, s0b)
    return tuple(res) if need_y else (None, None) + tuple(res)


def _rwkv_fin_kernel(yf_ref, yb_ref, bonus_ref, g_ref, lg_ref, lb_ref, bd_ref, o_ref):
    y = yf_ref[...] + yb_ref[...]
    bd = bd_ref[...]
    dv = y - _blocksum(y, bd) * (1.0 / RWKV_HEAD_DIM)
    var = _blocksum(dv * dv, bd) * (1.0 / RWKV_HEAD_DIM)
    yn = dv * lax.rsqrt(var + GN_EPS) * lg_ref[...] + lb_ref[...]
    out = (yn + bonus_ref[...]) * g_ref[...]
    if len(o_ref.shape) == 3:
        grid_rows, ncols, c = o_ref.shape
        out = jnp.swapaxes(out.reshape(ncols, grid_rows, c), 0, 1)
    o_ref[...] = out.astype(o_ref.dtype)


def _rwkv_finish(yf, yb, bonus, g, ln_g, ln_b, bd, row0, total_rows, into=None, grid_rows=None):
    nrows = yf.shape[0]
    if grid_rows is None:
        tq = _tile(nrows, 512)
        blk0 = row0 // tq
        out_spec = pl.BlockSpec((tq, RWKV_DIM), lambda i: (blk0 + i, 0))
        out_shape = jax.ShapeDtypeStruct((total_rows, RWKV_DIM), BF16)
    else:
        assert row0 == 0
        ncols = 16
        tq, ncg = ncols * grid_rows, GRID_W // ncols
        out_spec = pl.BlockSpec((grid_rows, ncols, RWKV_DIM), lambda i: (i // ncg, i % ncg, 0))
        out_shape = jax.ShapeDtypeStruct((total_rows // GRID_W, GRID_W, RWKV_DIM), BF16)
        into = None if into is None else into.reshape(out_shape.shape)
    tok = pl.BlockSpec((tq, RWKV_DIM), lambda i: (i, 0))
    res = _rows_call(
        _rwkv_fin_kernel, (yf, yb, bonus, g, ln_g, ln_b, bd),
        [tok] * 4 + [_const_spec((1, RWKV_DIM))] * 2 + [_const_spec(bd.shape)],
        out_spec, out_shape, into,
        grid=(nrows // tq,),
        compiler_params=_params("parallel"),
    )
    return res.reshape(total_rows, RWKV_DIM)


def _pad_cols(w, n):
    return jnp.pad(w, [(0, 0)] * (w.ndim - 1) + [(0, n - w.shape[-1])])


def _pad_rows(w, n):
    return jnp.pad(w, [(0, 0)] * (w.ndim - 2) + [(0, n - w.shape[-2]), (0, 0)])


def kernel(x, c, ctx, c_ctx, mod_w, mod_b, norm_g, w_in, conv_w, conv_b, conv_ln_g, conv_ln_b, conv_out, ssd_conv_w, ssd_conv_b, ssd_A_log, ssd_dt_bias, ssd_D, ssd_norm_g, ssd_out, fourier_out, rwkv_mu, rwkv_w0, rwkv_w2, rwkv_a0, rwkv_a2, rwkv_g2, rwkv_k_k, rwkv_k_a, rwkv_r_k, rwkv_ln_g, rwkv_ln_b, rwkv_out, w_o, mlp_up, mlp_down):
    B, L, D = x.shape
    CL = ctx.shape[1]
    depth = mod_w.shape[0]
    rows = L // GRID_W
    n_lat, n_ctx = B * L, B * CL
    n_all = n_lat + n_ctx
    tm = _tile(math.gcd(L, n_ctx), 1024)
    tm_small = _tile(tm, 512)
    tm_merge = _tile(tm, 256)
    mod_idx = lambda t: (lambda i: jnp.minimum((i * t) // L, B))

    w_u, w_rw, w_gate = _split_w_in(w_in)
    down_w, wo_w = mlp_down.astype(BF16), w_o.astype(BF16)
    branch_w = [conv_out.astype(BF16), ssd_out.astype(BF16), fourier_out.astype(BF16), rwkv_out.astype(BF16)]
    mu = rwkv_mu
    mu_cols = jnp.concatenate(
        [mu[:, :3 * RWKV_DIM]] + [_pad_cols(mu[:, a:b], LANE) for a, b in ((1536, 1600), (1600, 1664), (1664, 1728))]
        + [mu[:, 1728:]], axis=1)
    head_of = jnp.arange(RWKV_DIM, dtype=jnp.int32) // RWKV_HEAD_DIM
    bd = (head_of[:, None] == head_of[None, :]).astype(BF16)
    gidx = jnp.arange(FOURIER_DIM, dtype=jnp.int32)
    cc, sc_ = _dft_mats(FOURIER_GROUP_DIM, FOURIER_GROUP_DIM ** -0.5)
    same_group = (gidx[:, None] // FOURIER_GROUP_DIM) == (gidx[None, :] // FOURIER_GROUP_DIM)
    tile_g = lambda m: jnp.where(same_group, jnp.tile(m, (FOURIER_GROUPS, FOURIER_GROUPS)), 0.0)
    chan_w = jnp.concatenate([tile_g(cc), -tile_g(sc_)], axis=1).astype(BF16)
    conv_w_p = _pad_rows(conv_w, 32)
    ssd_conv_w_p = _pad_rows(ssd_conv_w, SUBLANE)
    dt_bias_p = _pad_cols(ssd_dt_bias.reshape(depth, 1, 2 * SSD_HEADS), U_DT_W)
    a_log_p = _pad_cols(ssd_A_log.reshape(depth, 1, 2 * SSD_HEADS), U_DT_W)
    d_skip = jnp.repeat(ssd_D, SSD_HEAD_DIM, axis=1).reshape(depth, 1, SSD_DIM)

    cond = jnp.concatenate([c, c_ctx[None, :], jnp.zeros((SUBLANE - (B + 1) % SUBLANE, D), F32)], axis=0)
    mod = _mod_all(cond, mod_w, mod_b)

    xs = jnp.concatenate([x.reshape(n_lat, D), ctx.reshape(n_ctx, D)], axis=0)
    ssd_zero = jnp.zeros((B, SSD_GROUPS, SSD_STATE, SSD_HEADS_PER_GROUP * SSD_HEAD_DIM), F32)
    rwkv_zero = jnp.zeros((B, RWKV_PAIRS, LANE, LANE), F32)

    mods = lambda l: tuple(mod[l, :, j * D:(j + 1) * D].reshape(-1, 1, D) for j in range(6))
    h = _norm_mod(xs, norm_g[0, 0].reshape(1, D), *mods(0)[1::-1], n_all, mod_idx(tm), tm)
    for i in range(depth):
        last = i == depth - 1
        m_out = n_lat if last else n_all
        sh1, sc1, g1, sh2, sc2, g2 = mods(i)
        ng = norm_g[i].reshape(4, 1, D)

        u = _matmul(h, w_u, n_all, tm, F32, tn_cap=2304, layer=i, w_is_nk=True)
        gates = _matmul(h, w_gate, m_out, tm, BF16, epilogue="sigmoid", tn_cap=2048, layer=i, w_is_nk=True)

        cargs = (conv_w_p[i], conv_b[i][None], conv_ln_g[i][None], conv_ln_b[i][None])
        a_conv = _conformer(u, 0, n_lat, GRID_W, *cargs, m_out)
        z = _matmul(u, chan_w, m_out, tm, BF16, a_col=U_FFT // U_FFT_W)
        a_fft = _fourier(z, 0, B, L, m_out)
        if not last:
            a_conv = _conformer(u, n_lat, n_ctx, CL, *cargs, m_out, into=a_conv)
            a_fft = _fourier(z, n_lat, B, CL, m_out, into=a_fft)

        ssd_args = (dt_bias_p[i], a_log_p[i])
        xs_c, bm_c, cm_c = _ssd_conv(u, n_lat, B, CL, ssd_conv_w_p[i], ssd_conv_b[i][None])
        yfc, sf = _ssd_scan(xs_c, bm_c, cm_c, u, n_lat, B, CL, *ssd_args, ssd_zero, False, not last)
        ybc, sb = _ssd_scan(xs_c, bm_c, cm_c, u, n_lat, B, CL, *ssd_args, ssd_zero, True, not last)
        xs_l, bm_l, cm_l = _ssd_conv(u, 0, B, L, ssd_conv_w_p[i], ssd_conv_b[i][None])
        yfl, _ = _ssd_scan(xs_l, bm_l, cm_l, u, 0, B, L, *ssd_args, sf, False, True)
        ybl, _ = _ssd_scan(xs_l, bm_l, cm_l, u, 0, B, L, *ssd_args, sb, True, True)
        a_ssd = _ssd_finish(yfl, ybl, xs_l, u, 0, d_skip[i], ssd_norm_g[i][None], m_out)
        if not last:
            a_ssd = _ssd_finish(yfc, ybc, xs_c, u, n_lat, d_skip[i], ssd_norm_g[i][None], m_out, into=a_ssd)

        rp = dict(mu=mu_cols[i][None], w0=rwkv_w0[i], w2=_pad_rows(rwkv_w2[i], LANE).astype(BF16),
                  a0=rwkv_a0[i][None], a2=_pad_rows(rwkv_a2[i], LANE).astype(BF16), g2=rwkv_g2[i].astype(BF16),
                  k_k=rwkv_k_k[i][None], k_a=rwkv_k_a[i][None], r_k=rwkv_r_k[i].reshape(1, RWKV_DIM), bd=bd)
        fin = (rwkv_ln_g[i][None], rwkv_ln_b[i][None], bd)
        u_r = _matmul(h, w_rw, n_all, tm, F32, tn_cap=2048, layer=i, w_is_nk=True)
        packed_c, g_c, bonus_c = _rwkv_prep(u_r, n_lat, B, CL, rp)
        yfc, ybc, s_f, s_b = _rwkv_scan(packed_c, B, CL, rwkv_zero, rwkv_zero, not last)
        packed, g_, bonus = _rwkv_prep(u_r, 0, B, L, rp, grid_rows=rows)
        yfl, ybl, _, _ = _rwkv_scan(packed, B, L, s_f, s_b, True)
        a_rwkv = _rwkv_finish(yfl, ybl, bonus, g_, *fin, 0, m_out, grid_rows=rows)
        if not last:
            a_rwkv = _rwkv_finish(yfc, ybc, bonus_c, g_c, *fin, n_lat, m_out, into=a_rwkv)

        m = _merge(gates, [a_conv, a_ssd, a_fft, a_rwkv], branch_w, m_out, tm_merge, i)
        xs, h2 = _matmul_residual(m, wo_w, xs, g1, ng[1], m_out, mod_idx(tm_small), tm_small,
                                  nxt=(ng[2], sc2, sh2), layer=i)

        mid = _matmul_f32w(h2, mlp_up, m_out, tm, BF16, epilogue="relu2", layer=i)
        if last:
            xs = _matmul_residual(mid, down_w, xs, g2, ng[3], m_out, mod_idx(tm_small), tm_small, layer=i)
        else:
            xs, h = _matmul_residual(mid, down_w, xs, g2, ng[3], m_out, mod_idx(tm_small), tm_small,
                                     nxt=(norm_g[i + 1, 0].reshape(1, D), *mods(i + 1)[1::-1]), layer=i)

    return xs[:n_lat].reshape(B, L, D)
```

```python
import functools
import math

import jax
import jax.numpy as jnp
import numpy as np
from jax import lax
from jax.experimental import pallas as pl
from jax.experimental.pallas import tpu as pltpu

F32 = jnp.float32
BF16 = jnp.bfloat16

GRID_W = 64
RMS_EPS = 1e-6
LN_EPS = 1e-5
GN_EPS = 64e-5
CONV_DIM = 512
CONV_WIDTH = 31
SSD_HEADS = 12
SSD_HEAD_DIM = 64
SSD_DIM = SSD_HEADS * SSD_HEAD_DIM
SSD_GROUPS = 4
SSD_HEADS_PER_GROUP = SSD_HEADS // SSD_GROUPS
SSD_STATE = 128
SSD_CONV = 5
SSD_BC = SSD_GROUPS * SSD_STATE
SSD_XBC = SSD_DIM + 2 * SSD_BC
SSD_IN = SSD_DIM + SSD_XBC + 2 * SSD_HEADS
SSD_CHUNK = 128
FOURIER_GROUPS = 4
FOURIER_GROUP_DIM = 128
FOURIER_DIM = FOURIER_GROUPS * FOURIER_GROUP_DIM
RWKV_HEADS = 8
RWKV_HEAD_DIM = 64
RWKV_DIM = RWKV_HEADS * RWKV_HEAD_DIM
RWKV_PAIRS = RWKV_HEADS // 2
DECAY_LORA = 64
ICL_LORA = 64
GATE_LORA = 128
RWKV_IN = 3 * RWKV_DIM + 2 * DECAY_LORA + ICL_LORA + GATE_LORA
RWKV_CHUNK = 64
REC_IN = SSD_IN + RWKV_IN
N_BRANCH = 4

LANE = 128
SUBLANE = 8
BF16_SUBLANE = 16
VMEM_LIMIT = 56 * 1024 * 1024
MM_ROWS = 1024
RES_ROWS = 512
MERGE_ROWS = 256
ELEM_ROWS = 512
SEQ_ROWS = 256
CONV_PAD = 16

U_XBC, U_XBC_W = 0, SSD_XBC
U_DT, U_DT_W = 1792, LANE
U_CONV, U_CONV_W = 2048, 2 * CONV_DIM
U_Z, U_Z_W = 3072, SSD_DIM
U_FFT, U_FFT_W = 4096, FOURIER_DIM
U_N = 4608
U_RWKV_W = 2048
RW_WF, RW_WB, RW_AL, RW_GL = 1536, 1664, 1792, 1920
RP_R, RP_LWF, RP_LWB, RP_K, RP_V, RP_KK, RP_KA, RP_N = 0, 1, 2, 3, 4, 5, 6, 7


def _params(*sem):
    return pltpu.CompilerParams(dimension_semantics=sem, vmem_limit_bytes=VMEM_LIMIT)


def _tile(n, cap, quantum=SUBLANE):
    if n <= cap:
        return n
    t = (cap // quantum) * quantum
    while t >= quantum:
        if n % t == 0:
            return t
        t -= quantum
    raise ValueError(f"no tile for {n} under {cap}")


def _const_spec(shape):
    nd = len(shape)
    return pl.BlockSpec(shape, lambda *_: (0,) * nd)


def _dot(a, b):
    return jnp.dot(a.astype(BF16), b.astype(BF16), preferred_element_type=F32)


def _dot_nt(a, b):
    return lax.dot_general(a.astype(BF16), b.astype(BF16), (((1,), (1,)), ((), ())),
                           preferred_element_type=F32)


def _split3(x):
    h1 = x.astype(BF16)
    r1 = x - h1.astype(F32)
    h2 = r1.astype(BF16)
    h3 = (r1 - h2.astype(F32)).astype(BF16)
    return h1, h2, h3


def _dot_exact_lhs(m, x):
    h1, h2, h3 = _split3(x)
    d = lambda h: jnp.dot(m, h, preferred_element_type=F32)
    return d(h1) + d(h2) + d(h3)


def _blocksum(x, bd):
    hi = x.astype(BF16)
    lo = (x - hi.astype(F32)).astype(BF16)
    return (jnp.dot(hi, bd, preferred_element_type=F32) + jnp.dot(lo, bd, preferred_element_type=F32))


def _sigmoid(x):
    return 1.0 / (1.0 + jnp.exp(-x))


def _softplus(x):
    return jnp.maximum(x, 0.0) + jnp.log1p(jnp.exp(-jnp.abs(x)))


def _rms(y, g):
    return y * lax.rsqrt(jnp.mean(y * y, axis=-1, keepdims=True) + RMS_EPS) * g


_O_RWKV, _O_CONV = SSD_IN, REC_IN
_O_FFT, _O_GATE = REC_IN + 2 * CONV_DIM, REC_IN + 2 * CONV_DIM + FOURIER_DIM
_O_WF = _O_RWKV + 3 * RWKV_DIM
_W_IN_PIECES = (
    ("rw", 0, _O_RWKV, 3 * RWKV_DIM), ("rw", RW_WF, _O_WF, DECAY_LORA),
    ("rw", RW_WB, _O_WF + DECAY_LORA, DECAY_LORA), ("rw", RW_AL, _O_WF + 2 * DECAY_LORA, ICL_LORA),
    ("rw", RW_GL, _O_WF + 2 * DECAY_LORA + ICL_LORA, GATE_LORA),
    ("u", U_XBC, SSD_DIM, SSD_XBC), ("u", U_DT, SSD_DIM + SSD_XBC, 2 * SSD_HEADS),
    ("u", U_CONV, _O_CONV, 2 * CONV_DIM), ("u", U_Z, 0, SSD_DIM), ("u", U_FFT, _O_FFT, FOURIER_DIM),
)


def _split_w_in(w_in):
    depth, d, _ = w_in.shape
    w_t = jnp.swapaxes(w_in, 1, 2)

    def assemble(name, n_rows):
        parts, pos = [], 0
        for _, dst, src, width in sorted(p for p in _W_IN_PIECES if p[0] == name):
            if dst > pos:
                parts.append(jnp.zeros((depth, dst - pos, d), w_in.dtype))
            parts.append(w_t[:, src:src + width, :])
            pos = dst + width
        if n_rows > pos:
            parts.append(jnp.zeros((depth, n_rows - pos, d), w_in.dtype))
        return jnp.concatenate(parts, axis=1).astype(BF16)

    return assemble("u", U_N), assemble("rw", U_RWKV_W), w_t[:, _O_GATE:, :].astype(BF16)


def _mod_kernel(a_ref, w_ref, b_ref, o_ref):
    a = a_ref[...]
    a = a * _sigmoid(a)
    o_ref[...] = _dot(a, w_ref[...]) + b_ref[...]


def _mod_all(cond, mod_w, mod_b):
    depth, d, n = mod_w.shape
    tn = _tile(n, 1024, LANE)
    return pl.pallas_call(
        _mod_kernel,
        grid=(depth, n // tn),
        in_specs=[_const_spec(cond.shape),
                  pl.BlockSpec((None, d, tn), lambda l, j: (l, 0, j)),
                  pl.BlockSpec((None, 1, tn), lambda l, j: (l, 0, j))],
        out_specs=pl.BlockSpec((None, cond.shape[0], tn), lambda l, j: (l, 0, j)),
        out_shape=jax.ShapeDtypeStruct((depth, cond.shape[0], n), F32),
        compiler_params=_params("parallel", "parallel"),
    )(cond, mod_w, mod_b.reshape(depth, 1, n))


def _norm_mod_kernel(x_ref, g_ref, sc_ref, sh_ref, o_ref):
    o_ref[...] = (_rms(x_ref[...], g_ref[...]) * (1.0 + sc_ref[...]) + sh_ref[...]).astype(o_ref.dtype)


def _norm_mod(x, g, sc, sh, m_rows, mod_idx, tm):
    d = x.shape[1]
    return pl.pallas_call(
        _norm_mod_kernel,
        grid=(m_rows // tm,),
        in_specs=[pl.BlockSpec((tm, d), lambda i: (i, 0)),
                  _const_spec((1, d)),
                  pl.BlockSpec((None, 1, d), lambda i: (mod_idx(i), 0, 0)),
                  pl.BlockSpec((None, 1, d), lambda i: (mod_idx(i), 0, 0))],
        out_specs=pl.BlockSpec((tm, d), lambda i: (i, 0)),
        out_shape=jax.ShapeDtypeStruct((m_rows, d), BF16),
        compiler_params=_params("parallel"),
    )(x, g, sc, sh)


def _mm_kernel(a_ref, w_ref, o_ref, *, epilogue, w_is_nk=False):
    acc = _dot_nt(a_ref[...], w_ref[...]) if w_is_nk else _dot(a_ref[...], w_ref[...])
    if epilogue == "relu2":
        acc = jnp.square(jnp.maximum(acc, 0.0))
    elif epilogue == "sigmoid":
        acc = _sigmoid(acc)
    o_ref[...] = acc.astype(o_ref.dtype)


def _layer_spec(w, layer, block, index):
    if w.ndim == 2:
        return pl.BlockSpec(block, index)
    return pl.BlockSpec((None,) + block, lambda *ids: (layer,) + index(*ids))


def _matmul(a, w, m_rows, tm, out_dtype, epilogue=None, a_col=0, a_row=0, tn_cap=1024, layer=0, w_is_nk=False):
    n, k = w.shape[-2:] if w_is_nk else w.shape[-2:][::-1]
    tn = _tile(n, tn_cap, LANE)
    w_spec = (_layer_spec(w, layer, (tn, k), lambda i, j: (j, 0)) if w_is_nk
              else _layer_spec(w, layer, (k, tn), lambda i, j: (0, j)))
    return pl.pallas_call(
        functools.partial(_mm_kernel, epilogue=epilogue, w_is_nk=w_is_nk),
        grid=(m_rows // tm, n // tn),
        in_specs=[pl.BlockSpec((tm, k), lambda i, j: (a_row + i, a_col)), w_spec],
        out_specs=pl.BlockSpec((tm, tn), lambda i, j: (i, j)),
        out_shape=jax.ShapeDtypeStruct((m_rows, n), out_dtype),
        compiler_params=_params("parallel", "arbitrary"),
    )(a, w)


def _mm_wcast_kernel(a_ref, w_ref, o_ref, wb_ref, *, epilogue):
    @pl.when(pl.program_id(1) == 0)
    def _():
        wb_ref[...] = w_ref[...].astype(wb_ref.dtype)

    _mm_kernel(a_ref, wb_ref, o_ref, epilogue=epilogue)


def _matmul_f32w(a, w, m_rows, tm, out_dtype, epilogue=None, tn_cap=1024, layer=0):
    k, n = w.shape[-2:]
    tn = _tile(n, tn_cap, LANE)
    return pl.pallas_call(
        functools.partial(_mm_wcast_kernel, epilogue=epilogue),
        grid=(n // tn, m_rows // tm),
        in_specs=[pl.BlockSpec((tm, k), lambda j, i: (i, 0)),
                  _layer_spec(w, layer, (k, tn), lambda j, i: (0, j))],
        out_specs=pl.BlockSpec((tm, tn), lambda j, i: (i, j)),
        out_shape=jax.ShapeDtypeStruct((m_rows, n), out_dtype),
        scratch_shapes=[pltpu.VMEM((k, tn), BF16)],
        compiler_params=_params("parallel", "arbitrary"),
    )(a, w)


def _mm_res_kernel(a_ref, w_ref, x_ref, gate_ref, g_ref, *rest, with_next, nk):
    if with_next:
        g2_ref, sc_ref, sh_ref, o_ref, h_ref = rest
    else:
        (o_ref,) = rest
    def finish(y):
        xn = x_ref[...] + gate_ref[...] * _rms(y, g_ref[...])
        o_ref[...] = xn
        if with_next:
            h_ref[...] = (_rms(xn, g2_ref[...]) * (1.0 + sc_ref[...]) + sh_ref[...]).astype(h_ref.dtype)

    if nk == 1:
        finish(_dot(a_ref[...], w_ref[...]))
        return
    k = pl.program_id(1)

    @pl.when(k == 0)
    def _():
        o_ref[...] = jnp.zeros_like(o_ref)

    o_ref[...] += _dot(a_ref[...], w_ref[...])

    @pl.when(k == nk - 1)
    def _():
        finish(o_ref[...])


def _matmul_residual(a, w, x, gate, g, m_rows, mod_idx, tm, nxt=None, tk_cap=2048, layer=0):
    k, d = w.shape[-2:]
    tk = _tile(k, tk_cap, LANE)
    mod_spec = pl.BlockSpec((None, 1, d), lambda i, kk: (mod_idx(i), 0, 0))
    tile_spec = pl.BlockSpec((tm, d), lambda i, kk: (i, 0))
    in_specs = [pl.BlockSpec((tm, tk), lambda i, kk: (i, kk)), _layer_spec(w, layer, (tk, d), lambda i, kk: (kk, 0)),
                tile_spec, mod_spec, _const_spec((1, d))]
    out_specs, out_shape, args = tile_spec, jax.ShapeDtypeStruct((m_rows, d), F32), (a, w, x, gate, g)
    if nxt is not None:
        in_specs += [_const_spec((1, d)), mod_spec, mod_spec]
        out_specs, out_shape = [tile_spec, tile_spec], [out_shape, jax.ShapeDtypeStruct((m_rows, d), BF16)]
        args += tuple(nxt)
    return pl.pallas_call(
        functools.partial(_mm_res_kernel, with_next=nxt is not None, nk=k // tk),
        grid=(m_rows // tm, k // tk),
        in_specs=in_specs, out_specs=out_specs, out_shape=out_shape,
        compiler_params=_params("parallel", "arbitrary"),
    )(*args)


def _merge_kernel(gates_ref, a0, a1, a2, a3, w0, w1, w2, w3, o_ref):
    d = o_ref.shape[-1]
    m = None
    for i, (a, w) in enumerate(((a0, w0), (a1, w1), (a2, w2), (a3, w3))):
        t = gates_ref[:, i * d:(i + 1) * d].astype(F32) * _dot(a[...], w[...])
        m = t if m is None else m + t
    o_ref[...] = m.astype(o_ref.dtype)


def _merge(gates, acts, weights, m_rows, tm, layer):
    d = weights[0].shape[-1]
    return pl.pallas_call(
        _merge_kernel,
        grid=(m_rows // tm,),
        in_specs=[pl.BlockSpec((tm, N_BRANCH * d), lambda i: (i, 0))]
        + [pl.BlockSpec((tm, a.shape[1]), lambda i: (i, 0)) for a in acts]
        + [_layer_spec(w, layer, w.shape[-2:], lambda i: (0, 0)) for w in weights],
        out_specs=pl.BlockSpec((tm, d), lambda i: (i, 0)),
        out_shape=jax.ShapeDtypeStruct((m_rows, d), BF16),
        compiler_params=_params("parallel"),
    )(gates, *acts, *weights)


def _conformer_kernel(u_ref, w_ref, b_ref, lg_ref, lb_ref, o_ref, sh_ref, *, seg, nseg):
    pad = CONV_PAD
    first = pad - (CONV_WIDTH - 1) // 2
    span = seg + 2 * pad - SUBLANE
    zeros = jnp.zeros((pad, CONV_DIM), F32)
    sh_ref[0, 0:pad, :] = zeros
    sh_ref[0, pad + seg:2 * pad + seg, :] = zeros

    def body(s, carry):
        r0 = pl.multiple_of(s * seg, seg)
        u = u_ref[pl.ds(r0, seg), :]
        sh_ref[0, pad:pad + seg, :] = u[:, :CONV_DIM] * _sigmoid(u[:, CONV_DIM:])
        for r in range(1, SUBLANE):
            sh_ref[r, 0:span, :] = sh_ref[0, pl.ds(r, span), :]
        acc = jnp.zeros((seg, CONV_DIM), F32) + b_ref[...]
        for j in range(CONV_WIDTH):
            start = first + j
            acc = acc + w_ref[j:j + 1, :] * sh_ref[start % SUBLANE, pl.ds(start - start % SUBLANE, seg), :]
        mu = jnp.mean(acc, axis=-1, keepdims=True)
        dv = acc - mu
        var = jnp.mean(dv * dv, axis=-1, keepdims=True)
        v = dv * lax.rsqrt(var + LN_EPS) * lg_ref[...] + lb_ref[...]
        o_ref[pl.ds(r0, seg), :] = (v * _sigmoid(v)).astype(o_ref.dtype)
        return carry

    lax.fori_loop(0, nseg, body, 0)


def _rows_call(kernel, args, in_specs, out_spec, out_struct, into, **kw):
    if into is None:
        return pl.pallas_call(kernel, in_specs=in_specs, out_specs=out_spec, out_shape=out_struct, **kw)(*args)
    n = len(args)

    def aliased(*refs):
        kernel(*refs[:n], *refs[n + 1:])

    return pl.pallas_call(aliased, in_specs=list(in_specs) + [pl.BlockSpec(memory_space=pl.ANY)],
                          out_specs=out_spec, out_shape=out_struct, input_output_aliases={n: 0}, **kw)(*args, into)


def _conformer(u, row0, nrows, seg, conv_w, conv_b, ln_g, ln_b, total_rows, into=None):
    tq = max(seg, min(SEQ_ROWS, nrows))
    blk0 = row0 // tq
    return _rows_call(
        functools.partial(_conformer_kernel, seg=seg, nseg=tq // seg),
        (u, conv_w, conv_b, ln_g, ln_b),
        [pl.BlockSpec((tq, U_CONV_W), lambda i: (blk0 + i, U_CONV // U_CONV_W)),
         _const_spec(conv_w.shape), _const_spec((1, CONV_DIM)),
         _const_spec((1, CONV_DIM)), _const_spec((1, CONV_DIM))],
        pl.BlockSpec((tq, CONV_DIM), lambda i: (blk0 + i, 0)),
        jax.ShapeDtypeStruct((total_rows, CONV_DIM), BF16), into,
        grid=(nrows // tq,),
        scratch_shapes=[pltpu.VMEM((SUBLANE, seg + 2 * CONV_PAD, CONV_DIM), F32)],
        compiler_params=_params("parallel"),
    )


def _dft_kernel(c_ref, s_ref, z_ref, o_ref):
    o_ref[...] = (jnp.dot(c_ref[...], z_ref[:, :FOURIER_DIM], preferred_element_type=F32)
                  + jnp.dot(s_ref[...], z_ref[:, FOURIER_DIM:], preferred_element_type=F32)).astype(o_ref.dtype)


def _dft_mats(n, scale):
    idx = np.arange(n, dtype=np.int64)
    ang = ((idx[:, None] * idx[None, :]) % n).astype(np.float64) * (2.0 * math.pi / n)
    return (jnp.asarray((np.cos(ang) * scale).astype(np.float32)),
            jnp.asarray((np.sin(ang) * scale).astype(np.float32)))


def _fourier(z, row0, nseq, seqlen, total_rows, into=None):
    cl, sl = _dft_mats(seqlen, seqlen ** -0.5)
    cl, sl = cl.astype(BF16), sl.astype(BF16)
    tm = _tile(seqlen, ELEM_ROWS)
    nt = seqlen // tm
    sblk0, oblk0 = row0 // seqlen, row0 // tm
    return _rows_call(
        _dft_kernel, (cl, sl, z),
        [pl.BlockSpec((tm, seqlen), lambda i, b: (i, 0)),
         pl.BlockSpec((tm, seqlen), lambda i, b: (i, 0)),
         pl.BlockSpec((seqlen, 2 * FOURIER_DIM), lambda i, b: (sblk0 + b, 0))],
        pl.BlockSpec((tm, FOURIER_DIM), lambda i, b: (oblk0 + b * nt + i, 0)),
        jax.ShapeDtypeStruct((total_rows, FOURIER_DIM), BF16), into,
        grid=(nt, nseq),
        compiler_params=_params("parallel", "arbitrary"),
    )


def _halo_specs(tq, width, blk0, col_blk, nblk_total):
    hb = tq // SUBLANE
    cur = pl.BlockSpec((tq, width), lambda i: (blk0 + i, col_blk))
    prev = pl.BlockSpec((SUBLANE, width), lambda i: (jnp.maximum((blk0 + i) * hb - 1, 0), col_blk))
    nxt = pl.BlockSpec((SUBLANE, width),
                       lambda i: (jnp.minimum((blk0 + i + 1) * hb, nblk_total - 1), col_blk))
    return cur, prev, nxt


def _fill_halo(scr, cur_ref, prev_ref, next_ref, tq, tiles_per_seq):
    i = pl.program_id(0)
    first = (i % tiles_per_seq) == 0
    last = (i % tiles_per_seq) == tiles_per_seq - 1
    scr[0:SUBLANE, :] = jnp.where(first, 0.0, prev_ref[...])
    scr[SUBLANE:SUBLANE + tq, :] = cur_ref[...]
    scr[SUBLANE + tq:2 * SUBLANE + tq, :] = jnp.where(last, 0.0, next_ref[...])


def _ssd_conv_kernel(cur_ref, prev_ref, next_ref, w_ref, b_ref, xs_ref, bm_ref, cm_ref, scr, *, tq, tiles_per_seq):
    _fill_halo(scr, cur_ref, prev_ref, next_ref, tq, tiles_per_seq)
    half = (SSD_CONV - 1) // 2
    cw = 256
    for c0 in range(0, SSD_XBC, cw):
        acc = jnp.zeros((tq, cw), F32) + b_ref[:, c0:c0 + cw]
        for j in range(SSD_CONV):
            acc = acc + w_ref[j:j + 1, c0:c0 + cw] * scr[pl.ds(SUBLANE - half + j, tq), c0:c0 + cw]
        y = acc * _sigmoid(acc)
        if c0 < SSD_DIM:
            xs_ref[:, c0:c0 + cw] = y
        elif c0 < SSD_DIM + SSD_BC:
            bm_ref[:, c0 - SSD_DIM:c0 - SSD_DIM + cw] = y
        else:
            cm_ref[:, c0 - SSD_DIM - SSD_BC:c0 - SSD_DIM - SSD_BC + cw] = y


def _ssd_conv(u, row0, nseq, seqlen, conv_w, conv_b):
    nrows = nseq * seqlen
    tq = _tile(seqlen, SEQ_ROWS)
    cur, prev, nxt = _halo_specs(tq, U_XBC_W, row0 // tq, U_XBC // U_XBC_W, u.shape[0] // SUBLANE)
    outs = [jax.ShapeDtypeStruct((nrows, w), F32) for w in (SSD_DIM, SSD_BC, SSD_BC)]
    return pl.pallas_call(
        functools.partial(_ssd_conv_kernel, tq=tq, tiles_per_seq=seqlen // tq),
        grid=(nrows // tq,),
        in_specs=[cur, prev, nxt, _const_spec(conv_w.shape), _const_spec((1, SSD_XBC))],
        out_specs=[pl.BlockSpec((tq, w), lambda i: (i, 0)) for w in (SSD_DIM, SSD_BC, SSD_BC)],
        out_shape=outs,
        scratch_shapes=[pltpu.VMEM((tq + 2 * SUBLANE, U_XBC_W), F32)],
        compiler_params=_params("parallel"),
    )(u, u, u, conv_w, conv_b)


def _ssd_scan_kernel(xs_ref, b_ref, c_ref, dt_ref, bias_ref, alog_ref, sel_ref, s0_ref, *rest,
                     q, sub, rev, direction, need_y):
    if need_y:
        y_ref, sfin_ref, st = rest
    else:
        sfin_ref, st = rest

    @pl.when(pl.program_id(1) == 0)
    def _():
        st[...] = s0_ref[...]

    hp, gw = SSD_HEAD_DIM, SSD_HEADS_PER_GROUP * SSD_HEAD_DIM
    ti = lax.broadcasted_iota(jnp.int32, (q, q), 0)
    tk = lax.broadcasted_iota(jnp.int32, (q, q), 1)
    before = (tk >= ti) if rev else (tk <= ti)
    tri = jnp.where(before, 1.0, 0.0).astype(BF16)
    neg_a = -jnp.exp(alog_ref[...])
    sel = sel_ref[...]
    spread = lambda parts: sum(jnp.dot(p, sel, preferred_element_type=F32) for p in parts)
    lane0 = lax.broadcasted_iota(jnp.int32, (q, LANE), 1) < hp

    for step in range(sub):
        rows = slice((sub - 1 - step) * q if rev else step * q, (sub - step) * q if rev else (step + 1) * q)
        dt = _softplus(dt_ref[rows, :] + bias_ref[...])
        cs = _dot_exact_lhs(tri, dt * neg_a)
        dt_b = spread(_split3(dt)[:2])
        cs_b = spread(_split3(cs))
        tot_b = cs_b[0:1, :] if rev else cs_b[q - 1:q, :]
        xd = xs_ref[rows, :] * dt_b
        xdd = xd * jnp.exp(tot_b - cs_b)
        dec = jnp.exp(tot_b)
        if need_y:
            ecs = jnp.exp(cs_b)
            cs_t = cs.T
        for g in range(SSD_GROUPS):
            cg = c_ref[rows, g * SSD_STATE:(g + 1) * SSD_STATE]
            bg = b_ref[rows, g * SSD_STATE:(g + 1) * SSD_STATE]
            state = st[g]
            if need_y:
                cb = _dot_nt(cg, bg)
                y_off = _dot(cg, state) * ecs[:, g * gw:(g + 1) * gw]
                for j in range(SSD_HEADS_PER_GROUP):
                    h = g * SSD_HEADS_PER_GROUP + j
                    col = direction * SSD_HEADS + h
                    pair = cs_b[:, (h // 2) * LANE:(h // 2 + 1) * LANE]
                    swapped = pltpu.roll(pair, hp, axis=1)
                    cs_col = jnp.where(lane0, pair, swapped) if h % 2 == 0 else jnp.where(lane0, swapped, pair)
                    lm = jnp.where(before, jnp.exp(jnp.minimum(cs_col - cs_t[col:col + 1, :], 0.0)), 0.0)
                    y_ref[rows, h * hp:(h + 1) * hp] = (_dot(cb * lm, xd[:, h * hp:(h + 1) * hp])
                                                        + y_off[:, j * hp:(j + 1) * hp])
            st[g] = state * dec[:, g * gw:(g + 1) * gw] + _dot(bg.T, xdd[:, g * gw:(g + 1) * gw])
    sfin_ref[...] = st[...]


def _ssd_scan(xs, bm, cm, u, row0, nseq, seqlen, dt_bias, a_log, s0, rev, need_y):
    q = SSD_CHUNK
    sub = max(s for s in (4, 2, 1) if seqlen % (s * q) == 0)
    bq = sub * q
    nc = seqlen // bq
    ublk0 = row0 // bq
    pos = (lambda c: nc - 1 - c) if rev else (lambda c: c)
    row = lambda b, c: (b * nc + pos(c), 0)
    sshape = (SSD_GROUPS, SSD_STATE, SSD_HEADS_PER_GROUP * SSD_HEAD_DIM)
    sspec = pl.BlockSpec((None,) + sshape, lambda b, c: (b, 0, 0, 0))
    head_col = int(rev) * SSD_HEADS + jnp.arange(SSD_DIM, dtype=jnp.int32) // SSD_HEAD_DIM
    sel = (jnp.arange(U_DT_W, dtype=jnp.int32)[:, None] == head_col[None, :]).astype(BF16)
    out_specs, out_shape = [sspec], [jax.ShapeDtypeStruct((nseq,) + sshape, F32)]
    if need_y:
        out_specs.insert(0, pl.BlockSpec((bq, SSD_DIM), row))
        out_shape.insert(0, jax.ShapeDtypeStruct((nseq * seqlen, SSD_DIM), F32))
    res = pl.pallas_call(
        functools.partial(_ssd_scan_kernel, q=q, sub=sub, rev=rev, direction=int(rev), need_y=need_y),
        grid=(nseq, nc),
        in_specs=[pl.BlockSpec((bq, SSD_DIM), row), pl.BlockSpec((bq, SSD_BC), row),
                  pl.BlockSpec((bq, SSD_BC), row),
                  pl.BlockSpec((bq, U_DT_W), lambda b, c: (ublk0 + b * nc + pos(c), U_DT // U_DT_W)),
                  _const_spec((1, U_DT_W)), _const_spec((1, U_DT_W)), _const_spec(sel.shape), sspec],
        out_specs=out_specs, out_shape=out_shape,
        scratch_shapes=[pltpu.VMEM(sshape, F32)],
        compiler_params=_params("parallel", "arbitrary"),
    )(xs, bm, cm, u, dt_bias, a_log, sel, s0)
    return res if need_y else (None, res[0])


def _ssd_fin_kernel(yf_ref, yb_ref, xs_ref, z_ref, d_ref, g_ref, o_ref):
    z = z_ref[...]
    y = (yf_ref[...] + yb_ref[...] + d_ref[...] * xs_ref[...]) * (z * _sigmoid(z))
    o_ref[...] = _rms(y, g_ref[...]).astype(o_ref.dtype)


def _ssd_finish(yf, yb, xs, u, row0, d_skip, norm_g, total_rows, into=None):
    nrows = yf.shape[0]
    tq = _tile(nrows, ELEM_ROWS)
    blk0 = row0 // tq
    tok = pl.BlockSpec((tq, SSD_DIM), lambda i: (i, 0))
    return _rows_call(
        _ssd_fin_kernel, (yf, yb, xs, u, d_skip, norm_g),
        [tok, tok, tok, pl.BlockSpec((tq, U_Z_W), lambda i: (blk0 + i, U_Z // U_Z_W)),
         _const_spec((1, SSD_DIM)), _const_spec((1, SSD_DIM))],
        pl.BlockSpec((tq, SSD_DIM), lambda i: (blk0 + i, 0)),
        jax.ShapeDtypeStruct((total_rows, SSD_DIM), BF16), into,
        grid=(nrows // tq,),
        compiler_params=_params("parallel"),
    )


def _rwkv_prep_kernel(cur_ref, prev_ref, next_ref, mu_ref, w0_ref, w2_ref, a0_ref, a2_ref, g2_ref,
                      kk_ref, ka_ref, rk_ref, bd_ref,
                      p_o, g_o, bonus_o, scr, *, tq, tiles_per_seq, colmajor):
    r_o, lwf_o, lwb_o, k_o, v_o, kk_o, ka_o = (
        p_o.at[:, c * RWKV_DIM:(c + 1) * RWKV_DIM] for c in (RP_R, RP_LWF, RP_LWB, RP_K, RP_V, RP_KK, RP_KA))
    if colmajor:
        i = pl.program_id(0)
        first = (i % tiles_per_seq) == 0
        last = (i % tiles_per_seq) == tiles_per_seq - 1
        u = jnp.swapaxes(cur_ref[...], 0, 1).reshape(tq, cur_ref.shape[-1])
        scr[SUBLANE:SUBLANE + tq, :] = u
        scr[SUBLANE - 1:SUBLANE, :] = jnp.where(first, 0.0, prev_ref[SUBLANE - 1, SUBLANE - 1:SUBLANE, :])
        scr[SUBLANE + tq:SUBLANE + tq + 1, :] = jnp.where(last, 0.0, next_ref[0, 0:1, :])
    else:
        _fill_halo(scr, cur_ref, prev_ref, next_ref, tq, tiles_per_seq)
        u = cur_ref[...]
    shift = 0.5 * (scr[pl.ds(SUBLANE - 1, tq), :] + scr[pl.ds(SUBLANE + 1, tq), :]) - u
    x = u + shift * mu_ref[...]
    r = x[:, 0:RWKV_DIM]
    k = x[:, RWKV_DIM:2 * RWKV_DIM]
    v = x[:, 2 * RWKV_DIM:3 * RWKV_DIM]
    bd = bd_ref[...]
    for d, (off, out) in enumerate(((RW_WF, lwf_o), (RW_WB, lwb_o))):
        lo = jnp.tanh(x[:, off:off + LANE])
        wl = -_softplus(-(w0_ref[d:d + 1, :] + _dot(lo, w2_ref[d]))) - 0.5
        out[...] = -jnp.exp(wl)
    a = _sigmoid(a0_ref[...] + _dot(x[:, RW_AL:RW_AL + LANE], a2_ref[...]))
    kk = k * kk_ref[...]
    kk = kk * lax.rsqrt(_blocksum(kk * kk, bd) + 1e-12)
    kmod = k * (1.0 + (a - 1.0) * ka_ref[...])
    r_o[...] = r
    k_o[...] = kmod
    v_o[...] = v
    kk_o[...] = kk
    ka_o[...] = kk * a
    g_o[...] = _dot(_sigmoid(x[:, RW_GL:RW_GL + GATE_LORA]), g2_ref[...])
    bonus_o[...] = _blocksum(r * kmod * rk_ref[...], bd) * v


def _rwkv_prep(u, row0, nseq, seqlen, p, grid_rows=None):
    nrows = nseq * seqlen
    colmajor = grid_rows is not None
    if colmajor:
        assert row0 == 0 and grid_rows % SUBLANE == 0 and seqlen == grid_rows * GRID_W
        ncg, rb = GRID_W // SUBLANE, grid_rows // SUBLANE
        tq = grid_rows * SUBLANE
        u = u.reshape(u.shape[0] // GRID_W, GRID_W, U_RWKV_W)
        halo = (SUBLANE, SUBLANE, U_RWKV_W)
        cur = pl.BlockSpec((grid_rows, SUBLANE, U_RWKV_W), lambda i: (i // ncg, i % ncg, 0))
        prev = pl.BlockSpec(halo, lambda i: ((i // ncg) * rb + rb - 1, jnp.maximum(i % ncg - 1, 0), 0))
        nxt = pl.BlockSpec(halo, lambda i: ((i // ncg) * rb, jnp.minimum(i % ncg + 1, ncg - 1), 0))
    else:
        tq = _tile(seqlen, SEQ_ROWS)
        cur, prev, nxt = _halo_specs(tq, U_RWKV_W, row0 // tq, 0, u.shape[0] // SUBLANE)
    consts = [p["mu"], p["w0"], p["w2"], p["a0"], p["a2"], p["g2"], p["k_k"], p["k_a"], p["r_k"], p["bd"]]
    tok = pl.BlockSpec((tq, RWKV_DIM), lambda i: (i, 0))
    return pl.pallas_call(
        functools.partial(_rwkv_prep_kernel, tq=tq, tiles_per_seq=seqlen // tq, colmajor=colmajor),
        grid=(nrows // tq,),
        in_specs=[cur, prev, nxt] + [_const_spec(c.shape) for c in consts],
        out_specs=[pl.BlockSpec((tq, RP_N * RWKV_DIM), lambda i: (i, 0)), tok, tok],
        out_shape=[jax.ShapeDtypeStruct((nrows, RP_N * RWKV_DIM), F32)]
        + [jax.ShapeDtypeStruct((nrows, RWKV_DIM), F32)] * 2,
        scratch_shapes=[pltpu.VMEM((tq + 2 * SUBLANE, U_RWKV_W), F32)],
        compiler_params=_params("parallel"),
    )(u, u, u, *consts)


def _rwkv_scan_kernel(*refs, C, sub, need_y):
    packed = refs[0:2]
    s0_refs = refs[2:4]
    if need_y:
        y_refs, sfin_refs, st = refs[4:6], refs[6:8], refs[8]
    else:
        sfin_refs, st = refs[4:6], refs[6]

    @pl.when(pl.program_id(1) == 0)
    def _():
        st[0] = s0_refs[0][...]
        st[1] = s0_refs[1][...]

    n2 = 2 * C
    row = lax.broadcasted_iota(jnp.int32, (n2, n2), 0)
    col = lax.broadcasted_iota(jnp.int32, (n2, n2), 1)
    t, s = row & (C - 1), col & (C - 1)
    strict = ((s < t), (s > t))
    incl = ((s <= t), (s >= t))
    eye = jnp.where(row == col, 1.0, 0.0)
    ti = lax.broadcasted_iota(jnp.int32, (C, C), 0)
    tk = lax.broadcasted_iota(jnp.int32, (C, C), 1)
    tri = (jnp.where(tk <= ti, 1.0, 0.0).astype(BF16), jnp.where(tk >= ti, 1.0, 0.0).astype(BF16))
    head0 = lax.broadcasted_iota(jnp.int32, (C, LANE), 1) < RWKV_HEAD_DIM

    def stack(x):
        return jnp.concatenate([jnp.where(head0, x, 0.0), jnp.where(head0, 0.0, x)], axis=0)

    probs = [(d, p) for d in (0, 1) for p in range(RWKV_PAIRS)]
    blocks = ((RP_R, RP_LWF, RP_K, RP_V, RP_KK, RP_KA), (RP_R, RP_LWB, RP_K, RP_V, RP_KK, RP_KA))
    for step in range(sub):
        _rwkv_chunk(packed, y_refs if need_y else None, st, ((step * C), (sub - 1 - step) * C), C, probs, blocks,
                    strict, incl, eye, tri, stack)
    sfin_refs[0][...] = st[0]
    sfin_refs[1][...] = st[1]


def _rwkv_chunk(packed, y_refs, st, offs, C, probs, blocks, strict, incl, eye, tri, stack):
    n2 = 2 * C
    need_y = y_refs is not None
    ld = lambda d, p, i: packed[d][offs[d]:offs[d] + C,
                                   blocks[d][i] * RWKV_DIM + p * LANE:blocks[d][i] * RWKV_DIM + (p + 1) * LANE]
    lw = [ld(d, p, 1) for d, p in probs]
    cs = [_dot_exact_lhs(tri[d], x) for (d, p), x in zip(probs, lw)]
    tot = [x[C - 1:C, :] if d == 0 else x[0:1, :] for (d, p), x in zip(probs, cs)]
    ems = [jnp.exp(-x) for x in cs]
    edec = [jnp.exp(a - x) for a, x in zip(tot, cs)]
    kt = [stack(ld(d, p, 4) * jnp.exp(x - w)) for (d, p), x, w in zip(probs, cs, lw)]
    rt = [stack(ld(d, p, 0) * jnp.exp(x)) for (d, p), x in zip(probs, cs)]
    at = [stack(ld(d, p, 5) * e) for (d, p), e in zip(probs, ems)]
    kb = [stack(ld(d, p, 2) * e) for (d, p), e in zip(probs, ems)]
    at2 = [stack(ld(d, p, 5) * e) for (d, p), e in zip(probs, edec)]
    kb2 = [stack(ld(d, p, 2) * e) for (d, p), e in zip(probs, edec)]
    vs_t = [stack(ld(d, p, 3)).T for d, p in probs]
    big = [_dot_nt(jnp.concatenate([a, b], axis=0), jnp.concatenate([c, e], axis=0))
           for a, b, c, e in zip(kt, rt, at, kb)]
    npow = [-jnp.where(strict[d], x[:n2, :n2], 0.0) for (d, p), x in zip(probs, big)]
    tinv = [eye + x for x in npow]
    npow = [_dot(x, x) for x in npow]
    for step in range(int(math.log2(C)) - 1):
        if step < int(math.log2(C)) - 2:
            both = [_dot(jnp.concatenate([t, x], axis=0), x) for t, x in zip(tinv, npow)]
            tinv = [t + b[:n2] for t, b in zip(tinv, both)]
            npow = [b[n2:] for b in both]
        else:
            tinv = [t + _dot(t, x) for t, x in zip(tinv, npow)]
    state = [st[d, p] for d, p in probs]
    lhs = [jnp.concatenate(
        [jnp.concatenate([a, jnp.where(strict[d], x[:n2, n2:], 0.0)], axis=1),
         jnp.concatenate([b, jnp.where(incl[d], x[n2:, n2:], 0.0)], axis=1)], axis=0)
        for (d, p), x, a, b in zip(probs, big, kt, rt)]
    xy = [_dot_nt(a, jnp.concatenate([x, v], axis=1)) for a, x, v in zip(lhs, state, vs_t)]
    u_m = [-_dot(x, y[:n2]) for x, y in zip(tinv, xy)]
    if need_y:
        for (d, p), x, u, y in zip(probs, big, u_m, xy):
            yy = y[n2:] + _dot(jnp.where(incl[d], x[n2:, :n2], 0.0), u)
            y_refs[d][offs[d]:offs[d] + C, p * LANE:(p + 1) * LANE] = yy[:C] + yy[C:]
    for (d, p), x, a, u, v, k2, a2 in zip(probs, state, tot, u_m, vs_t, kb2, at2):
        st[d, p] = x * jnp.exp(a) + _dot(jnp.concatenate([u.T, v], axis=1), jnp.concatenate([a2, k2], axis=0))


def _rwkv_scan(packed, nseq, seqlen, s0f, s0b, need_y):
    C = RWKV_CHUNK
    sub = max(s for s in (4, 2, 1) if seqlen % (s * C) == 0)
    nc = seqlen // (sub * C)
    fwd, bwd = (lambda b, c: (b * nc + c, 0)), (lambda b, c: (b * nc + nc - 1 - c, 0))
    tok_f, tok_b = pl.BlockSpec((sub * C, RWKV_DIM), fwd), pl.BlockSpec((sub * C, RWKV_DIM), bwd)
    pk_f = pl.BlockSpec((sub * C, RP_N * RWKV_DIM), fwd)
    pk_b = pl.BlockSpec((sub * C, RP_N * RWKV_DIM), bwd)
    sshape = (RWKV_PAIRS, LANE, LANE)
    sspec = pl.BlockSpec((None,) + sshape, lambda b, c: (b, 0, 0, 0))
    sstruct = jax.ShapeDtypeStruct((nseq,) + sshape, F32)
    out_specs, out_shape = [sspec, sspec], [sstruct, sstruct]
    if need_y:
        ystruct = jax.ShapeDtypeStruct((nseq * seqlen, RWKV_DIM), F32)
        out_specs, out_shape = [tok_f, tok_b] + out_specs, [ystruct, ystruct] + out_shape
    res = pl.pallas_call(
        functools.partial(_rwkv_scan_kernel, C=C, sub=sub, need_y=need_y),
        grid=(nseq, nc),
        in_specs=[pk_f, pk_b, sspec, sspec],
        out_specs=out_specs, out_shape=out_shape,
        scratch_shapes=[pltpu.VMEM((2,) + sshape, F32)],
        compiler_params=_params("parallel", "arbitrary"),
    )(packed, packed, s0f, s0b)
    return tuple(res) if need_y else (None, None) + tuple(res)


def _rwkv_fin_kernel(yf_ref, yb_ref, bonus_ref, g_ref, lg_ref, lb_ref, bd_ref, o_ref):
    y = yf_ref[...] + yb_ref[...]
    bd = bd_ref[...]
    dv = y - _blocksum(y, bd) * (1.0 / RWKV_HEAD_DIM)
    var = _blocksum(dv * dv, bd) * (1.0 / RWKV_HEAD_DIM)
    yn = dv * lax.rsqrt(var + GN_EPS) * lg_ref[...] + lb_ref[...]
    out = (yn + bonus_ref[...]) * g_ref[...]
    if len(o_ref.shape) == 3:
        grid_rows, ncols, c = o_ref.shape
        out = jnp.swapaxes(out.reshape(ncols, grid_rows, c), 0, 1)
    o_ref[...] = out.astype(o_ref.dtype)


def _rwkv_finish(yf, yb, bonus, g, ln_g, ln_b, bd, row0, total_rows, into=None, grid_rows=None):
    nrows = yf.shape[0]
    if grid_rows is None:
        tq = _tile(nrows, ELEM_ROWS)
        blk0 = row0 // tq
        out_spec = pl.BlockSpec((tq, RWKV_DIM), lambda i: (blk0 + i, 0))
        out_shape = jax.ShapeDtypeStruct((total_rows, RWKV_DIM), BF16)
    else:
        assert row0 == 0
        ncols = BF16_SUBLANE
        tq, ncg = ncols * grid_rows, GRID_W // ncols
        out_spec = pl.BlockSpec((grid_rows, ncols, RWKV_DIM), lambda i: (i // ncg, i % ncg, 0))
        out_shape = jax.ShapeDtypeStruct((total_rows // GRID_W, GRID_W, RWKV_DIM), BF16)
        into = None if into is None else into.reshape(out_shape.shape)
    tok = pl.BlockSpec((tq, RWKV_DIM), lambda i: (i, 0))
    res = _rows_call(
        _rwkv_fin_kernel, (yf, yb, bonus, g, ln_g, ln_b, bd),
        [tok] * 4 + [_const_spec((1, RWKV_DIM))] * 2 + [_const_spec(bd.shape)],
        out_spec, out_shape, into,
        grid=(nrows // tq,),
        compiler_params=_params("parallel"),
    )
    return res.reshape(total_rows, RWKV_DIM)


def _pad_cols(w, n):
    return jnp.pad(w, [(0, 0)] * (w.ndim - 1) + [(0, n - w.shape[-1])])


def _pad_rows(w, n):
    return jnp.pad(w, [(0, 0)] * (w.ndim - 2) + [(0, n - w.shape[-2]), (0, 0)])


def kernel(x, c, ctx, c_ctx, mod_w, mod_b, norm_g, w_in, conv_w, conv_b, conv_ln_g, conv_ln_b, conv_out, ssd_conv_w, ssd_conv_b, ssd_A_log, ssd_dt_bias, ssd_D, ssd_norm_g, ssd_out, fourier_out, rwkv_mu, rwkv_w0, rwkv_w2, rwkv_a0, rwkv_a2, rwkv_g2, rwkv_k_k, rwkv_k_a, rwkv_r_k, rwkv_ln_g, rwkv_ln_b, rwkv_out, w_o, mlp_up, mlp_down):
    B, L, D = x.shape
    CL = ctx.shape[1]
    depth = mod_w.shape[0]
    rows = L // GRID_W
    n_lat, n_ctx = B * L, B * CL
    n_all = n_lat + n_ctx
    tm = _tile(math.gcd(L, n_ctx), MM_ROWS)
    tm_small = _tile(tm, RES_ROWS)
    tm_merge = _tile(tm, MERGE_ROWS)
    mod_idx = lambda t: (lambda i: jnp.minimum((i * t) // L, B))

    w_u, w_rw, w_gate = _split_w_in(w_in)
    down_w, wo_w = mlp_down.astype(BF16), w_o.astype(BF16)
    branch_w = [conv_out.astype(BF16), ssd_out.astype(BF16), fourier_out.astype(BF16), rwkv_out.astype(BF16)]
    mu_cols = jnp.zeros((depth, U_RWKV_W), F32)
    for _, dst, src, width in (p for p in _W_IN_PIECES if p[0] == "rw"):
        mu_cols = mu_cols.at[:, dst:dst + width].set(rwkv_mu[:, src - _O_RWKV:src - _O_RWKV + width])
    head_of = jnp.arange(RWKV_DIM, dtype=jnp.int32) // RWKV_HEAD_DIM
    bd = (head_of[:, None] == head_of[None, :]).astype(BF16)
    gidx = jnp.arange(FOURIER_DIM, dtype=jnp.int32)
    cc, sc_ = _dft_mats(FOURIER_GROUP_DIM, FOURIER_GROUP_DIM ** -0.5)
    same_group = (gidx[:, None] // FOURIER_GROUP_DIM) == (gidx[None, :] // FOURIER_GROUP_DIM)
    tile_g = lambda m: jnp.where(same_group, jnp.tile(m, (FOURIER_GROUPS, FOURIER_GROUPS)), 0.0)
    chan_w = jnp.concatenate([tile_g(cc), -tile_g(sc_)], axis=1).astype(BF16)
    conv_w_p = _pad_rows(conv_w, 32)
    ssd_conv_w_p = _pad_rows(ssd_conv_w, SUBLANE)
    dt_bias_p = _pad_cols(ssd_dt_bias.reshape(depth, 1, 2 * SSD_HEADS), U_DT_W)
    a_log_p = _pad_cols(ssd_A_log.reshape(depth, 1, 2 * SSD_HEADS), U_DT_W)
    d_skip = jnp.repeat(ssd_D, SSD_HEAD_DIM, axis=1).reshape(depth, 1, SSD_DIM)

    cond = jnp.concatenate([c, c_ctx[None, :], jnp.zeros((SUBLANE - (B + 1) % SUBLANE, D), F32)], axis=0)
    mod = _mod_all(cond, mod_w, mod_b)

    xs = jnp.concatenate([x.reshape(n_lat, D), ctx.reshape(n_ctx, D)], axis=0)
    ssd_zero = jnp.zeros((B, SSD_GROUPS, SSD_STATE, SSD_HEADS_PER_GROUP * SSD_HEAD_DIM), F32)
    rwkv_zero = jnp.zeros((B, RWKV_PAIRS, LANE, LANE), F32)

    mods = lambda l: tuple(mod[l, :, j * D:(j + 1) * D].reshape(-1, 1, D) for j in range(6))
    h = _norm_mod(xs, norm_g[0, 0].reshape(1, D), *mods(0)[1::-1], n_all, mod_idx(tm), tm)
    for i in range(depth):
        last = i == depth - 1
        m_out = n_lat if last else n_all
        sh1, sc1, g1, sh2, sc2, g2 = mods(i)
        ng = norm_g[i].reshape(4, 1, D)

        u = _matmul(h, w_u, n_all, tm, F32, tn_cap=2304, layer=i, w_is_nk=True)
        gates = _matmul(h, w_gate, m_out, tm, BF16, epilogue="sigmoid", tn_cap=2048, layer=i, w_is_nk=True)

        cargs = (conv_w_p[i], conv_b[i][None], conv_ln_g[i][None], conv_ln_b[i][None])
        a_conv = _conformer(u, 0, n_lat, GRID_W, *cargs, m_out)
        z = _matmul(u, chan_w, m_out, tm, BF16, a_col=U_FFT // U_FFT_W)
        a_fft = _fourier(z, 0, B, L, m_out)
        if not last:
            a_conv = _conformer(u, n_lat, n_ctx, CL, *cargs, m_out, into=a_conv)
            a_fft = _fourier(z, n_lat, B, CL, m_out, into=a_fft)

        ssd_args = (dt_bias_p[i], a_log_p[i])
        xs_c, bm_c, cm_c = _ssd_conv(u, n_lat, B, CL, ssd_conv_w_p[i], ssd_conv_b[i][None])
        yfc, sf = _ssd_scan(xs_c, bm_c, cm_c, u, n_lat, B, CL, *ssd_args, ssd_zero, False, not last)
        ybc, sb = _ssd_scan(xs_c, bm_c, cm_c, u, n_lat, B, CL, *ssd_args, ssd_zero, True, not last)
        xs_l, bm_l, cm_l = _ssd_conv(u, 0, B, L, ssd_conv_w_p[i], ssd_conv_b[i][None])
        yfl, _ = _ssd_scan(xs_l, bm_l, cm_l, u, 0, B, L, *ssd_args, sf, False, True)
        ybl, _ = _ssd_scan(xs_l, bm_l, cm_l, u, 0, B, L, *ssd_args, sb, True, True)
        a_ssd = _ssd_finish(yfl, ybl, xs_l, u, 0, d_skip[i], ssd_norm_g[i][None], m_out)
        if not last:
            a_ssd = _ssd_finish(yfc, ybc, xs_c, u, n_lat, d_skip[i], ssd_norm_g[i][None], m_out, into=a_ssd)

        rp = dict(mu=mu_cols[i][None], w0=rwkv_w0[i], w2=_pad_rows(rwkv_w2[i], LANE).astype(BF16),
                  a0=rwkv_a0[i][None], a2=_pad_rows(rwkv_a2[i], LANE).astype(BF16), g2=rwkv_g2[i].astype(BF16),
                  k_k=rwkv_k_k[i][None], k_a=rwkv_k_a[i][None], r_k=rwkv_r_k[i].reshape(1, RWKV_DIM), bd=bd)
        fin = (rwkv_ln_g[i][None], rwkv_ln_b[i][None], bd)
        u_r = _matmul(h, w_rw, n_all, tm, F32, tn_cap=2048, layer=i, w_is_nk=True)
        packed_c, g_c, bonus_c = _rwkv_prep(u_r, n_lat, B, CL, rp)
        yfc, ybc, s_f, s_b = _rwkv_scan(packed_c, B, CL, rwkv_zero, rwkv_zero, not last)
        packed, g_, bonus = _rwkv_prep(u_r, 0, B, L, rp, grid_rows=rows)
        yfl, ybl, _, _ = _rwkv_scan(packed, B, L, s_f, s_b, True)
        a_rwkv = _rwkv_finish(yfl, ybl, bonus, g_, *fin, 0, m_out, grid_rows=rows)
        if not last:
            a_rwkv = _rwkv_finish(yfc, ybc, bonus_c, g_c, *fin, n_lat, m_out, into=a_rwkv)

        m = _merge(gates, [a_conv, a_ssd, a_fft, a_rwkv], branch_w, m_out, tm_merge, i)
        xs, h2 = _matmul_residual(m, wo_w, xs, g1, ng[1], m_out, mod_idx(tm_small), tm_small,
                                  nxt=(ng[2], sc2, sh2), layer=i)

        mid = _matmul_f32w(h2, mlp_up, m_out, tm, BF16, epilogue="relu2", layer=i)
        if last:
            xs = _matmul_residual(mid, down_w, xs, g2, ng[3], m_out, mod_idx(tm_small), tm_small, layer=i)
        else:
            xs, h = _matmul_residual(mid, down_w, xs, g2, ng[3], m_out, mod_idx(tm_small), tm_small,
                                     nxt=(norm_g[i + 1, 0].reshape(1, D), *mods(i + 1)[1::-1]), layer=i)

    return xs[:n_lat].reshape(B, L, D)
```

```python
import functools
import math

import jax
import jax.numpy as jnp
import numpy as np
from jax import lax
from jax.experimental import pallas as pl
from jax.experimental.pallas import tpu as pltpu

F32 = jnp.float32
BF16 = jnp.bfloat16

GRID_W = 64
RMS_EPS = 1e-6
LN_EPS = 1e-5
GN_EPS = 64e-5
CONV_DIM = 512
CONV_WIDTH = 31
SSD_HEADS = 12
SSD_HEAD_DIM = 64
SSD_DIM = SSD_HEADS * SSD_HEAD_DIM
SSD_GROUPS = 4
SSD_HEADS_PER_GROUP = SSD_HEADS // SSD_GROUPS
SSD_STATE = 128
SSD_CONV = 5
SSD_BC = SSD_GROUPS * SSD_STATE
SSD_XBC = SSD_DIM + 2 * SSD_BC
SSD_IN = SSD_DIM + SSD_XBC + 2 * SSD_HEADS
SSD_CHUNK = 128
FOURIER_GROUPS = 4
FOURIER_GROUP_DIM = 128
FOURIER_DIM = FOURIER_GROUPS * FOURIER_GROUP_DIM
RWKV_HEADS = 8
RWKV_HEAD_DIM = 64
RWKV_DIM = RWKV_HEADS * RWKV_HEAD_DIM
RWKV_PAIRS = RWKV_HEADS // 2
DECAY_LORA = 64
ICL_LORA = 64
GATE_LORA = 128
RWKV_IN = 3 * RWKV_DIM + 2 * DECAY_LORA + ICL_LORA + GATE_LORA
RWKV_CHUNK = 64
REC_IN = SSD_IN + RWKV_IN
N_BRANCH = 4

LANE = 128
SUBLANE = 8
BF16_SUBLANE = 16
VMEM_LIMIT = 56 * 1024 * 1024
MM_ROWS = 1024
RES_ROWS = 512
MERGE_ROWS = 256
ELEM_ROWS = 512
SEQ_ROWS = 256
CONV_PAD = 16

U_XBC, U_XBC_W = 0, SSD_XBC
U_DT, U_DT_W = 1792, LANE
U_CONV, U_CONV_W = 2048, 2 * CONV_DIM
U_Z, U_Z_W = 3072, SSD_DIM
U_FFT, U_FFT_W = 4096, FOURIER_DIM
U_N = 4608
U_RWKV_W = 2048
RW_WF, RW_WB, RW_AL, RW_GL = 1536, 1664, 1792, 1920
RP_R, RP_LWF, RP_LWB, RP_K, RP_V, RP_KK, RP_KA, RP_N = 0, 1, 2, 3, 4, 5, 6, 7


def _params(*sem):
    return pltpu.CompilerParams(dimension_semantics=sem, vmem_limit_bytes=VMEM_LIMIT)


def _tile(n, cap, quantum=SUBLANE):
    if n <= cap:
        return n
    t = (cap // quantum) * quantum
    while t >= quantum:
        if n % t == 0:
            return t
        t -= quantum
    raise ValueError(f"no tile for {n} under {cap}")


def _const_spec(shape):
    nd = len(shape)
    return pl.BlockSpec(shape, lambda *_: (0,) * nd)


def _dot(a, b):
    return jnp.dot(a.astype(BF16), b.astype(BF16), preferred_element_type=F32)


def _dot_nt(a, b):
    return lax.dot_general(a.astype(BF16), b.astype(BF16), (((1,), (1,)), ((), ())),
                           preferred_element_type=F32)


def _split3(x):
    h1 = x.astype(BF16)
    r1 = x - h1.astype(F32)
    h2 = r1.astype(BF16)
    h3 = (r1 - h2.astype(F32)).astype(BF16)
    return h1, h2, h3


def _dot_exact_lhs(m, x):
    h1, h2, h3 = _split3(x)
    d = lambda h: jnp.dot(m, h, preferred_element_type=F32)
    return d(h1) + d(h2) + d(h3)


def _blocksum(x, bd):
    hi = x.astype(BF16)
    lo = (x - hi.astype(F32)).astype(BF16)
    return (jnp.dot(hi, bd, preferred_element_type=F32) + jnp.dot(lo, bd, preferred_element_type=F32))


def _sigmoid(x):
    return 1.0 / (1.0 + jnp.exp(-x))


def _softplus(x):
    return jnp.maximum(x, 0.0) + jnp.log1p(jnp.exp(-jnp.abs(x)))


def _rms(y, g):
    return y * lax.rsqrt(jnp.mean(y * y, axis=-1, keepdims=True) + RMS_EPS) * g


_O_RWKV, _O_CONV = SSD_IN, REC_IN
_O_FFT, _O_GATE = REC_IN + 2 * CONV_DIM, REC_IN + 2 * CONV_DIM + FOURIER_DIM
_O_WF = _O_RWKV + 3 * RWKV_DIM
_W_IN_PIECES = (
    ("rw", 0, _O_RWKV, 3 * RWKV_DIM), ("rw", RW_WF, _O_WF, DECAY_LORA),
    ("rw", RW_WB, _O_WF + DECAY_LORA, DECAY_LORA), ("rw", RW_AL, _O_WF + 2 * DECAY_LORA, ICL_LORA),
    ("rw", RW_GL, _O_WF + 2 * DECAY_LORA + ICL_LORA, GATE_LORA),
    ("u", U_XBC, SSD_DIM, SSD_XBC), ("u", U_DT, SSD_DIM + SSD_XBC, 2 * SSD_HEADS),
    ("u", U_CONV, _O_CONV, 2 * CONV_DIM), ("u", U_Z, 0, SSD_DIM), ("u", U_FFT, _O_FFT, FOURIER_DIM),
)


def _split_w_in(w_in):
    depth, d, _ = w_in.shape
    w_t = jnp.swapaxes(w_in, 1, 2)

    def assemble(name, n_rows):
        parts, pos = [], 0
        for _, dst, src, width in sorted(p for p in _W_IN_PIECES if p[0] == name):
            if dst > pos:
                parts.append(jnp.zeros((depth, dst - pos, d), w_in.dtype))
            parts.append(w_t[:, src:src + width, :])
            pos = dst + width
        if n_rows > pos:
            parts.append(jnp.zeros((depth, n_rows - pos, d), w_in.dtype))
        return jnp.concatenate(parts, axis=1).astype(BF16)

    return assemble("u", U_N), assemble("rw", U_RWKV_W), w_t[:, _O_GATE:, :].astype(BF16)


def _mod_kernel(a_ref, w_ref, b_ref, o_ref):
    a = a_ref[...]
    a = a * _sigmoid(a)
    o_ref[...] = _dot(a, w_ref[...]) + b_ref[...]


def _mod_all(cond, mod_w, mod_b):
    depth, d, n = mod_w.shape
    tn = _tile(n, 1024, LANE)
    return pl.pallas_call(
        _mod_kernel,
        grid=(depth, n // tn),
        in_specs=[_const_spec(cond.shape),
                  pl.BlockSpec((None, d, tn), lambda l, j: (l, 0, j)),
                  pl.BlockSpec((None, 1, tn), lambda l, j: (l, 0, j))],
        out_specs=pl.BlockSpec((None, cond.shape[0], tn), lambda l, j: (l, 0, j)),
        out_shape=jax.ShapeDtypeStruct((depth, cond.shape[0], n), F32),
        compiler_params=_params("parallel", "parallel"),
    )(cond, mod_w, mod_b.reshape(depth, 1, n))


def _norm_mod_kernel(x_ref, g_ref, sc_ref, sh_ref, o_ref):
    o_ref[...] = (_rms(x_ref[...], g_ref[...]) * (1.0 + sc_ref[...]) + sh_ref[...]).astype(o_ref.dtype)


def _norm_mod(x, g, sc, sh, m_rows, mod_idx, tm):
    d = x.shape[1]
    return pl.pallas_call(
        _norm_mod_kernel,
        grid=(m_rows // tm,),
        in_specs=[pl.BlockSpec((tm, d), lambda i: (i, 0)),
                  _const_spec((1, d)),
                  pl.BlockSpec((None, 1, d), lambda i: (mod_idx(i), 0, 0)),
                  pl.BlockSpec((None, 1, d), lambda i: (mod_idx(i), 0, 0))],
        out_specs=pl.BlockSpec((tm, d), lambda i: (i, 0)),
        out_shape=jax.ShapeDtypeStruct((m_rows, d), BF16),
        compiler_params=_params("parallel"),
    )(x, g, sc, sh)


def _mm_kernel(a_ref, w_ref, o_ref, *, epilogue, w_is_nk=False):
    acc = _dot_nt(a_ref[...], w_ref[...]) if w_is_nk else _dot(a_ref[...], w_ref[...])
    if epilogue == "relu2":
        acc = jnp.square(jnp.maximum(acc, 0.0))
    elif epilogue == "sigmoid":
        acc = _sigmoid(acc)
    o_ref[...] = acc.astype(o_ref.dtype)


def _layer_spec(w, layer, block, index):
    if w.ndim == 2:
        return pl.BlockSpec(block, index)
    return pl.BlockSpec((None,) + block, lambda *ids: (layer,) + index(*ids))


def _matmul(a, w, m_rows, tm, out_dtype, epilogue=None, a_col=0, a_row=0, tn_cap=1024, layer=0, w_is_nk=False):
    n, k = w.shape[-2:] if w_is_nk else w.shape[-2:][::-1]
    tn = _tile(n, tn_cap, LANE)
    w_spec = (_layer_spec(w, layer, (tn, k), lambda i, j: (j, 0)) if w_is_nk
              else _layer_spec(w, layer, (k, tn), lambda i, j: (0, j)))
    return pl.pallas_call(
        functools.partial(_mm_kernel, epilogue=epilogue, w_is_nk=w_is_nk),
        grid=(m_rows // tm, n // tn),
        in_specs=[pl.BlockSpec((tm, k), lambda i, j: (a_row + i, a_col)), w_spec],
        out_specs=pl.BlockSpec((tm, tn), lambda i, j: (i, j)),
        out_shape=jax.ShapeDtypeStruct((m_rows, n), out_dtype),
        compiler_params=_params("parallel", "arbitrary"),
    )(a, w)


def _mm_wcast_kernel(a_ref, w_ref, o_ref, wb_ref, *, epilogue):
    @pl.when(pl.program_id(1) == 0)
    def _():
        wb_ref[...] = w_ref[...].astype(wb_ref.dtype)

    _mm_kernel(a_ref, wb_ref, o_ref, epilogue=epilogue)


def _matmul_f32w(a, w, m_rows, tm, out_dtype, epilogue=None, tn_cap=1024, layer=0):
    k, n = w.shape[-2:]
    tn = _tile(n, tn_cap, LANE)
    return pl.pallas_call(
        functools.partial(_mm_wcast_kernel, epilogue=epilogue),
        grid=(n // tn, m_rows // tm),
        in_specs=[pl.BlockSpec((tm, k), lambda j, i: (i, 0)),
                  _layer_spec(w, layer, (k, tn), lambda j, i: (0, j))],
        out_specs=pl.BlockSpec((tm, tn), lambda j, i: (i, j)),
        out_shape=jax.ShapeDtypeStruct((m_rows, n), out_dtype),
        scratch_shapes=[pltpu.VMEM((k, tn), BF16)],
        compiler_params=_params("parallel", "arbitrary"),
    )(a, w)


def _mm_res_kernel(a_ref, w_ref, x_ref, gate_ref, g_ref, *rest, with_next, nk):
    if with_next:
        g2_ref, sc_ref, sh_ref, o_ref, h_ref = rest
    else:
        (o_ref,) = rest
    def finish(y):
        xn = x_ref[...] + gate_ref[...] * _rms(y, g_ref[...])
        o_ref[...] = xn
        if with_next:
            h_ref[...] = (_rms(xn, g2_ref[...]) * (1.0 + sc_ref[...]) + sh_ref[...]).astype(h_ref.dtype)

    if nk == 1:
        finish(_dot(a_ref[...], w_ref[...]))
        return
    k = pl.program_id(1)

    @pl.when(k == 0)
    def _():
        o_ref[...] = jnp.zeros_like(o_ref)

    o_ref[...] += _dot(a_ref[...], w_ref[...])

    @pl.when(k == nk - 1)
    def _():
        finish(o_ref[...])


def _matmul_residual(a, w, x, gate, g, m_rows, mod_idx, tm, nxt=None, tk_cap=2048, layer=0):
    k, d = w.shape[-2:]
    tk = _tile(k, tk_cap, LANE)
    mod_spec = pl.BlockSpec((None, 1, d), lambda i, kk: (mod_idx(i), 0, 0))
    tile_spec = pl.BlockSpec((tm, d), lambda i, kk: (i, 0))
    in_specs = [pl.BlockSpec((tm, tk), lambda i, kk: (i, kk)), _layer_spec(w, layer, (tk, d), lambda i, kk: (kk, 0)),
                tile_spec, mod_spec, _const_spec((1, d))]
    out_specs, out_shape, args = tile_spec, jax.ShapeDtypeStruct((m_rows, d), F32), (a, w, x, gate, g)
    if nxt is not None:
        in_specs += [_const_spec((1, d)), mod_spec, mod_spec]
        out_specs, out_shape = [tile_spec, tile_spec], [out_shape, jax.ShapeDtypeStruct((m_rows, d), BF16)]
        args += tuple(nxt)
    return pl.pallas_call(
        functools.partial(_mm_res_kernel, with_next=nxt is not None, nk=k // tk),
        grid=(m_rows // tm, k // tk),
        in_specs=in_specs, out_specs=out_specs, out_shape=out_shape,
        compiler_params=_params("parallel", "arbitrary"),
    )(*args)


def _merge_kernel(gates_ref, a0, a1, a2, a3, w0, w1, w2, w3, o_ref):
    d = o_ref.shape[-1]
    m = None
    for i, (a, w) in enumerate(((a0, w0), (a1, w1), (a2, w2), (a3, w3))):
        t = gates_ref[:, i * d:(i + 1) * d].astype(F32) * _dot(a[...], w[...])
        m = t if m is None else m + t
    o_ref[...] = m.astype(o_ref.dtype)


def _merge(gates, acts, weights, m_rows, tm, layer):
    d = weights[0].shape[-1]
    return pl.pallas_call(
        _merge_kernel,
        grid=(m_rows // tm,),
        in_specs=[pl.BlockSpec((tm, N_BRANCH * d), lambda i: (i, 0))]
        + [pl.BlockSpec((tm, a.shape[1]), lambda i: (i, 0)) for a in acts]
        + [_layer_spec(w, layer, w.shape[-2:], lambda i: (0, 0)) for w in weights],
        out_specs=pl.BlockSpec((tm, d), lambda i: (i, 0)),
        out_shape=jax.ShapeDtypeStruct((m_rows, d), BF16),
        compiler_params=_params("parallel"),
    )(gates, *acts, *weights)


def _conformer_kernel(u_ref, w_ref, b_ref, lg_ref, lb_ref, o_ref, sh_ref, *, seg, nseg):
    pad = CONV_PAD
    first = pad - (CONV_WIDTH - 1) // 2
    span = seg + 2 * pad - SUBLANE
    zeros = jnp.zeros((pad, CONV_DIM), F32)
    sh_ref[0, 0:pad, :] = zeros
    sh_ref[0, pad + seg:2 * pad + seg, :] = zeros

    def body(s, carry):
        r0 = pl.multiple_of(s * seg, seg)
        u = u_ref[pl.ds(r0, seg), :]
        sh_ref[0, pad:pad + seg, :] = u[:, :CONV_DIM] * _sigmoid(u[:, CONV_DIM:])
        for r in range(1, SUBLANE):
            sh_ref[r, 0:span, :] = sh_ref[0, pl.ds(r, span), :]
        acc = jnp.zeros((seg, CONV_DIM), F32) + b_ref[...]
        for j in range(CONV_WIDTH):
            start = first + j
            acc = acc + w_ref[j:j + 1, :] * sh_ref[start % SUBLANE, pl.ds(start - start % SUBLANE, seg), :]
        mu = jnp.mean(acc, axis=-1, keepdims=True)
        dv = acc - mu
        var = jnp.mean(dv * dv, axis=-1, keepdims=True)
        v = dv * lax.rsqrt(var + LN_EPS) * lg_ref[...] + lb_ref[...]
        o_ref[pl.ds(r0, seg), :] = (v * _sigmoid(v)).astype(o_ref.dtype)
        return carry

    lax.fori_loop(0, nseg, body, 0)


def _rows_call(kernel, args, in_specs, out_spec, out_struct, into, **kw):
    if into is None:
        return pl.pallas_call(kernel, in_specs=in_specs, out_specs=out_spec, out_shape=out_struct, **kw)(*args)
    n = len(args)

    def aliased(*refs):
        kernel(*refs[:n], *refs[n + 1:])

    return pl.pallas_call(aliased, in_specs=list(in_specs) + [pl.BlockSpec(memory_space=pl.ANY)],
                          out_specs=out_spec, out_shape=out_struct, input_output_aliases={n: 0}, **kw)(*args, into)


def _conformer(u, row0, nrows, seg, conv_w, conv_b, ln_g, ln_b, total_rows, into=None):
    tq = max(seg, min(SEQ_ROWS, nrows))
    blk0 = row0 // tq
    return _rows_call(
        functools.partial(_conformer_kernel, seg=seg, nseg=tq // seg),
        (u, conv_w, conv_b, ln_g, ln_b),
        [pl.BlockSpec((tq, U_CONV_W), lambda i: (blk0 + i, U_CONV // U_CONV_W)),
         _const_spec(conv_w.shape), _const_spec((1, CONV_DIM)),
         _const_spec((1, CONV_DIM)), _const_spec((1, CONV_DIM))],
        pl.BlockSpec((tq, CONV_DIM), lambda i: (blk0 + i, 0)),
        jax.ShapeDtypeStruct((total_rows, CONV_DIM), BF16), into,
        grid=(nrows // tq,),
        scratch_shapes=[pltpu.VMEM((SUBLANE, seg + 2 * CONV_PAD, CONV_DIM), F32)],
        compiler_params=_params("parallel"),
    )


def _dft_kernel(c_ref, s_ref, z_ref, o_ref):
    o_ref[...] = (jnp.dot(c_ref[...], z_ref[:, :FOURIER_DIM], preferred_element_type=F32)
                  + jnp.dot(s_ref[...], z_ref[:, FOURIER_DIM:], preferred_element_type=F32)).astype(o_ref.dtype)


def _dft_mats(n, scale):
    idx = np.arange(n, dtype=np.int64)
    ang = ((idx[:, None] * idx[None, :]) % n).astype(np.float64) * (2.0 * math.pi / n)
    return (jnp.asarray((np.cos(ang) * scale).astype(np.float32)),
            jnp.asarray((np.sin(ang) * scale).astype(np.float32)))


def _fourier(z, row0, nseq, seqlen, total_rows, into=None):
    cl, sl = _dft_mats(seqlen, seqlen ** -0.5)
    cl, sl = cl.astype(BF16), sl.astype(BF16)
    tm = _tile(seqlen, ELEM_ROWS)
    nt = seqlen // tm
    sblk0, oblk0 = row0 // seqlen, row0 // tm
    return _rows_call(
        _dft_kernel, (cl, sl, z),
        [pl.BlockSpec((tm, seqlen), lambda i, b: (i, 0)),
         pl.BlockSpec((tm, seqlen), lambda i, b: (i, 0)),
         pl.BlockSpec((seqlen, 2 * FOURIER_DIM), lambda i, b: (sblk0 + b, 0))],
        pl.BlockSpec((tm, FOURIER_DIM), lambda i, b: (oblk0 + b * nt + i, 0)),
        jax.ShapeDtypeStruct((total_rows, FOURIER_DIM), BF16), into,
        grid=(nt, nseq),
        compiler_params=_params("parallel", "arbitrary"),
    )


def _halo_specs(tq, width, blk0, col_blk, nblk_total):
    hb = tq // SUBLANE
    cur = pl.BlockSpec((tq, width), lambda i: (blk0 + i, col_blk))
    prev = pl.BlockSpec((SUBLANE, width), lambda i: (jnp.maximum((blk0 + i) * hb - 1, 0), col_blk))
    nxt = pl.BlockSpec((SUBLANE, width),
                       lambda i: (jnp.minimum((blk0 + i + 1) * hb, nblk_total - 1), col_blk))
    return cur, prev, nxt


def _fill_halo(scr, cur_ref, prev_ref, next_ref, tq, tiles_per_seq):
    i = pl.program_id(0)
    first = (i % tiles_per_seq) == 0
    last = (i % tiles_per_seq) == tiles_per_seq - 1
    scr[0:SUBLANE, :] = jnp.where(first, 0.0, prev_ref[...])
    scr[SUBLANE:SUBLANE + tq, :] = cur_ref[...]
    scr[SUBLANE + tq:2 * SUBLANE + tq, :] = jnp.where(last, 0.0, next_ref[...])


def _ssd_conv_kernel(cur_ref, prev_ref, next_ref, w_ref, b_ref, xs_ref, bm_ref, cm_ref, scr, *, tq, tiles_per_seq):
    _fill_halo(scr, cur_ref, prev_ref, next_ref, tq, tiles_per_seq)
    half = (SSD_CONV - 1) // 2
    cw = 256
    for c0 in range(0, SSD_XBC, cw):
        acc = jnp.zeros((tq, cw), F32) + b_ref[:, c0:c0 + cw]
        for j in range(SSD_CONV):
            acc = acc + w_ref[j:j + 1, c0:c0 + cw] * scr[pl.ds(SUBLANE - half + j, tq), c0:c0 + cw]
        y = acc * _sigmoid(acc)
        if c0 < SSD_DIM:
            xs_ref[:, c0:c0 + cw] = y
        elif c0 < SSD_DIM + SSD_BC:
            bm_ref[:, c0 - SSD_DIM:c0 - SSD_DIM + cw] = y
        else:
            cm_ref[:, c0 - SSD_DIM - SSD_BC:c0 - SSD_DIM - SSD_BC + cw] = y


def _ssd_conv(u, row0, nseq, seqlen, conv_w, conv_b):
    nrows = nseq * seqlen
    tq = _tile(seqlen, SEQ_ROWS)
    cur, prev, nxt = _halo_specs(tq, U_XBC_W, row0 // tq, U_XBC // U_XBC_W, u.shape[0] // SUBLANE)
    outs = [jax.ShapeDtypeStruct((nrows, w), F32) for w in (SSD_DIM, SSD_BC, SSD_BC)]
    return pl.pallas_call(
        functools.partial(_ssd_conv_kernel, tq=tq, tiles_per_seq=seqlen // tq),
        grid=(nrows // tq,),
        in_specs=[cur, prev, nxt, _const_spec(conv_w.shape), _const_spec((1, SSD_XBC))],
        out_specs=[pl.BlockSpec((tq, w), lambda i: (i, 0)) for w in (SSD_DIM, SSD_BC, SSD_BC)],
        out_shape=outs,
        scratch_shapes=[pltpu.VMEM((tq + 2 * SUBLANE, U_XBC_W), F32)],
        compiler_params=_params("parallel"),
    )(u, u, u, conv_w, conv_b)


def _ssd_scan_kernel(xs_ref, b_ref, c_ref, dt_ref, bias_ref, alog_ref, sel_ref, s0_ref, *rest,
                     q, sub, rev, direction, need_y):
    if need_y:
        y_ref, sfin_ref, st = rest
    else:
        sfin_ref, st = rest

    @pl.when(pl.program_id(1) == 0)
    def _():
        st[...] = s0_ref[...]

    hp, gw = SSD_HEAD_DIM, SSD_HEADS_PER_GROUP * SSD_HEAD_DIM
    ti = lax.broadcasted_iota(jnp.int32, (q, q), 0)
    tk = lax.broadcasted_iota(jnp.int32, (q, q), 1)
    before = (tk >= ti) if rev else (tk <= ti)
    tri = jnp.where(before, 1.0, 0.0).astype(BF16)
    neg_a = -jnp.exp(alog_ref[...])
    sel = sel_ref[...]
    spread = lambda parts: sum(jnp.dot(p, sel, preferred_element_type=F32) for p in parts)
    lane0 = lax.broadcasted_iota(jnp.int32, (q, LANE), 1) < hp

    for step in range(sub):
        rows = slice((sub - 1 - step) * q if rev else step * q, (sub - step) * q if rev else (step + 1) * q)
        dt = _softplus(dt_ref[rows, :] + bias_ref[...])
        cs = _dot_exact_lhs(tri, dt * neg_a)
        dt_b = spread(_split3(dt)[:2])
        cs_b = spread(_split3(cs))
        tot_b = cs_b[0:1, :] if rev else cs_b[q - 1:q, :]
        xd = xs_ref[rows, :] * dt_b
        xdd = xd * jnp.exp(tot_b - cs_b)
        dec = jnp.exp(tot_b)
        if need_y:
            ecs = jnp.exp(cs_b)
            cs_t = cs.T
        for g in range(SSD_GROUPS):
            cg = c_ref[rows, g * SSD_STATE:(g + 1) * SSD_STATE]
            bg = b_ref[rows, g * SSD_STATE:(g + 1) * SSD_STATE]
            state = st[g]
            if need_y:
                cb = _dot_nt(cg, bg)
                y_off = _dot(cg, state) * ecs[:, g * gw:(g + 1) * gw]
                for j in range(SSD_HEADS_PER_GROUP):
                    h = g * SSD_HEADS_PER_GROUP + j
                    col = direction * SSD_HEADS + h
                    pair = cs_b[:, (h // 2) * LANE:(h // 2 + 1) * LANE]
                    swapped = pltpu.roll(pair, hp, axis=1)
                    cs_col = jnp.where(lane0, pair, swapped) if h % 2 == 0 else jnp.where(lane0, swapped, pair)
                    lm = jnp.where(before, jnp.exp(jnp.minimum(cs_col - cs_t[col:col + 1, :], 0.0)), 0.0)
                    y_ref[rows, h * hp:(h + 1) * hp] = (_dot(cb * lm, xd[:, h * hp:(h + 1) * hp])
                                                        + y_off[:, j * hp:(j + 1) * hp])
            st[g] = state * dec[:, g * gw:(g + 1) * gw] + _dot(bg.T, xdd[:, g * gw:(g + 1) * gw])
    sfin_ref[...] = st[...]


def _ssd_scan(xs, bm, cm, u, row0, nseq, seqlen, dt_bias, a_log, s0, rev, need_y):
    q = SSD_CHUNK
    sub = max(s for s in (4, 2, 1) if seqlen % (s * q) == 0)
    bq = sub * q
    nc = seqlen // bq
    ublk0 = row0 // bq
    pos = (lambda c: nc - 1 - c) if rev else (lambda c: c)
    row = lambda b, c: (b * nc + pos(c), 0)
    sshape = (SSD_GROUPS, SSD_STATE, SSD_HEADS_PER_GROUP * SSD_HEAD_DIM)
    sspec = pl.BlockSpec((None,) + sshape, lambda b, c: (b, 0, 0, 0))
    head_col = int(rev) * SSD_HEADS + jnp.arange(SSD_DIM, dtype=jnp.int32) // SSD_HEAD_DIM
    sel = (jnp.arange(U_DT_W, dtype=jnp.int32)[:, None] == head_col[None, :]).astype(BF16)
    out_specs, out_shape = [sspec], [jax.ShapeDtypeStruct((nseq,) + sshape, F32)]
    if need_y:
        out_specs.insert(0, pl.BlockSpec((bq, SSD_DIM), row))
        out_shape.insert(0, jax.ShapeDtypeStruct((nseq * seqlen, SSD_DIM), F32))
    res = pl.pallas_call(
        functools.partial(_ssd_scan_kernel, q=q, sub=sub, rev=rev, direction=int(rev), need_y=need_y),
        grid=(nseq, nc),
        in_specs=[pl.BlockSpec((bq, SSD_DIM), row), pl.BlockSpec((bq, SSD_BC), row),
                  pl.BlockSpec((bq, SSD_BC), row),
                  pl.BlockSpec((bq, U_DT_W), lambda b, c: (ublk0 + b * nc + pos(c), U_DT // U_DT_W)),
                  _const_spec((1, U_DT_W)), _const_spec((1, U_DT_W)), _const_spec(sel.shape), sspec],
        out_specs=out_specs, out_shape=out_shape,
        scratch_shapes=[pltpu.VMEM(sshape, F32)],
        compiler_params=_params("parallel", "arbitrary"),
    )(xs, bm, cm, u, dt_bias, a_log, sel, s0)
    return res if need_y else (None, res[0])


def _ssd_fin_kernel(yf_ref, yb_ref, xs_ref, z_ref, d_ref, g_ref, o_ref):
    z = z_ref[...]
    y = (yf_ref[...] + yb_ref[...] + d_ref[...] * xs_ref[...]) * (z * _sigmoid(z))
    o_ref[...] = _rms(y, g_ref[...]).astype(o_ref.dtype)


def _ssd_finish(yf, yb, xs, u, row0, d_skip, norm_g, total_rows, into=None):
    nrows = yf.shape[0]
    tq = _tile(nrows, ELEM_ROWS)
    blk0 = row0 // tq
    tok = pl.BlockSpec((tq, SSD_DIM), lambda i: (i, 0))
    return _rows_call(
        _ssd_fin_kernel, (yf, yb, xs, u, d_skip, norm_g),
        [tok, tok, tok, pl.BlockSpec((tq, U_Z_W), lambda i: (blk0 + i, U_Z // U_Z_W)),
         _const_spec((1, SSD_DIM)), _const_spec((1, SSD_DIM))],
        pl.BlockSpec((tq, SSD_DIM), lambda i: (blk0 + i, 0)),
        jax.ShapeDtypeStruct((total_rows, SSD_DIM), BF16), into,
        grid=(nrows // tq,),
        compiler_params=_params("parallel"),
    )


def _rwkv_prep_kernel(cur_ref, prev_ref, next_ref, mu_ref, w0_ref, w2_ref, a0_ref, a2_ref, g2_ref,
                      kk_ref, ka_ref, rk_ref, bd_ref,
                      p_o, g_o, bonus_o, scr, *, tq, tiles_per_seq, colmajor):
    r_o, lwf_o, lwb_o, k_o, v_o, kk_o, ka_o = (
        p_o.at[:, c * RWKV_DIM:(c + 1) * RWKV_DIM] for c in (RP_R, RP_LWF, RP_LWB, RP_K, RP_V, RP_KK, RP_KA))
    if colmajor:
        i = pl.program_id(0)
        first = (i % tiles_per_seq) == 0
        last = (i % tiles_per_seq) == tiles_per_seq - 1
        u = jnp.swapaxes(cur_ref[...], 0, 1).reshape(tq, cur_ref.shape[-1])
        scr[SUBLANE:SUBLANE + tq, :] = u
        scr[SUBLANE - 1:SUBLANE, :] = jnp.where(first, 0.0, prev_ref[SUBLANE - 1, SUBLANE - 1:SUBLANE, :])
        scr[SUBLANE + tq:SUBLANE + tq + 1, :] = jnp.where(last, 0.0, next_ref[0, 0:1, :])
    else:
        _fill_halo(scr, cur_ref, prev_ref, next_ref, tq, tiles_per_seq)
        u = cur_ref[...]
    shift = 0.5 * (scr[pl.ds(SUBLANE - 1, tq), :] + scr[pl.ds(SUBLANE + 1, tq), :]) - u
    x = u + shift * mu_ref[...]
    r = x[:, 0:RWKV_DIM]
    k = x[:, RWKV_DIM:2 * RWKV_DIM]
    v = x[:, 2 * RWKV_DIM:3 * RWKV_DIM]
    bd = bd_ref[...]
    for d, (off, out) in enumerate(((RW_WF, lwf_o), (RW_WB, lwb_o))):
        lo = jnp.tanh(x[:, off:off + LANE])
        wl = -_softplus(-(w0_ref[d:d + 1, :] + _dot(lo, w2_ref[d]))) - 0.5
        out[...] = -jnp.exp(wl)
    a = _sigmoid(a0_ref[...] + _dot(x[:, RW_AL:RW_AL + LANE], a2_ref[...]))
    kk = k * kk_ref[...]
    kk = kk * lax.rsqrt(_blocksum(kk * kk, bd) + 1e-12)
    kmod = k * (1.0 + (a - 1.0) * ka_ref[...])
    r_o[...] = r
    k_o[...] = kmod
    v_o[...] = v
    kk_o[...] = kk
    ka_o[...] = kk * a
    g_o[...] = _dot(_sigmoid(x[:, RW_GL:RW_GL + GATE_LORA]), g2_ref[...])
    bonus_o[...] = _blocksum(r * kmod * rk_ref[...], bd) * v


def _rwkv_prep(u, row0, nseq, seqlen, p, grid_rows=None):
    nrows = nseq * seqlen
    colmajor = grid_rows is not None
    if colmajor:
        assert row0 == 0 and grid_rows % SUBLANE == 0 and seqlen == grid_rows * GRID_W
        ncg, rb = GRID_W // SUBLANE, grid_rows // SUBLANE
        tq = grid_rows * SUBLANE
        u = u.reshape(u.shape[0] // GRID_W, GRID_W, U_RWKV_W)
        halo = (SUBLANE, SUBLANE, U_RWKV_W)
        cur = pl.BlockSpec((grid_rows, SUBLANE, U_RWKV_W), lambda i: (i // ncg, i % ncg, 0))
        prev = pl.BlockSpec(halo, lambda i: ((i // ncg) * rb + rb - 1, jnp.maximum(i % ncg - 1, 0), 0))
        nxt = pl.BlockSpec(halo, lambda i: ((i // ncg) * rb, jnp.minimum(i % ncg + 1, ncg - 1), 0))
    else:
        tq = _tile(seqlen, SEQ_ROWS)
        cur, prev, nxt = _halo_specs(tq, U_RWKV_W, row0 // tq, 0, u.shape[0] // SUBLANE)
    consts = [p["mu"], p["w0"], p["w2"], p["a0"], p["a2"], p["g2"], p["k_k"], p["k_a"], p["r_k"], p["bd"]]
    tok = pl.BlockSpec((tq, RWKV_DIM), lambda i: (i, 0))
    return pl.pallas_call(
        functools.partial(_rwkv_prep_kernel, tq=tq, tiles_per_seq=seqlen // tq, colmajor=colmajor),
        grid=(nrows // tq,),
        in_specs=[cur, prev, nxt] + [_const_spec(c.shape) for c in consts],
        out_specs=[pl.BlockSpec((tq, RP_N * RWKV_DIM), lambda i: (i, 0)), tok, tok],
        out_shape=[jax.ShapeDtypeStruct((nrows, RP_N * RWKV_DIM), F32)]
        + [jax.ShapeDtypeStruct((nrows, RWKV_DIM), F32)] * 2,
        scratch_shapes=[pltpu.VMEM((tq + 2 * SUBLANE, U_RWKV_W), F32)],
        compiler_params=_params("parallel"),
    )(u, u, u, *consts)


def _rwkv_scan_kernel(*refs, C, sub, need_y):
    packed = refs[0:2]
    s0_refs = refs[2:4]
    if need_y:
        y_refs, sfin_refs, st = refs[4:6], refs[6:8], refs[8]
    else:
        sfin_refs, st = refs[4:6], refs[6]

    @pl.when(pl.program_id(1) == 0)
    def _():
        st[0] = s0_refs[0][...]
        st[1] = s0_refs[1][...]

    n2 = 2 * C
    row = lax.broadcasted_iota(jnp.int32, (n2, n2), 0)
    col = lax.broadcasted_iota(jnp.int32, (n2, n2), 1)
    t, s = row & (C - 1), col & (C - 1)
    strict = ((s < t), (s > t))
    incl = ((s <= t), (s >= t))
    eye = jnp.where(row == col, 1.0, 0.0)
    ti = lax.broadcasted_iota(jnp.int32, (C, C), 0)
    tk = lax.broadcasted_iota(jnp.int32, (C, C), 1)
    tri = (jnp.where(tk <= ti, 1.0, 0.0).astype(BF16), jnp.where(tk >= ti, 1.0, 0.0).astype(BF16))
    head0 = lax.broadcasted_iota(jnp.int32, (C, LANE), 1) < RWKV_HEAD_DIM

    def stack(x):
        return jnp.concatenate([jnp.where(head0, x, 0.0), jnp.where(head0, 0.0, x)], axis=0)

    probs = [(d, p) for d in (0, 1) for p in range(RWKV_PAIRS)]
    blocks = ((RP_R, RP_LWF, RP_K, RP_V, RP_KK, RP_KA), (RP_R, RP_LWB, RP_K, RP_V, RP_KK, RP_KA))
    for step in range(sub):
        _rwkv_chunk(packed, y_refs if need_y else None, st, ((step * C), (sub - 1 - step) * C), C, probs, blocks,
                    strict, incl, eye, tri, stack)
    sfin_refs[0][...] = st[0]
    sfin_refs[1][...] = st[1]


def _rwkv_chunk(packed, y_refs, st, offs, C, probs, blocks, strict, incl, eye, tri, stack):
    n2 = 2 * C
    need_y = y_refs is not None
    ld = lambda d, p, i: packed[d][offs[d]:offs[d] + C,
                                   blocks[d][i] * RWKV_DIM + p * LANE:blocks[d][i] * RWKV_DIM + (p + 1) * LANE]
    lw = [ld(d, p, 1) for d, p in probs]
    cs = [_dot_exact_lhs(tri[d], x) for (d, p), x in zip(probs, lw)]
    tot = [x[C - 1:C, :] if d == 0 else x[0:1, :] for (d, p), x in zip(probs, cs)]
    ems = [jnp.exp(-x) for x in cs]
    edec = [jnp.exp(a - x) for a, x in zip(tot, cs)]
    kt = [stack(ld(d, p, 4) * jnp.exp(x - w)) for (d, p), x, w in zip(probs, cs, lw)]
    rt = [stack(ld(d, p, 0) * jnp.exp(x)) for (d, p), x in zip(probs, cs)]
    at = [stack(ld(d, p, 5) * e) for (d, p), e in zip(probs, ems)]
    kb = [stack(ld(d, p, 2) * e) for (d, p), e in zip(probs, ems)]
    at2 = [stack(ld(d, p, 5) * e) for (d, p), e in zip(probs, edec)]
    kb2 = [stack(ld(d, p, 2) * e) for (d, p), e in zip(probs, edec)]
    vs_t = [stack(ld(d, p, 3)).T for d, p in probs]
    big = [_dot_nt(jnp.concatenate([a, b], axis=0), jnp.concatenate([c, e], axis=0))
           for a, b, c, e in zip(kt, rt, at, kb)]
    npow = [-jnp.where(strict[d], x[:n2, :n2], 0.0) for (d, p), x in zip(probs, big)]
    tinv = [eye + x for x in npow]
    npow = [_dot(x, x) for x in npow]
    for step in range(int(math.log2(C)) - 1):
        if step < int(math.log2(C)) - 2:
            both = [_dot(jnp.concatenate([t, x], axis=0), x) for t, x in zip(tinv, npow)]
            tinv = [t + b[:n2] for t, b in zip(tinv, both)]
            npow = [b[n2:] for b in both]
        else:
            tinv = [t + _dot(t, x) for t, x in zip(tinv, npow)]
    state = [st[d, p] for d, p in probs]
    lhs = [jnp.concatenate(
        [jnp.concatenate([a, jnp.where(strict[d], x[:n2, n2:], 0.0)], axis=1),
         jnp.concatenate([b, jnp.where(incl[d], x[n2:, n2:], 0.0)], axis=1)], axis=0)
        for (d, p), x, a, b in zip(probs, big, kt, rt)]
    xy = [_dot_nt(a, jnp.concatenate([x, v], axis=1)) for a, x, v in zip(lhs, state, vs_t)]
    u_m = [-_dot(x, y[:n2]) for x, y in zip(tinv, xy)]
    if need_y:
        for (d, p), x, u, y in zip(probs, big, u_m, xy):
            yy = y[n2:] + _dot(jnp.where(incl[d], x[n2:, :n2], 0.0), u)
            y_refs[d][offs[d]:offs[d] + C, p * LANE:(p + 1) * LANE] = yy[:C] + yy[C:]
    for (d, p), x, a, u, v, k2, a2 in zip(probs, state, tot, u_m, vs_t, kb2, at2):
        st[d, p] = x * jnp.exp(a) + _dot(jnp.concatenate([u.T, v], axis=1), jnp.concatenate([a2, k2], axis=0))


def _rwkv_scan(packed, nseq, seqlen, s0f, s0b, need_y):
    C = RWKV_CHUNK
    sub = max(s for s in (4, 2, 1) if seqlen % (s * C) == 0)
    nc = seqlen // (sub * C)
    fwd, bwd = (lambda b, c: (b * nc + c, 0)), (lambda b, c: (b * nc + nc - 1 - c, 0))
    tok_f, tok_b = pl.BlockSpec((sub * C, RWKV_DIM), fwd), pl.BlockSpec((sub * C, RWKV_DIM), bwd)
    pk_f = pl.BlockSpec((sub * C, RP_N * RWKV_DIM), fwd)
    pk_b = pl.BlockSpec((sub * C, RP_N * RWKV_DIM), bwd)
    sshape = (RWKV_PAIRS, LANE, LANE)
    sspec = pl.BlockSpec((None,) + sshape, lambda b, c: (b, 0, 0, 0))
    sstruct = jax.ShapeDtypeStruct((nseq,) + sshape, F32)
    out_specs, out_shape = [sspec, sspec], [sstruct, sstruct]
    if need_y:
        ystruct = jax.ShapeDtypeStruct((nseq * seqlen, RWKV_DIM), F32)
        out_specs, out_shape = [tok_f, tok_b] + out_specs, [ystruct, ystruct] + out_shape
    res = pl.pallas_call(
        functools.partial(_rwkv_scan_kernel, C=C, sub=sub, need_y=need_y),
        grid=(nseq, nc),
        in_specs=[pk_f, pk_b, sspec, sspec],
        out_specs=out_specs, out_shape=out_shape,
        scratch_shapes=[pltpu.VMEM((2,) + sshape, F32)],
        compiler_params=_params("parallel", "arbitrary"),
    )(packed, packed, s0f, s0b)
    return tuple(res) if need_y else (None, None) + tuple(res)


def _rwkv_fin_kernel(yf_ref, yb_ref, bonus_ref, g_ref, lg_ref, lb_ref, bd_ref, o_ref):
    y = yf_ref[...] + yb_ref[...]
    bd = bd_ref[...]
    dv = y - _blocksum(y, bd) * (1.0 / RWKV_HEAD_DIM)
    var = _blocksum(dv * dv, bd) * (1.0 / RWKV_HEAD_DIM)
    yn = dv * lax.rsqrt(var + GN_EPS) * lg_ref[...] + lb_ref[...]
    out = (yn + bonus_ref[...]) * g_ref[...]
    if len(o_ref.shape) == 3:
        grid_rows, ncols, c = o_ref.shape
        out = jnp.swapaxes(out.reshape(ncols, grid_rows, c), 0, 1)
    o_ref[...] = out.astype(o_ref.dtype)


def _rwkv_finish(yf, yb, bonus, g, ln_g, ln_b, bd, row0, total_rows, into=None, grid_rows=None):
    nrows = yf.shape[0]
    if grid_rows is None:
        tq = _tile(nrows, ELEM_ROWS)
        blk0 = row0 // tq
        out_spec = pl.BlockSpec((tq, RWKV_DIM), lambda i: (blk0 + i, 0))
        out_shape = jax.ShapeDtypeStruct((total_rows, RWKV_DIM), BF16)
    else:
        assert row0 == 0
        ncols = BF16_SUBLANE
        tq, ncg = ncols * grid_rows, GRID_W // ncols
        out_spec = pl.BlockSpec((grid_rows, ncols, RWKV_DIM), lambda i: (i // ncg, i % ncg, 0))
        out_shape = jax.ShapeDtypeStruct((total_rows // GRID_W, GRID_W, RWKV_DIM), BF16)
        into = None if into is None else into.reshape(out_shape.shape)
    tok = pl.BlockSpec((tq, RWKV_DIM), lambda i: (i, 0))
    res = _rows_call(
        _rwkv_fin_kernel, (yf, yb, bonus, g, ln_g, ln_b, bd),
        [tok] * 4 + [_const_spec((1, RWKV_DIM))] * 2 + [_const_spec(bd.shape)],
        out_spec, out_shape, into,
        grid=(nrows // tq,),
        compiler_params=_params("parallel"),
    )
    return res.reshape(total_rows, RWKV_DIM)


def _pad_cols(w, n):
    return jnp.pad(w, [(0, 0)] * (w.ndim - 1) + [(0, n - w.shape[-1])])


def _pad_rows(w, n):
    return jnp.pad(w, [(0, 0)] * (w.ndim - 2) + [(0, n - w.shape[-2]), (0, 0)])


def kernel(x, c, ctx, c_ctx, mod_w, mod_b, norm_g, w_in, conv_w, conv_b, conv_ln_g, conv_ln_b, conv_out, ssd_conv_w, ssd_conv_b, ssd_A_log, ssd_dt_bias, ssd_D, ssd_norm_g, ssd_out, fourier_out, rwkv_mu, rwkv_w0, rwkv_w2, rwkv_a0, rwkv_a2, rwkv_g2, rwkv_k_k, rwkv_k_a, rwkv_r_k, rwkv_ln_g, rwkv_ln_b, rwkv_out, w_o, mlp_up, mlp_down):
    B, L, D = x.shape
    CL = ctx.shape[1]
    depth = mod_w.shape[0]
    rows = L // GRID_W
    n_lat, n_ctx = B * L, B * CL
    n_all = n_lat + n_ctx
    tm = _tile(math.gcd(L, n_ctx), MM_ROWS)
    tm_small = _tile(tm, RES_ROWS)
    tm_merge = _tile(tm, MERGE_ROWS)
    mod_idx = lambda t: (lambda i: jnp.minimum((i * t) // L, B))

    w_u, w_rw, w_gate = _split_w_in(w_in)
    down_w, wo_w = mlp_down.astype(BF16), w_o.astype(BF16)
    branch_w = [conv_out.astype(BF16), ssd_out.astype(BF16), fourier_out.astype(BF16), rwkv_out.astype(BF16)]
    mu_cols = jnp.zeros((depth, U_RWKV_W), F32)
    for _, dst, src, width in (p for p in _W_IN_PIECES if p[0] == "rw"):
        mu_cols = mu_cols.at[:, dst:dst + width].set(rwkv_mu[:, src - _O_RWKV:src - _O_RWKV + width])
    head_of = jnp.arange(RWKV_DIM, dtype=jnp.int32) // RWKV_HEAD_DIM
    bd = (head_of[:, None] == head_of[None, :]).astype(BF16)
    gidx = jnp.arange(FOURIER_DIM, dtype=jnp.int32)
    cc, sc_ = _dft_mats(FOURIER_GROUP_DIM, FOURIER_GROUP_DIM ** -0.5)
    same_group = (gidx[:, None] // FOURIER_GROUP_DIM) == (gidx[None, :] // FOURIER_GROUP_DIM)
    tile_g = lambda m: jnp.where(same_group, jnp.tile(m, (FOURIER_GROUPS, FOURIER_GROUPS)), 0.0)
    chan_w = jnp.concatenate([tile_g(cc), -tile_g(sc_)], axis=1).astype(BF16)
    conv_w_p = _pad_rows(conv_w, 32)
    ssd_conv_w_p = _pad_rows(ssd_conv_w, SUBLANE)
    dt_bias_p = _pad_cols(ssd_dt_bias.reshape(depth, 1, 2 * SSD_HEADS), U_DT_W)
    a_log_p = _pad_cols(ssd_A_log.reshape(depth, 1, 2 * SSD_HEADS), U_DT_W)
    d_skip = jnp.repeat(ssd_D, SSD_HEAD_DIM, axis=1).reshape(depth, 1, SSD_DIM)

    cond = jnp.concatenate([c, c_ctx[None, :], jnp.zeros((SUBLANE - (B + 1) % SUBLANE, D), F32)], axis=0)
    mod = _mod_all(cond, mod_w, mod_b)

    xs = jnp.concatenate([x.reshape(n_lat, D), ctx.reshape(n_ctx, D)], axis=0)
    ssd_zero = jnp.zeros((B, SSD_GROUPS, SSD_STATE, SSD_HEADS_PER_GROUP * SSD_HEAD_DIM), F32)
    rwkv_zero = jnp.zeros((B, RWKV_PAIRS, LANE, LANE), F32)

    mods = lambda l: tuple(mod[l, :, j * D:(j + 1) * D].reshape(-1, 1, D) for j in range(6))
    h = _norm_mod(xs, norm_g[0, 0].reshape(1, D), *mods(0)[1::-1], n_all, mod_idx(tm), tm)
    for i in range(depth):
        last = i == depth - 1
        m_out = n_lat if last else n_all
        sh1, sc1, g1, sh2, sc2, g2 = mods(i)
        ng = norm_g[i].reshape(4, 1, D)

        u = _matmul(h, w_u, n_all, tm, F32, tn_cap=2304, layer=i, w_is_nk=True)
        gates = _matmul(h, w_gate, m_out, tm, BF16, epilogue="sigmoid", tn_cap=2048, layer=i, w_is_nk=True)

        cargs = (conv_w_p[i], conv_b[i][None], conv_ln_g[i][None], conv_ln_b[i][None])
        shared = lambda width: None if last else jnp.zeros((m_out, width), BF16)
        a_conv = _conformer(u, 0, n_lat, GRID_W, *cargs, m_out, into=shared(CONV_DIM))
        z = _matmul(u, chan_w, m_out, tm, BF16, a_col=U_FFT // U_FFT_W)
        a_fft = _fourier(z, 0, B, L, m_out, into=shared(FOURIER_DIM))
        if not last:
            a_conv = _conformer(u, n_lat, n_ctx, CL, *cargs, m_out, into=a_conv)
            a_fft = _fourier(z, n_lat, B, CL, m_out, into=a_fft)

        ssd_args = (dt_bias_p[i], a_log_p[i])
        xs_c, bm_c, cm_c = _ssd_conv(u, n_lat, B, CL, ssd_conv_w_p[i], ssd_conv_b[i][None])
        yfc, sf = _ssd_scan(xs_c, bm_c, cm_c, u, n_lat, B, CL, *ssd_args, ssd_zero, False, not last)
        ybc, sb = _ssd_scan(xs_c, bm_c, cm_c, u, n_lat, B, CL, *ssd_args, ssd_zero, True, not last)
        xs_l, bm_l, cm_l = _ssd_conv(u, 0, B, L, ssd_conv_w_p[i], ssd_conv_b[i][None])
        yfl, _ = _ssd_scan(xs_l, bm_l, cm_l, u, 0, B, L, *ssd_args, sf, False, True)
        ybl, _ = _ssd_scan(xs_l, bm_l, cm_l, u, 0, B, L, *ssd_args, sb, True, True)
        a_ssd = _ssd_finish(yfl, ybl, xs_l, u, 0, d_skip[i], ssd_norm_g[i][None], m_out, into=shared(SSD_DIM))
        if not last:
            a_ssd = _ssd_finish(yfc, ybc, xs_c, u, n_lat, d_skip[i], ssd_norm_g[i][None], m_out, into=a_ssd)

        rp = dict(mu=mu_cols[i][None], w0=rwkv_w0[i], w2=_pad_rows(rwkv_w2[i], LANE).astype(BF16),
                  a0=rwkv_a0[i][None], a2=_pad_rows(rwkv_a2[i], LANE).astype(BF16), g2=rwkv_g2[i].astype(BF16),
                  k_k=rwkv_k_k[i][None], k_a=rwkv_k_a[i][None], r_k=rwkv_r_k[i].reshape(1, RWKV_DIM), bd=bd)
        fin = (rwkv_ln_g[i][None], rwkv_ln_b[i][None], bd)
        u_r = _matmul(h, w_rw, n_all, tm, F32, tn_cap=2048, layer=i, w_is_nk=True)
        packed_c, g_c, bonus_c = _rwkv_prep(u_r, n_lat, B, CL, rp)
        yfc, ybc, s_f, s_b = _rwkv_scan(packed_c, B, CL, rwkv_zero, rwkv_zero, not last)
        packed, g_, bonus = _rwkv_prep(u_r, 0, B, L, rp, grid_rows=rows)
        yfl, ybl, _, _ = _rwkv_scan(packed, B, L, s_f, s_b, True)
        a_rwkv = _rwkv_finish(yfl, ybl, bonus, g_, *fin, 0, m_out, into=shared(RWKV_DIM), grid_rows=rows)
        if not last:
            a_rwkv = _rwkv_finish(yfc, ybc, bonus_c, g_c, *fin, n_lat, m_out, into=a_rwkv)

        m = _merge(gates, [a_conv, a_ssd, a_fft, a_rwkv], branch_w, m_out, tm_merge, i)
        xs, h2 = _matmul_residual(m, wo_w, xs, g1, ng[1], m_out, mod_idx(tm_small), tm_small,
                                  nxt=(ng[2], sc2, sh2), layer=i)

        mid = _matmul_f32w(h2, mlp_up, m_out, tm, BF16, epilogue="relu2", layer=i)
        if last:
            xs = _matmul_residual(mid, down_w, xs, g2, ng[3], m_out, mod_idx(tm_small), tm_small, layer=i)
        else:
            xs, h = _matmul_residual(mid, down_w, xs, g2, ng[3], m_out, mod_idx(tm_small), tm_small,
                                     nxt=(norm_g[i + 1, 0].reshape(1, D), *mods(i + 1)[1::-1]), layer=i)

    return xs[:n_lat].reshape(B, L, D)
```
